```python
import jax
import jax.numpy as jnp
from jax import lax
import numpy as np


D_MODEL = 1024
BATCH = 2
SEQ = 16384
DEPTH = 4

HEAD_DIM = 64
N_HEADS = D_MODEL // HEAD_DIM
A_KV_HEADS = 4
CMP_BLOCK = 32
CMP_STRIDE = 16
CMP_HIDDEN = 256
SEL_BLOCK = 64
N_SELECT = 16
WINDOW_A = 512
QUERY_BLOCK = 128
B_KV_HEADS = 2
WINDOW_B = 128
N_A_LAYERS = max(1, DEPTH // 2)
D_FF = 2816
CONV_WIDTH = 3
ROPE_THETA = 10000.0
EPS = 1e-6
NEG = -1e30
FORCE = 1e6
A_SPLIT = [N_HEADS * HEAD_DIM] + [A_KV_HEADS * HEAD_DIM] * 6 + [3 * N_HEADS]
A_IN_COLS = sum(A_SPLIT)

kernel_name = 'hybrid_nsa_swa_sink_yoco_convffn'


def rms_norm(x, g):
    xf = x.astype(jnp.float32)
    y = xf * lax.rsqrt(jnp.mean(xf * xf, axis=-1, keepdims=True) + EPS)
    return (y * g.astype(jnp.float32)).astype(x.dtype)


def rope(x, pos):
    half = HEAD_DIM // 2
    inv = jnp.float32(ROPE_THETA) ** (-jnp.arange(half, dtype=jnp.float32) / half)
    ang = pos.astype(jnp.float32)[:, None] * inv[None, :]
    cos = jnp.cos(ang)[None, :, None, :]
    sin = jnp.sin(ang)[None, :, None, :]
    xf = x.astype(jnp.float32)
    x1, x2 = xf[..., :half], xf[..., half:]
    return jnp.concatenate([x1 * cos - x2 * sin, x2 * cos + x1 * sin], axis=-1).astype(x.dtype)


def compress(k, pos_emb, w1, w2):
    b, s, g, dh = k.shape
    nseg = s // CMP_STRIDE
    r = CMP_BLOCK // CMP_STRIDE
    nc = nseg - r + 1
    seg = k.reshape(b, nseg, CMP_STRIDE, g, dh)
    blocks = jnp.concatenate([seg[:, i:i + nc] for i in range(r)], axis=2)
    blocks = blocks + pos_emb[None, None, :, None, :]
    flat = blocks.transpose(0, 1, 3, 2, 4).reshape(b, nc, g, CMP_BLOCK * dh)
    return jax.nn.gelu(flat @ w1) @ w2


def _sel_weights():
    r_c = CMP_BLOCK // CMP_STRIDE
    r_s = SEL_BLOCK // CMP_STRIDE
    ws = []
    for o in range(-(r_c - 1), r_s):
        lo = max(o * CMP_STRIDE, 0)
        hi = min(o * CMP_STRIDE + CMP_BLOCK, SEL_BLOCK)
        ws.append(max(hi - lo, 0) // CMP_STRIDE)
    return ws


def block_importance(p, n_sel):
    r_c = CMP_BLOCK // CMP_STRIDE
    r_s = SEL_BLOCK // CMP_STRIDE
    ws = _sel_weights()
    front = r_c - 1
    span = r_s * (n_sel - 1) + 1
    back = max(len(ws) - 1 + span - front - p.shape[-1], 0)
    pp = jnp.pad(p, [(0, 0)] * (p.ndim - 1) + [(front, back)])
    out = ws[0] * pp[..., 0:span:r_s]
    for i in range(1, len(ws)):
        out = out + ws[i] * pp[..., i:i + span:r_s]
    return out


def nsa_mixer(h, w_in, cmp_pos, cmp_w1, cmp_w2, w_out):
    b, s, _ = h.shape
    g_n = A_KV_HEADS
    r_n = N_HEADS // g_n
    dh = HEAD_DIM
    split_idx = np.cumsum(A_SPLIT)[:-1].tolist()
    q, kc, vc, ks, vs, kw, vw, gl = jnp.split(h @ w_in, split_idx, axis=-1)
    pos = jnp.arange(s)
    q = rope(q.reshape(b, s, N_HEADS, dh), pos)
    kc = kc.reshape(b, s, g_n, dh)
    vc = vc.reshape(b, s, g_n, dh)
    ks = rope(ks.reshape(b, s, g_n, dh), pos)
    vs = vs.reshape(b, s, g_n, dh)
    kw = rope(kw.reshape(b, s, g_n, dh), pos)
    vw = vw.reshape(b, s, g_n, dh)
    kc_c = compress(kc, cmp_pos[0], cmp_w1[0], cmp_w2[0])
    vc_c = compress(vc, cmp_pos[1], cmp_w1[1], cmp_w2[1])
    nc = kc_c.shape[1]
    cmp_end = jnp.arange(nc) * CMP_STRIDE + CMP_BLOCK - 1
    kc_c = rope(kc_c, cmp_end)
    gates = jax.nn.sigmoid(gl.astype(jnp.float32)).astype(h.dtype)
    gh = gates.reshape(b, s, g_n, r_n, 3).transpose(0, 2, 3, 1, 4)
    qh = q.reshape(b, s, g_n, r_n, dh).transpose(0, 2, 3, 1, 4)
    kc_h = kc_c.transpose(0, 2, 1, 3)
    vc_h = vc_c.transpose(0, 2, 1, 3)
    n_sel = s // SEL_BLOCK
    ks_b = ks.transpose(0, 2, 1, 3).reshape(b, g_n, n_sel, SEL_BLOCK, dh)
    vs_b = vs.transpose(0, 2, 1, 3).reshape(b, g_n, n_sel, SEL_BLOCK, dh)
    pad_w = ((0, 0), (0, 0), (WINDOW_A, 0), (0, 0))
    kw_p = jnp.pad(kw.transpose(0, 2, 1, 3), pad_w)
    vw_p = jnp.pad(vw.transpose(0, 2, 1, 3), pad_w)
    top = min(N_SELECT, n_sel)
    scale = dh ** -0.5
    bi = jnp.arange(b)[:, None, None, None]
    gi = jnp.arange(g_n)[None, :, None, None]
    sel_j = jnp.arange(n_sel)
    sel_c = jnp.arange(SEL_BLOCK)

    def one_block(qb):
        start = qb * QUERY_BLOCK
        t = start + jnp.arange(QUERY_BLOCK)
        qq = lax.dynamic_slice_in_dim(qh, start, QUERY_BLOCK, axis=3)
        gg = lax.dynamic_slice_in_dim(gh, start, QUERY_BLOCK, axis=3)
        sc = jnp.einsum('bgrqd,bgnd->bgrqn', qq, kc_h).astype(jnp.float32) * scale
        valid = cmp_end[None, :] <= t[:, None]
        p_c = jax.nn.softmax(jnp.where(valid, sc, NEG), axis=-1) * valid
        o_c = jnp.einsum('bgrqn,bgnd->bgrqd', p_c.astype(vc_h.dtype), vc_h)
        imp = block_importance(p_c.sum(axis=2), n_sel)
        cur = t // SEL_BLOCK
        forced = (sel_j[None, :] == 0) | (sel_j[None, :] == cur[:, None]) | (sel_j[None, :] == cur[:, None] - 1)
        causal_blk = sel_j[None, :] * SEL_BLOCK <= t[:, None]
        imp = jnp.where(forced, FORCE, jnp.where(causal_blk, imp, -FORCE))
        _, idx = lax.top_k(imp, top)
        k_sel = ks_b[bi, gi, idx]
        v_sel = vs_b[bi, gi, idx]
        kpos = idx[..., None] * SEL_BLOCK + sel_c
        m_sel = (kpos <= t[:, None, None])[:, :, None]
        ss = jnp.einsum('bgrqd,bgqkcd->bgrqkc', qq, k_sel).astype(jnp.float32) * scale
        ss = jnp.where(m_sel, ss, NEG).reshape(b, g_n, r_n, QUERY_BLOCK, top * SEL_BLOCK)
        p_s = jax.nn.softmax(ss, axis=-1).reshape(b, g_n, r_n, QUERY_BLOCK, top, SEL_BLOCK)
        o_s = jnp.einsum('bgrqkc,bgqkcd->bgrqd', p_s.astype(v_sel.dtype), v_sel)
        kwb = lax.dynamic_slice_in_dim(kw_p, start, QUERY_BLOCK + WINDOW_A, axis=2)
        vwb = lax.dynamic_slice_in_dim(vw_p, start, QUERY_BLOCK + WINDOW_A, axis=2)
        wpos = start - WINDOW_A + jnp.arange(QUERY_BLOCK + WINDOW_A)
        dist = t[:, None] - wpos[None, :]
        m_w = (dist >= 0) & (dist < WINDOW_A) & (wpos[None, :] >= 0)
        sw = jnp.einsum('bgrqd,bgkd->bgrqk', qq, kwb).astype(jnp.float32) * scale
        p_w = jax.nn.softmax(jnp.where(m_w, sw, NEG), axis=-1)
        o_w = jnp.einsum('bgrqk,bgkd->bgrqd', p_w.astype(vwb.dtype), vwb)
        return gg[..., 0:1] * o_c + gg[..., 1:2] * o_s + gg[..., 2:3] * o_w

    out = lax.map(one_block, jnp.arange(s // QUERY_BLOCK))
    out = out.transpose(1, 0, 4, 2, 3, 5).reshape(b, s, N_HEADS * dh)
    return out @ w_out


def shared_kv(h, kv_norm, w_kv):
    b, s, _ = h.shape
    k, v = jnp.split(rms_norm(h, kv_norm) @ w_kv, 2, axis=-1)
    k = rope(k.reshape(b, s, B_KV_HEADS, HEAD_DIM), jnp.arange(s))
    v = v.reshape(b, s, B_KV_HEADS, HEAD_DIM)
    return k, v


def _with_prev_block(xb):
    pad = [(0, 0), (1, 0)] + [(0, 0)] * (xb.ndim - 2)
    return jnp.concatenate([jnp.pad(xb, pad)[:, :-1], xb], axis=2)


def swa_sink_mixer(h, k, v, w_q, sinks, w_out):
    b, s, _ = h.shape
    g_n = B_KV_HEADS
    r_n = N_HEADS // g_n
    w = WINDOW_B
    nb = s // w
    q = rope((h @ w_q).reshape(b, s, N_HEADS, HEAD_DIM), jnp.arange(s))
    qb = q.reshape(b, nb, w, g_n, r_n, HEAD_DIM)
    kk = _with_prev_block(k.reshape(b, nb, w, g_n, HEAD_DIM))
    vv = _with_prev_block(v.reshape(b, nb, w, g_n, HEAD_DIM))
    sc = jnp.einsum('bnqgrd,bnkgd->bgrnqk', qb, kk).astype(jnp.float32) * (HEAD_DIM ** -0.5)
    qi = jnp.arange(w)[:, None]
    ki = jnp.arange(2 * w)[None, :]
    rel = qi + w - ki
    band = (rel >= 0) & (rel < w)
    exists = (jnp.arange(nb)[:, None, None] > 0) | (ki[None] >= w)
    mask = band[None] & exists
    sc = jnp.where(mask, sc, NEG)
    sink = sinks.astype(jnp.float32).reshape(1, g_n, r_n, 1, 1, 1)
    mx = jnp.maximum(sc.max(axis=-1, keepdims=True), sink)
    e = jnp.exp(sc - mx)
    p = e / (e.sum(axis=-1, keepdims=True) + jnp.exp(sink - mx))
    o = jnp.einsum('bgrnqk,bnkgd->bnqgrd', p.astype(vv.dtype), vv)
    return o.reshape(b, s, N_HEADS * HEAD_DIM) @ w_out


def conv_ffn(h, w_in, conv_w, conv_b, w_out):
    s = h.shape[1]
    u = h @ w_in
    up = jnp.pad(u, ((0, 0), (CONV_WIDTH - 1, 0), (0, 0)))
    c = conv_b + conv_w[0] * up[:, 0:s]
    for i in range(1, CONV_WIDTH):
        c = c + conv_w[i] * up[:, i:i + s]
    a, gate_in = jnp.split(c, 2, axis=-1)
    return (jax.nn.silu(a) * gate_in) @ w_out


def setup_inputs(seed: int = 0) -> dict:
    key = jax.random.key(seed)
    ks = jax.random.split(key, 18)
    d = D_MODEL
    n_a = N_A_LAYERS
    n_b = DEPTH - N_A_LAYERS
    f32 = jnp.float32

    def w(k, shape, fan_in):
        return jax.random.normal(k, shape, f32) * (fan_in ** -0.5)

    def gain(k, shape):
        return 1.0 + 0.02 * jax.random.normal(k, shape, f32)

    return {
        'x': jax.random.normal(ks[0], (BATCH, SEQ, d), f32),
        'norm_attn': gain(ks[1], (DEPTH, d)),
        'norm_ffn': gain(ks[2], (DEPTH, d)),
        'a_w_in': w(ks[3], (n_a, d, A_IN_COLS), d),
        'a_cmp_pos': 0.1 * jax.random.normal(ks[4], (n_a, 2, CMP_BLOCK, HEAD_DIM), f32),
        'a_cmp_w1': w(ks[5], (n_a, 2, CMP_BLOCK * HEAD_DIM, CMP_HIDDEN), CMP_BLOCK * HEAD_DIM),
        'a_cmp_w2': w(ks[6], (n_a, 2, CMP_HIDDEN, HEAD_DIM), CMP_HIDDEN),
        'a_w_out': w(ks[7], (n_a, d, d), d),
        'kv_norm': gain(ks[8], (d,)),
        'b_w_kv': w(ks[9], (d, 2 * B_KV_HEADS * HEAD_DIM), d),
        'b_w_q': w(ks[10], (n_b, d, d), d),
        'b_sinks': jax.random.normal(ks[11], (n_b, N_HEADS), f32),
        'b_w_out': w(ks[12], (n_b, d, d), d),
        'ffn_w_in': w(ks[13], (DEPTH, d, 2 * D_FF), d),
        'ffn_conv_w': w(ks[14], (DEPTH, CONV_WIDTH, 2 * D_FF), CONV_WIDTH),
        'ffn_conv_b': 0.02 * jax.random.normal(ks[15], (DEPTH, 2 * D_FF), f32),
        'ffn_w_out': w(ks[16], (DEPTH, D_FF, d), D_FF),
        'final_norm': gain(ks[17], (d,)),
    }


def reference(x, norm_attn, norm_ffn, a_w_in, a_cmp_pos, a_cmp_w1, a_cmp_w2, a_w_out, kv_norm, b_w_kv, b_w_q, b_sinks, b_w_out, ffn_w_in, ffn_conv_w, ffn_conv_b, ffn_w_out, final_norm):
    h = x
    k_sh = None
    v_sh = None
    for layer in range(DEPTH):
        hn = rms_norm(h, norm_attn[layer])
        if layer < N_A_LAYERS:
            h = h + nsa_mixer(hn, a_w_in[layer], a_cmp_pos[layer], a_cmp_w1[layer], a_cmp_w2[layer], a_w_out[layer])
        else:
            if layer == N_A_LAYERS:
                k_sh, v_sh = shared_kv(h, kv_norm, b_w_kv)
                hn = rms_norm(h, norm_attn[layer])
            j = layer - N_A_LAYERS
            h = h + swa_sink_mixer(hn, k_sh, v_sh, b_w_q[j], b_sinks[j], b_w_out[j])
        h = h + conv_ffn(rms_norm(h, norm_ffn[layer]), ffn_w_in[layer], ffn_conv_w[layer], ffn_conv_b[layer], ffn_w_out[layer])
    return rms_norm(h, final_norm)
```

```python
import functools
import math

import jax
import jax.numpy as jnp
from jax import lax
from jax.experimental import pallas as pl
from jax.experimental.pallas import tpu as pltpu

F32 = jnp.float32
BF16 = jnp.bfloat16

HEAD_DIM = 64
HALF = HEAD_DIM // 2
N_HEADS = 16
A_KV_HEADS = 4
B_KV_HEADS = 2
CMP_BLOCK = 32
CMP_STRIDE = 16
SEL_BLOCK = 64
N_SELECT = 16
WINDOW_A = 512
WINDOW_B = 128
CONV_WIDTH = 3
ROPE_THETA = 10000.0
EPS = 1e-6
FORCE = 1e6

LANES = 128
BF16_SUBLANES = 16
VMEM_LIMIT = 56 * 1024 * 1024
TQ = 128
TK = 256
TM = 512
FF_CHUNK = 256
VROWS = HEAD_DIM + BF16_SUBLANES
MASK_ROWS = 128

LOG2E = 1.4426950408889634
NEG = -1e30
M_INIT = -1e29
REMOVED = -3e38

_NT = (((1,), (1,)), ((), ()))


def _cparams(sem):
    return pltpu.CompilerParams(dimension_semantics=sem, vmem_limit_bytes=VMEM_LIMIT)


def _const_spec(shape):
    n = len(shape)
    return pl.BlockSpec(shape, lambda *_: (0,) * n, pipeline_mode=pl.Buffered(1))


def _rms(x, g):
    ms = jnp.mean(x * x, axis=-1, keepdims=True)
    return x * lax.rsqrt(ms + EPS) * g


def _proj_body(*refs, nat_plan, tr_plan, tm):
    it = iter(refs)
    h_ref, g_ref = next(it), next(it)
    if nat_plan:
        cn_ref, sn_ref, wn_ref = next(it), next(it), next(it)
    ct_ref, st_ref, wt_ref = next(it), next(it), next(it)
    outs = list(it)

    hn = _rms(h_ref[...], g_ref[...]).astype(BF16)
    oi = 0
    for kind, c0, n in nat_plan:
        o_ref = outs[oi]
        oi += 1
        y = jnp.dot(hn, wn_ref[:, c0:c0 + n], preferred_element_type=F32)
        if kind == "rope":
            yr = jnp.dot(hn, wn_ref[:, c0 + n:c0 + 2 * n], preferred_element_type=F32)
            c, s = cn_ref[...], sn_ref[...]
            for g in range(n // LANES):
                sl = slice(LANES * g, LANES * (g + 1))
                o_ref[:, sl] = (y[:, sl] * c + yr[:, sl] * s).astype(o_ref.dtype)
        else:
            o_ref[...] = y.astype(o_ref.dtype)
    for kind, r0, n in tr_plan:
        o_ref = outs[oi]
        oi += 1
        y = lax.dot_general(wt_ref[r0:r0 + n, :], hn, _NT, preferred_element_type=F32)
        if kind == "ropeq":
            c, s = ct_ref[...], st_ref[...]
            for hd in range(n // HEAD_DIM):
                a = HEAD_DIM * hd
                y1, y2 = y[a:a + HALF], y[a + HALF:a + HEAD_DIM]
                o_ref[a:a + HALF, :] = (y1 * c - y2 * s).astype(o_ref.dtype)
                o_ref[a + HALF:a + HEAD_DIM, :] = (y2 * c + y1 * s).astype(o_ref.dtype)
        elif kind == "sigmoid":
            o_ref[...] = jax.nn.sigmoid(y)
        else:
            rows = lax.broadcasted_iota(jnp.int32, (VROWS, tm), 0)
            for g in range(n // VROWS):
                yg = jnp.where(rows == HEAD_DIM, 1.0, y[VROWS * g:VROWS * (g + 1)]).astype(o_ref.dtype)
                for t in range(tm // TK):
                    o_ref[t, VROWS * g:VROWS * (g + 1), :] = yg[:, t * TK:(t + 1) * TK]


def _proj_call(h, gain, tabs, wn, wt, nat_plan, tr_plan, nat_dtypes, batch, seq):
    t_tokens, d = h.shape
    tm = min(TM, seq)
    ns = seq // tm
    in_specs = [pl.BlockSpec((tm, d), lambda i: (i, 0)), _const_spec((1, d))]
    args = [h, gain.reshape(1, d)]
    if nat_plan:
        in_specs += [pl.BlockSpec((tm, LANES), lambda i: (i % ns, 0)),
                     pl.BlockSpec((tm, LANES), lambda i: (i % ns, 0)),
                     _const_spec(wn.shape)]
        args += [tabs["cn"], tabs["sn"], wn]
    in_specs += [pl.BlockSpec((HALF, tm), lambda i: (0, i % ns)),
                 pl.BlockSpec((HALF, tm), lambda i: (0, i % ns)),
                 _const_spec(wt.shape)]
    args += [tabs["ct"], tabs["st"], wt]
    out_shape, out_specs = [], []
    for (kind, _, n), dt in zip(nat_plan, nat_dtypes):
        out_shape.append(jax.ShapeDtypeStruct((t_tokens, n), dt))
        out_specs.append(pl.BlockSpec((tm, n), lambda i: (i, 0)))
    for kind, _, n in tr_plan:
        if kind == "vaug":
            out_shape.append(jax.ShapeDtypeStruct((batch, seq // TK, n, TK), BF16))
            out_specs.append(pl.BlockSpec((None, tm // TK, n, TK), lambda i: (i // ns, i % ns, 0, 0)))
        else:
            dt = F32 if kind == "sigmoid" else BF16
            out_shape.append(jax.ShapeDtypeStruct((batch, n, seq), dt))
            out_specs.append(pl.BlockSpec((None, n, tm), lambda i: (i // ns, 0, i % ns)))
    return pl.pallas_call(
        functools.partial(_proj_body, nat_plan=tuple(nat_plan), tr_plan=tuple(tr_plan), tm=tm),
        grid=(t_tokens // tm,),
        in_specs=in_specs, out_specs=out_specs, out_shape=out_shape,
        compiler_params=_cparams(("parallel",)),
        name="norm_proj",
    )(*args)


def _gelu_tanh(x):
    c = math.sqrt(2.0 / math.pi)
    return x * (0.5 * (1.0 + jnp.tanh(c * (x + 0.044715 * (x * x * x)))))


def _compress_body(sk_ref, sv_ref, pos_ref, w1_ref, w2k_ref, w2v_ref, cc_ref, sc_ref, ok_ref, ov_ref, *, ncp):
    def hidden(seg, kv):
        xa = (seg + pos_ref[kv, 0]).astype(BF16)
        xb = (seg + pos_ref[kv, 1]).astype(BF16)
        a = jnp.dot(xa, w1_ref[kv, 0], preferred_element_type=F32)
        b = jnp.dot(xb, w1_ref[kv, 1], preferred_element_type=F32)
        return _gelu_tanh(a + pltpu.roll(b, ncp - 1, axis=0)).astype(BF16)

    gk = hidden(sk_ref[...], 0)
    k = jnp.dot(gk, w2k_ref[0], preferred_element_type=F32)
    kr = jnp.dot(gk, w2k_ref[1], preferred_element_type=F32)
    ok_ref[...] = (k * cc_ref[...] + kr * sc_ref[...]).astype(ok_ref.dtype)
    gv = hidden(sv_ref[...], 1)
    vt = lax.dot_general(w2v_ref[...], gv, _NT, preferred_element_type=F32)
    rows = lax.broadcasted_iota(jnp.int32, (VROWS, ncp), 0)
    ov_ref[...] = jnp.where(rows == HEAD_DIM, 1.0, vt).astype(ov_ref.dtype)


def _compress_call(segk, segv, pos, w1, w2k, w2v, cc, sc):
    b, g, ncp, f = segk.shape
    seg_spec = pl.BlockSpec((None, None, ncp, f), lambda i, j: (i, j, 0, 0))
    return pl.pallas_call(
        functools.partial(_compress_body, ncp=ncp),
        grid=(b, g),
        in_specs=[seg_spec, seg_spec, _const_spec(pos.shape), _const_spec(w1.shape), _const_spec(w2k.shape),
                  _const_spec(w2v.shape), _const_spec(cc.shape), _const_spec(sc.shape)],
        out_specs=[pl.BlockSpec((None, None, ncp, LANES), lambda i, j: (i, j, 0, 0)),
                   pl.BlockSpec((None, None, VROWS, ncp), lambda i, j: (i, j, 0, 0))],
        out_shape=[jax.ShapeDtypeStruct((b, g, ncp, LANES), BF16),
                   jax.ShapeDtypeStruct((b, g, VROWS, ncp), BF16)],
        compiler_params=_cparams(("parallel", "parallel")),
        name="compress",
    )(segk, segv, pos, w1, w2k, w2v, cc, sc)


def _load_queries(q_ref, qa_ref, r, tq):
    for rr in range(r):
        qa_ref[0:HEAD_DIM, rr * tq:(rr + 1) * tq] = q_ref[HEAD_DIM * rr:HEAD_DIM * (rr + 1), :]
    qa_ref[HEAD_DIM:LANES, :] = jnp.zeros((LANES - HEAD_DIM, r * tq), BF16)


def _store_heads(o_ref, acc, gate_ref, r, tq):
    den = acc[HEAD_DIM:HEAD_DIM + 1, :]
    o = acc[0:HEAD_DIM, :] * (1.0 / jnp.where(den > 0.0, den, 1.0))
    parts = []
    for rr in range(r):
        z = o[:, rr * tq:(rr + 1) * tq]
        if gate_ref is not None:
            z = z * gate_ref[rr:rr + 1, :]
        parts.append(z)
    o_ref[...] = jnp.concatenate(parts, axis=0).T


def _cmp_body(q_ref, k_ref, v_ref, gate_ref, o_ref, mb_ref, qa_ref, *, tq, nb, r):
    qi = pl.program_id(2)
    lw = r * tq
    _load_queries(q_ref, qa_ref, r, tq)
    s = jnp.dot(k_ref[...], qa_ref[...], preferred_element_type=F32)
    i_io = lax.broadcasted_iota(jnp.int32, (nb, lw), 0)
    t_io = qi * tq + (lax.broadcasted_iota(jnp.int32, (nb, lw), 1) & (tq - 1))
    lim = t_io - (CMP_BLOCK - 1)
    valid = [SEL_BLOCK * i_io + CMP_STRIDE * m <= lim for m in range(4)]
    sm = [jnp.where(valid[m], s[m * nb:(m + 1) * nb], NEG) for m in range(4)]
    mx = jnp.max(jnp.maximum(jnp.maximum(sm[0], sm[1]), jnp.maximum(sm[2], sm[3])), axis=0, keepdims=True)
    pm = [jnp.where(valid[m], jnp.exp2(sm[m] - mx), 0.0) for m in range(4)]
    den = jnp.sum(pm[0] + pm[1] + pm[2] + pm[3], axis=0, keepdims=True)
    acc = jnp.dot(v_ref[...], jnp.concatenate(pm, axis=0).astype(BF16), preferred_element_type=F32)
    _store_heads(o_ref, acc, gate_ref, r, tq)

    inv = 1.0 / jnp.where(den > 0.0, den, 1.0)
    ps = []
    for m in range(4):
        pn = pm[m] * inv
        acc_h = pn[:, 0:tq]
        for rr in range(1, r):
            acc_h = acc_h + pn[:, rr * tq:(rr + 1) * tq]
        ps.append(acc_h)
    j_io = lax.broadcasted_iota(jnp.int32, (nb, tq), 0)
    tt = qi * tq + lax.broadcasted_iota(jnp.int32, (nb, tq), 1)
    prev3 = jnp.where(j_io == 0, 0.0, pltpu.roll(ps[3], 1, axis=0))
    imp = prev3 + 2.0 * (ps[0] + ps[1] + ps[2]) + ps[3]
    cur = tt >> (SEL_BLOCK.bit_length() - 1)
    forced = (j_io == 0) | (j_io == cur) | (j_io == cur - 1)
    causal = j_io * SEL_BLOCK <= tt
    v = jnp.where(forced, FORCE, jnp.where(causal, imp, -FORCE))
    jf = j_io.astype(F32)
    for _ in range(min(N_SELECT, nb)):
        top = jnp.max(v, axis=0, keepdims=True)
        idx = jnp.min(jnp.where(v == top, jf, float(nb)), axis=0, keepdims=True)
        v = jnp.where(jf == idx, REMOVED, v)
    mb_ref[...] = jnp.where(v == REMOVED, 0.0, NEG).astype(mb_ref.dtype)


def _cmp_call(qt, kc, vct, gates, batch, seq):
    g, r = A_KV_HEADS, N_HEADS // A_KV_HEADS
    nb = seq // SEL_BLOCK
    ncp = 4 * nb
    return pl.pallas_call(
        functools.partial(_cmp_body, tq=TQ, nb=nb, r=r),
        grid=(batch, g, seq // TQ),
        in_specs=[pl.BlockSpec((None, r * HEAD_DIM, TQ), lambda b, gg, q: (b, gg, q)),
                  pl.BlockSpec((None, None, ncp, LANES), lambda b, gg, q: (b, gg, 0, 0)),
                  pl.BlockSpec((None, None, VROWS, ncp), lambda b, gg, q: (b, gg, 0, 0)),
                  pl.BlockSpec((None, None, None, r, TQ), lambda b, gg, q: (b, 0, gg, 0, q))],
        out_specs=[pl.BlockSpec((None, TQ, r * HEAD_DIM), lambda b, gg, q: (b, q, gg)),
                   pl.BlockSpec((None, None, nb, TQ), lambda b, gg, q: (b, gg, 0, q))],
        out_shape=[jax.ShapeDtypeStruct((batch, seq, N_HEADS * HEAD_DIM), F32),
                   jax.ShapeDtypeStruct((batch, g, nb, seq), BF16)],
        scratch_shapes=[pltpu.VMEM((LANES, r * TQ), BF16)],
        compiler_params=_cparams(("parallel", "parallel", "parallel")),
        name="cmp_attn_topk",
    )(qt, kc, vct, gates)


def _attn_body(*refs, mode, tq, r, window, mr, nb):
    if mode == "sel":
        q_ref, k_ref, v_ref, gate_ref, mb_ref, e_ref, o_ref, qa_ref, m_ref, acc_ref = refs
    elif mode == "win":
        q_ref, k_ref, v_ref, gate_ref, o_ref, qa_ref, m_ref, acc_ref = refs
    else:
        q_ref, k_ref, v_ref, sink_ref, o_ref, qa_ref, m_ref, acc_ref = refs
        gate_ref = None
    qi = pl.program_id(2)
    lw = r * tq
    start = qi * tq
    jd = start // TK
    _load_queries(q_ref, qa_ref, r, tq)
    if mode == "swa":
        m_ref[...] = sink_ref[...]
        rows = lax.broadcasted_iota(jnp.int32, (VROWS, lw), 0)
        acc_ref[...] = jnp.where(rows == HEAD_DIM, 1.0, 0.0)
    else:
        m_ref[...] = jnp.full((1, lw), M_INIT, F32)
        acc_ref[...] = jnp.zeros((VROWS, lw), F32)

    def step(j, masked):
        kt = k_ref[pl.ds(pl.multiple_of(j * TK, TK), TK), :]
        if mode == "sel":
            eoff = pl.multiple_of((j * TK) % (mr * SEL_BLOCK), TK)
            kt = jnp.concatenate([kt, e_ref[pl.ds(eoff, TK), :]], axis=1)
        s = jnp.dot(kt, qa_ref[...], preferred_element_type=F32)
        if masked:
            t_io = start + (lax.broadcasted_iota(jnp.int32, (TK, lw), 1) & (tq - 1))
            dist = t_io - (j * TK + lax.broadcasted_iota(jnp.int32, (TK, lw), 0))
            ok = dist >= 0
            if window is not None:
                ok = ok & (dist < window)
            s = jnp.where(ok, s, NEG)
        m_old = m_ref[...]
        m_new = jnp.maximum(m_old, jnp.max(s, axis=0, keepdims=True))
        p = jnp.exp2(s - m_new).astype(BF16)
        acc_ref[...] = acc_ref[...] * jnp.exp2(m_old - m_new) + jnp.dot(v_ref[j], p, preferred_element_type=F32)
        m_ref[...] = m_new

    def loop(lo, hi, masked):
        def body(j, c):
            step(j, masked)
            return c
        lax.fori_loop(lo, hi, body, 0)

    if mode == "sel":
        tph = mr * SEL_BLOCK // TK
        for hf in range(nb // mr):
            lo = hf * tph

            @pl.when(jd >= lo)
            def _(hf=hf, lo=lo):
                for rr in range(r):
                    qa_ref[LANES:LANES + mr, rr * tq:(rr + 1) * tq] = mb_ref[hf * mr:(hf + 1) * mr, :]
                loop(lo, jnp.minimum(jd, lo + tph), False)

                @pl.when(jd < lo + tph)
                def _():
                    step(jd, True)
    else:
        loop(jnp.maximum(start - window + 1, 0) // TK, jd + 1, True)

    _store_heads(o_ref, acc_ref[...], gate_ref, r, tq)


def _attn_call(mode, qt, k, vt, batch, seq, g, gates=None, branch=None, mb=None, emat=None, sinks=None,
               window=None):
    r = N_HEADS // g
    nb = seq // SEL_BLOCK
    mr = emat.shape[1] if mode == "sel" else 0
    nkt = seq // TK
    kc = LANES + mr
    in_specs = [pl.BlockSpec((None, r * HEAD_DIM, TQ), lambda b, gg, q: (b, gg, q)),
                pl.BlockSpec((None, seq, LANES), lambda b, gg, q: (b, 0, gg)),
                pl.BlockSpec((None, nkt, VROWS, TK), lambda b, gg, q: (b, 0, gg, 0))]
    args = [qt, k, vt]
    if mode in ("sel", "win"):
        in_specs.append(pl.BlockSpec((None, None, None, r, TQ), lambda b, gg, q: (b, branch, gg, 0, q)))
        args.append(gates)
    if mode == "sel":
        in_specs += [pl.BlockSpec((None, None, nb, TQ), lambda b, gg, q: (b, gg, 0, q)), _const_spec(emat.shape)]
        args += [mb, emat]
    if mode == "swa":
        in_specs.append(pl.BlockSpec((None, 1, r * TQ), lambda b, gg, q: (gg, 0, 0)))
        args.append(sinks)
    return pl.pallas_call(
        functools.partial(_attn_body, mode=mode, tq=TQ, r=r, window=window, mr=mr, nb=nb),
        grid=(batch, g, seq // TQ),
        in_specs=in_specs,
        out_specs=pl.BlockSpec((None, TQ, r * HEAD_DIM), lambda b, gg, q: (b, q, gg)),
        out_shape=jax.ShapeDtypeStruct((batch, seq, N_HEADS * HEAD_DIM), F32),
        scratch_shapes=[pltpu.VMEM((kc, r * TQ), BF16), pltpu.VMEM((1, r * TQ), F32),
                        pltpu.VMEM((VROWS, r * TQ), F32)],
        compiler_params=_cparams(("parallel", "parallel", "arbitrary")),
        name=mode + "_attn",
    )(*args)


def _ffn_body(*refs, n_o, final, tm, ns, nchunk):
    it = iter(refs)
    h_ref = next(it)
    o_refs = [next(it) for _ in range(n_o)]
    wo_ref, g_ref, wa_ref, wg_ref, cw_ref, wout_ref = (next(it) for _ in range(6))
    gf_ref = next(it) if final else None
    out_ref, hn_ref, y_ref, prev_ref = (next(it) for _ in range(4))

    osum = o_refs[0][...]
    for o_ref in o_refs[1:]:
        osum = osum + o_ref[...]
    h = h_ref[...] + jnp.dot(osum.astype(BF16), wo_ref[...], preferred_element_type=F32)
    hn_ref[...] = _rms(h, g_ref[...]).astype(BF16)
    y_ref[...] = jnp.zeros_like(y_ref)

    @pl.when(pl.program_id(0) % ns == 0)
    def _():
        prev_ref[...] = jnp.zeros_like(prev_ref)

    rid = lax.broadcasted_iota(jnp.int32, (8, FF_CHUNK), 0)

    def conv(u, p8, w):
        u1 = pltpu.roll(u, 1, axis=0)
        u2 = pltpu.roll(u, 2, axis=0)
        f1 = jnp.where(rid < 1, pltpu.roll(p8, 1, axis=0), u1[0:8])
        f2 = jnp.where(rid < 2, pltpu.roll(p8, 2, axis=0), u2[0:8])
        u1 = jnp.concatenate([f1, u1[8:]], axis=0)
        u2 = jnp.concatenate([f2, u2[8:]], axis=0)
        return w[3:4] + w[0:1] * u2 + w[1:2] * u1 + w[2:3] * u

    def chunk(c, carry):
        hn = hn_ref[...]
        ua = jnp.dot(hn, wa_ref[c], preferred_element_type=F32)
        ug = jnp.dot(hn, wg_ref[c], preferred_element_type=F32)
        cw = cw_ref[c]
        pa, pg = prev_ref[c, 0:8], prev_ref[c, 8:16]
        prev_ref[c, 0:8] = ua[tm - 8:tm]
        prev_ref[c, 8:16] = ug[tm - 8:tm]
        ca = conv(ua, pa, cw[0:4])
        cg = conv(ug, pg, cw[4:8])
        act = (ca * jax.nn.sigmoid(ca) * cg).astype(BF16)
        y_ref[...] += jnp.dot(act, wout_ref[c], preferred_element_type=F32)
        return carry

    lax.fori_loop(0, nchunk, chunk, 0)
    out = h + y_ref[...]
    if final:
        out = _rms(out, gf_ref[...])
    out_ref[...] = out


def _ffn_call(h, o_list, wo, gain, wa, wg, cw, wout, seq, final_gain=None):
    t_tokens, d = h.shape
    tm = min(TM, seq)
    ns = seq // tm
    nchunk = wa.shape[0]
    tile = pl.BlockSpec((tm, d), lambda i: (i, 0))
    in_specs = [tile] + [tile] * len(o_list) + [
        _const_spec(wo.shape), _const_spec((1, d)), _const_spec(wa.shape), _const_spec(wg.shape),
        _const_spec(cw.shape), _const_spec(wout.shape)]
    args = [h] + list(o_list) + [wo, gain.reshape(1, d), wa, wg, cw, wout]
    if final_gain is not None:
        in_specs.append(_const_spec((1, d)))
        args.append(final_gain.reshape(1, d))
    return pl.pallas_call(
        functools.partial(_ffn_body, n_o=len(o_list), final=final_gain is not None, tm=tm, ns=ns, nchunk=nchunk),
        grid=(t_tokens // tm,),
        in_specs=in_specs, out_specs=tile,
        out_shape=jax.ShapeDtypeStruct((t_tokens, d), F32),
        scratch_shapes=[pltpu.VMEM((tm, d), BF16), pltpu.VMEM((tm, d), F32),
                        pltpu.VMEM((nchunk, 16, FF_CHUNK), F32)],
        compiler_params=_cparams(("arbitrary",)),
        name="attn_out_ffn",
    )(*args)


def _pad_heads(w, g, width):
    d = w.shape[0]
    w3 = w.reshape(d, g, HEAD_DIM)
    return jnp.pad(w3, ((0, 0), (0, 0), (0, width - HEAD_DIM))).reshape(d, g * width)


def _rot_heads(w, g):
    d = w.shape[0]
    w3 = w.reshape(d, g, HEAD_DIM)
    return jnp.concatenate([-w3[..., HALF:], w3[..., :HALF]], axis=-1).reshape(d, g * HEAD_DIM)


def _rope_k_weights(w, g):
    return jnp.concatenate([_pad_heads(w, g, LANES), _pad_heads(_rot_heads(w, g), g, LANES)], axis=1)


def _rope_tables(seq):
    inv = jnp.float32(ROPE_THETA) ** (-jnp.arange(HALF, dtype=F32) / HALF)

    def cs(pos):
        ang = pos.astype(F32)[:, None] * inv[None, :]
        return jnp.cos(ang), jnp.sin(ang)

    def nat(c):
        return jnp.concatenate([c, c, jnp.zeros((c.shape[0], LANES - HEAD_DIM), F32)], axis=1)

    cos, sin = cs(jnp.arange(seq))
    qscale = HEAD_DIM ** -0.5 * LOG2E
    cc, sc = cs(jnp.arange(seq // CMP_STRIDE) * CMP_STRIDE + CMP_BLOCK - 1)
    return {"cn": nat(cos), "sn": nat(sin), "ct": (cos * qscale).T, "st": (sin * qscale).T,
            "cc": nat(cc), "sc": nat(sc)}


def _ffn_weights(w_in, conv_w, conv_b, w_out):
    d, two_ff = w_in.shape
    dff = two_ff // 2
    nchunk = dff // FF_CHUNK

    def chunks(w):
        return w.reshape(d, nchunk, FF_CHUNK).transpose(1, 0, 2).astype(BF16)

    wa, wg = chunks(w_in[:, :dff]), chunks(w_in[:, dff:])
    taps = jnp.concatenate([conv_w, conv_b[None, :]], axis=0)
    cw = jnp.concatenate([taps[:, :dff].reshape(4, nchunk, FF_CHUNK), taps[:, dff:].reshape(4, nchunk, FF_CHUNK)],
                         axis=0).transpose(1, 0, 2)
    return wa, wg, cw, w_out.reshape(nchunk, FF_CHUNK, d).astype(BF16)


def _nsa_attention(h, gain, w_in, cmp_pos, cmp_w1, cmp_w2, tabs, emat, batch, seq):
    g, r = A_KV_HEADS, N_HEADS // A_KV_HEADS
    d = h.shape[1]
    kvw = g * HEAD_DIM
    nq = N_HEADS * HEAD_DIM
    wq, wkc, wvc, wks, wvs, wkw, wvw, wgl = jnp.split(
        w_in, [nq, nq + kvw, nq + 2 * kvw, nq + 3 * kvw, nq + 4 * kvw, nq + 5 * kvw, nq + 6 * kvw], axis=1)
    wn = jnp.concatenate([_rope_k_weights(wks, g), _rope_k_weights(wkw, g), wkc, wvc], axis=1).astype(BF16)
    wgl = wgl.reshape(d, N_HEADS, 3).transpose(0, 2, 1).reshape(d, 3 * N_HEADS)
    wt = jnp.concatenate([wq, wgl, _pad_heads(wvs, g, VROWS), _pad_heads(wvw, g, VROWS)], axis=1).T.astype(BF16)
    kpad = g * LANES
    nat_plan = [("rope", 0, kpad), ("rope", 2 * kpad, kpad), ("plain", 4 * kpad, kvw), ("plain", 4 * kpad + kvw, kvw)]
    ng = 3 * N_HEADS
    tr_plan = [("ropeq", 0, nq), ("sigmoid", nq, ng), ("vaug", nq + ng, g * VROWS),
               ("vaug", nq + ng + g * VROWS, g * VROWS)]
    ks, kw, kc, vc, qt, gates, vst, vwt = _proj_call(
        h, gain, tabs, wn, wt, nat_plan, tr_plan, [BF16, BF16, F32, F32], batch, seq)
    ks = ks.reshape(batch, seq, kpad)
    kw = kw.reshape(batch, seq, kpad)
    gates = gates.reshape(batch, 3, g, r, seq)

    nseg = seq // CMP_STRIDE
    nb = seq // SEL_BLOCK
    seg_f = CMP_STRIDE * HEAD_DIM

    def segs(x):
        return x.reshape(batch, nseg, CMP_STRIDE, g, HEAD_DIM).transpose(0, 3, 1, 2, 4).reshape(batch, g, nseg, seg_f)

    pos = cmp_pos.reshape(2, 2, 1, seg_f)
    w1 = cmp_w1.reshape(2, 2, seg_f, cmp_w1.shape[-1]).astype(BF16)
    w2k = jnp.stack([_pad_heads(cmp_w2[0], 1, LANES), _pad_heads(_rot_heads(cmp_w2[0], 1), 1, LANES)]).astype(BF16)
    w2v = _pad_heads(cmp_w2[1], 1, VROWS).T.astype(BF16)
    kcc, vcc = _compress_call(segs(kc), segs(vc), pos, w1, w2k, w2v, tabs["cc"], tabs["sc"])
    kcc = kcc.reshape(batch, g, nb, 4, LANES).transpose(0, 1, 3, 2, 4).reshape(batch, g, nseg, LANES)
    vcc = vcc.reshape(batch, g, VROWS, nb, 4).transpose(0, 1, 2, 4, 3).reshape(batch, g, VROWS, nseg)

    o_c, mb = _cmp_call(qt, kcc, vcc, gates, batch, seq)
    o_s = _attn_call("sel", qt, ks, vst, batch, seq, g, gates=gates, branch=1, mb=mb, emat=emat)
    o_w = _attn_call("win", qt, kw, vwt, batch, seq, g, gates=gates, branch=2, window=WINDOW_A)
    t_tokens = batch * seq
    return [o.reshape(t_tokens, nq) for o in (o_c, o_s, o_w)]


def kernel(x, norm_attn, norm_ffn, a_w_in, a_cmp_pos, a_cmp_w1, a_cmp_w2, a_w_out, kv_norm, b_w_kv, b_w_q, b_sinks,
           b_w_out, ffn_w_in, ffn_conv_w, ffn_conv_b, ffn_w_out, final_norm):
    batch, seq, d = x.shape
    depth = norm_attn.shape[0]
    n_a = a_w_in.shape[0]
    tabs = _rope_tables(seq)
    mr = min(MASK_ROWS, seq // SEL_BLOCK)
    emat = (jnp.arange(mr * SEL_BLOCK)[:, None] // SEL_BLOCK == jnp.arange(mr)[None, :]).astype(BF16)
    h = x.reshape(batch * seq, d)
    k_sh = v_sh = None
    gb, rb = B_KV_HEADS, N_HEADS // B_KV_HEADS
    for layer in range(depth):
        if layer < n_a:
            o_list = _nsa_attention(h, norm_attn[layer], a_w_in[layer], a_cmp_pos[layer], a_cmp_w1[layer],
                                    a_cmp_w2[layer], tabs, emat, batch, seq)
            wo = a_w_out[layer]
        else:
            j = layer - n_a
            if k_sh is None:
                wk, wv = jnp.split(b_w_kv, 2, axis=1)
                kpad = gb * LANES
                k_sh, v_sh = _proj_call(
                    h, kv_norm, tabs, _rope_k_weights(wk, gb).astype(BF16), _pad_heads(wv, gb, VROWS).T.astype(BF16),
                    [("rope", 0, kpad)], [("vaug", 0, gb * VROWS)], [BF16], batch, seq)
                k_sh = k_sh.reshape(batch, seq, kpad)
            (qt,) = _proj_call(h, norm_attn[layer], tabs, None, b_w_q[j].T.astype(BF16), [],
                               [("ropeq", 0, N_HEADS * HEAD_DIM)], [], batch, seq)
            sinks = jnp.broadcast_to((b_sinks[j] * LOG2E).reshape(gb, 1, rb, 1), (gb, 1, rb, TQ)).reshape(gb, 1, rb * TQ)
            o = _attn_call("swa", qt, k_sh, v_sh, batch, seq, gb, sinks=sinks, window=WINDOW_B)
            o_list = [o.reshape(batch * seq, N_HEADS * HEAD_DIM)]
            wo = b_w_out[j]
        wa, wg, cw, wout = _ffn_weights(ffn_w_in[layer], ffn_conv_w[layer], ffn_conv_b[layer], ffn_w_out[layer])
        h = _ffn_call(h, o_list, wo.astype(BF16), norm_ffn[layer], wa, wg, cw, wout, seq,
                      final_gain=final_norm if layer == depth - 1 else None)
    return h.reshape(batch, seq, d)
```

```python
import functools
import math

import jax
import jax.numpy as jnp
from jax import lax
from jax.experimental import pallas as pl
from jax.experimental.pallas import tpu as pltpu

F32 = jnp.float32
BF16 = jnp.bfloat16

HEAD_DIM = 64
HALF = HEAD_DIM // 2
N_HEADS = 16
A_KV_HEADS = 4
B_KV_HEADS = 2
CMP_BLOCK = 32
CMP_STRIDE = 16
SEL_BLOCK = 64
N_SELECT = 16
WINDOW_A = 512
WINDOW_B = 128
CONV_WIDTH = 3
ROPE_THETA = 10000.0
EPS = 1e-6
FORCE = 1e6

LANES = 128
BF16_SUBLANES = 16
VMEM_LIMIT = 56 * 1024 * 1024
TQ = 128
TK = 256
SEL_GROUP = 4
TM = 512
FF_CHUNK = 256
VROWS = HEAD_DIM + BF16_SUBLANES
MASK_ROWS = 128

LOG2E = 1.4426950408889634
NEG = -1e30
M_INIT = -1e29
REMOVED = -3e38

_NT = (((1,), (1,)), ((), ()))


def _cparams(sem):
    return pltpu.CompilerParams(dimension_semantics=sem, vmem_limit_bytes=VMEM_LIMIT)


def _const_spec(shape):
    n = len(shape)
    return pl.BlockSpec(shape, lambda *_: (0,) * n, pipeline_mode=pl.Buffered(1))


def _rms(x, g):
    ms = jnp.mean(x * x, axis=-1, keepdims=True)
    return x * lax.rsqrt(ms + EPS) * g


def _proj_body(*refs, nat_plan, tr_plan, tm):
    it = iter(refs)
    h_ref, g_ref = next(it), next(it)
    if nat_plan:
        cn_ref, sn_ref, wn_ref = next(it), next(it), next(it)
    ct_ref, st_ref, wt_ref = next(it), next(it), next(it)
    outs = list(it)

    hn = _rms(h_ref[...], g_ref[...]).astype(BF16)
    oi = 0
    for kind, c0, n in nat_plan:
        o_ref = outs[oi]
        oi += 1
        y = jnp.dot(hn, wn_ref[:, c0:c0 + n], preferred_element_type=F32)
        if kind == "rope":
            yr = jnp.dot(hn, wn_ref[:, c0 + n:c0 + 2 * n], preferred_element_type=F32)
            c, s = cn_ref[...], sn_ref[...]
            for g in range(n // LANES):
                sl = slice(LANES * g, LANES * (g + 1))
                o_ref[:, sl] = (y[:, sl] * c + yr[:, sl] * s).astype(o_ref.dtype)
        else:
            o_ref[...] = y.astype(o_ref.dtype)
    for kind, r0, n in tr_plan:
        o_ref = outs[oi]
        oi += 1
        y = lax.dot_general(wt_ref[r0:r0 + n, :], hn, _NT, preferred_element_type=F32)
        if kind == "ropeq":
            c, s = ct_ref[...], st_ref[...]
            for hd in range(n // HEAD_DIM):
                a = HEAD_DIM * hd
                y1, y2 = y[a:a + HALF], y[a + HALF:a + HEAD_DIM]
                o_ref[a:a + HALF, :] = (y1 * c - y2 * s).astype(o_ref.dtype)
                o_ref[a + HALF:a + HEAD_DIM, :] = (y2 * c + y1 * s).astype(o_ref.dtype)
        elif kind == "sigmoid":
            o_ref[...] = jax.nn.sigmoid(y)
        else:
            rows = lax.broadcasted_iota(jnp.int32, (VROWS, tm), 0)
            for g in range(n // VROWS):
                yg = jnp.where(rows == HEAD_DIM, 1.0, y[VROWS * g:VROWS * (g + 1)]).astype(o_ref.dtype)
                for t in range(tm // TK):
                    o_ref[t, VROWS * g:VROWS * (g + 1), :] = yg[:, t * TK:(t + 1) * TK]


def _proj_call(h, gain, tabs, wn, wt, nat_plan, tr_plan, nat_dtypes, batch, seq):
    t_tokens, d = h.shape
    tm = min(TM, seq)
    ns = seq // tm
    in_specs = [pl.BlockSpec((tm, d), lambda i: (i, 0)), _const_spec((1, d))]
    args = [h, gain.reshape(1, d)]
    if nat_plan:
        in_specs += [pl.BlockSpec((tm, LANES), lambda i: (i % ns, 0)),
                     pl.BlockSpec((tm, LANES), lambda i: (i % ns, 0)),
                     _const_spec(wn.shape)]
        args += [tabs["cn"], tabs["sn"], wn]
    in_specs += [pl.BlockSpec((HALF, tm), lambda i: (0, i % ns)),
                 pl.BlockSpec((HALF, tm), lambda i: (0, i % ns)),
                 _const_spec(wt.shape)]
    args += [tabs["ct"], tabs["st"], wt]
    out_shape, out_specs = [], []
    for (kind, _, n), dt in zip(nat_plan, nat_dtypes):
        out_shape.append(jax.ShapeDtypeStruct((t_tokens, n), dt))
        out_specs.append(pl.BlockSpec((tm, n), lambda i: (i, 0)))
    for kind, _, n in tr_plan:
        if kind == "vaug":
            out_shape.append(jax.ShapeDtypeStruct((batch, seq // TK, n, TK), BF16))
            out_specs.append(pl.BlockSpec((None, tm // TK, n, TK), lambda i: (i // ns, i % ns, 0, 0)))
        else:
            dt = F32 if kind == "sigmoid" else BF16
            out_shape.append(jax.ShapeDtypeStruct((batch, n, seq), dt))
            out_specs.append(pl.BlockSpec((None, n, tm), lambda i: (i // ns, 0, i % ns)))
    return pl.pallas_call(
        functools.partial(_proj_body, nat_plan=tuple(nat_plan), tr_plan=tuple(tr_plan), tm=tm),
        grid=(t_tokens // tm,),
        in_specs=in_specs, out_specs=out_specs, out_shape=out_shape,
        compiler_params=_cparams(("parallel",)),
        name="norm_proj",
    )(*args)


def _gelu_tanh(x):
    c = math.sqrt(2.0 / math.pi)
    return x * (0.5 * (1.0 + jnp.tanh(c * (x + 0.044715 * (x * x * x)))))


def _compress_body(sk_ref, sv_ref, pos_ref, w1_ref, w2k_ref, w2v_ref, cc_ref, sc_ref, ok_ref, ov_ref, *, ncp):
    def hidden(seg, kv):
        xa = (seg + pos_ref[kv, 0]).astype(BF16)
        xb = (seg + pos_ref[kv, 1]).astype(BF16)
        a = jnp.dot(xa, w1_ref[kv, 0], preferred_element_type=F32)
        b = jnp.dot(xb, w1_ref[kv, 1], preferred_element_type=F32)
        return _gelu_tanh(a + pltpu.roll(b, ncp - 1, axis=0)).astype(BF16)

    gk = hidden(sk_ref[...], 0)
    k = jnp.dot(gk, w2k_ref[0], preferred_element_type=F32)
    kr = jnp.dot(gk, w2k_ref[1], preferred_element_type=F32)
    ok_ref[...] = (k * cc_ref[...] + kr * sc_ref[...]).astype(ok_ref.dtype)
    gv = hidden(sv_ref[...], 1)
    vt = lax.dot_general(w2v_ref[...], gv, _NT, preferred_element_type=F32)
    rows = lax.broadcasted_iota(jnp.int32, (VROWS, ncp), 0)
    ov_ref[...] = jnp.where(rows == HEAD_DIM, 1.0, vt).astype(ov_ref.dtype)


def _compress_call(segk, segv, pos, w1, w2k, w2v, cc, sc):
    b, g, ncp, f = segk.shape
    seg_spec = pl.BlockSpec((None, None, ncp, f), lambda i, j: (i, j, 0, 0))
    return pl.pallas_call(
        functools.partial(_compress_body, ncp=ncp),
        grid=(b, g),
        in_specs=[seg_spec, seg_spec, _const_spec(pos.shape), _const_spec(w1.shape), _const_spec(w2k.shape),
                  _const_spec(w2v.shape), _const_spec(cc.shape), _const_spec(sc.shape)],
        out_specs=[pl.BlockSpec((None, None, ncp, LANES), lambda i, j: (i, j, 0, 0)),
                   pl.BlockSpec((None, None, VROWS, ncp), lambda i, j: (i, j, 0, 0))],
        out_shape=[jax.ShapeDtypeStruct((b, g, ncp, LANES), BF16),
                   jax.ShapeDtypeStruct((b, g, VROWS, ncp), BF16)],
        compiler_params=_cparams(("parallel", "parallel")),
        name="compress",
    )(segk, segv, pos, w1, w2k, w2v, cc, sc)


def _load_queries(q_ref, qa_ref, r, tq):
    for rr in range(r):
        qa_ref[0:HEAD_DIM, rr * tq:(rr + 1) * tq] = q_ref[HEAD_DIM * rr:HEAD_DIM * (rr + 1), :]
    qa_ref[HEAD_DIM:LANES, :] = jnp.zeros((LANES - HEAD_DIM, r * tq), BF16)


def _store_heads(o_ref, acc, gate_ref, r, tq):
    den = acc[HEAD_DIM:HEAD_DIM + 1, :]
    o = acc[0:HEAD_DIM, :] * (1.0 / jnp.where(den > 0.0, den, 1.0))
    parts = []
    for rr in range(r):
        z = o[:, rr * tq:(rr + 1) * tq]
        if gate_ref is not None:
            z = z * gate_ref[rr:rr + 1, :]
        parts.append(z)
    o_ref[...] = jnp.concatenate(parts, axis=0).T


def _cmp_body(q_ref, k_ref, v_ref, gate_ref, o_ref, mb_ref, qa_ref, *, tq, nb, r):
    qi = pl.program_id(2)
    lw = r * tq
    _load_queries(q_ref, qa_ref, r, tq)
    s = jnp.dot(k_ref[...], qa_ref[...], preferred_element_type=F32)
    i_io = lax.broadcasted_iota(jnp.int32, (nb, lw), 0)
    t_io = qi * tq + (lax.broadcasted_iota(jnp.int32, (nb, lw), 1) & (tq - 1))
    lim = t_io - (CMP_BLOCK - 1)
    valid = [SEL_BLOCK * i_io + CMP_STRIDE * m <= lim for m in range(4)]
    sm = [jnp.where(valid[m], s[m * nb:(m + 1) * nb], NEG) for m in range(4)]
    mx = jnp.max(jnp.maximum(jnp.maximum(sm[0], sm[1]), jnp.maximum(sm[2], sm[3])), axis=0, keepdims=True)
    pm = [jnp.where(valid[m], jnp.exp2(sm[m] - mx), 0.0) for m in range(4)]
    den = jnp.sum(pm[0] + pm[1] + pm[2] + pm[3], axis=0, keepdims=True)
    acc = jnp.dot(v_ref[...], jnp.concatenate(pm, axis=0).astype(BF16), preferred_element_type=F32)
    _store_heads(o_ref, acc, gate_ref, r, tq)

    inv = 1.0 / jnp.where(den > 0.0, den, 1.0)
    ps = []
    for m in range(4):
        pn = pm[m] * inv
        acc_h = pn[:, 0:tq]
        for rr in range(1, r):
            acc_h = acc_h + pn[:, rr * tq:(rr + 1) * tq]
        ps.append(acc_h)
    j_io = lax.broadcasted_iota(jnp.int32, (nb, tq), 0)
    tt = qi * tq + lax.broadcasted_iota(jnp.int32, (nb, tq), 1)
    prev3 = jnp.where(j_io == 0, 0.0, pltpu.roll(ps[3], 1, axis=0))
    imp = prev3 + 2.0 * (ps[0] + ps[1] + ps[2]) + ps[3]
    cur = tt >> (SEL_BLOCK.bit_length() - 1)
    forced = (j_io == 0) | (j_io == cur) | (j_io == cur - 1)
    causal = j_io * SEL_BLOCK <= tt
    v = jnp.where(forced, FORCE, jnp.where(causal, imp, -FORCE))
    jf = j_io.astype(F32)
    for _ in range(min(N_SELECT, nb)):
        top = jnp.max(v, axis=0, keepdims=True)
        idx = jnp.min(jnp.where(v == top, jf, float(nb)), axis=0, keepdims=True)
        v = jnp.where(jf == idx, REMOVED, v)
    mb_ref[...] = jnp.where(v == REMOVED, 0.0, NEG).astype(mb_ref.dtype)


def _cmp_call(qt, kc, vct, gates, batch, seq):
    g, r = A_KV_HEADS, N_HEADS // A_KV_HEADS
    nb = seq // SEL_BLOCK
    ncp = 4 * nb
    return pl.pallas_call(
        functools.partial(_cmp_body, tq=TQ, nb=nb, r=r),
        grid=(batch, g, seq // TQ),
        in_specs=[pl.BlockSpec((None, r * HEAD_DIM, TQ), lambda b, gg, q: (b, gg, q)),
                  pl.BlockSpec((None, None, ncp, LANES), lambda b, gg, q: (b, gg, 0, 0)),
                  pl.BlockSpec((None, None, VROWS, ncp), lambda b, gg, q: (b, gg, 0, 0)),
                  pl.BlockSpec((None, None, None, r, TQ), lambda b, gg, q: (b, 0, gg, 0, q))],
        out_specs=[pl.BlockSpec((None, TQ, r * HEAD_DIM), lambda b, gg, q: (b, q, gg)),
                   pl.BlockSpec((None, None, nb, TQ), lambda b, gg, q: (b, gg, 0, q))],
        out_shape=[jax.ShapeDtypeStruct((batch, seq, N_HEADS * HEAD_DIM), F32),
                   jax.ShapeDtypeStruct((batch, g, nb, seq), BF16)],
        scratch_shapes=[pltpu.VMEM((LANES, r * TQ), BF16)],
        compiler_params=_cparams(("parallel", "parallel", "parallel")),
        name="cmp_attn_topk",
    )(qt, kc, vct, gates)


def _attn_body(*refs, mode, tq, r, window, mr, nb):
    if mode == "sel":
        (q_ref, k_ref, v_ref, gate_ref, mb_ref, e_ref, o_ref, qa_ref, m_ref, acc_ref, macc_ref,
         s0_ref, s1_ref, p0_ref, p1_ref, mt0_ref, mt1_ref, cm0_ref, cm1_ref) = refs
        s_refs, p_refs, mt_refs, cm_refs = (s0_ref, s1_ref), (p0_ref, p1_ref), (mt0_ref, mt1_ref), (cm0_ref, cm1_ref)
    elif mode == "win":
        q_ref, k_ref, v_ref, gate_ref, o_ref, qa_ref, m_ref, acc_ref = refs
    else:
        q_ref, k_ref, v_ref, sink_ref, o_ref, qa_ref, m_ref, acc_ref = refs
        gate_ref = None
    qi = pl.program_id(2)
    lw = r * tq
    start = qi * tq
    jd = start // TK
    _load_queries(q_ref, qa_ref.at[0] if mode == "sel" else qa_ref, r, tq)
    if mode == "swa":
        m_ref[...] = sink_ref[...]
        rows = lax.broadcasted_iota(jnp.int32, (VROWS, lw), 0)
        acc_ref[...] = jnp.where(rows == HEAD_DIM, 1.0, 0.0)
    else:
        m_ref[...] = jnp.full((1, lw), M_INIT, F32)
        acc_ref[...] = jnp.zeros((VROWS, lw), F32)

    def step(j0, u, masked):
        row0 = pl.multiple_of(j0 * TK, TK)
        kt = k_ref[pl.ds(row0, u * TK), :]
        qa = qa_ref[...]
        m = m_ref[...]
        acc = acc_ref[...]
        s_next = jnp.dot(kt[0:TK], qa, preferred_element_type=F32)
        for i in range(u):
            s = s_next
            if i + 1 < u:
                s_next = jnp.dot(kt[(i + 1) * TK:(i + 2) * TK], qa, preferred_element_type=F32)
            if masked:
                t_io = start + (lax.broadcasted_iota(jnp.int32, (TK, lw), 1) & (tq - 1))
                dist = t_io - (row0 + i * TK + lax.broadcasted_iota(jnp.int32, (TK, lw), 0))
                ok = dist >= 0
                if window is not None:
                    ok = ok & (dist < window)
                s = jnp.where(ok, s, NEG)
            m_new = jnp.maximum(m, jnp.max(s, axis=0, keepdims=True))
            p = jnp.exp2(s - m_new).astype(BF16)
            acc = acc * jnp.exp2(m - m_new) + jnp.dot(v_ref[j0 + i], p, preferred_element_type=F32)
            m = m_new
        acc_ref[...] = acc
        m_ref[...] = m

    if mode == "sel":
        u = p0_ref.shape[0]
        gph = mr * SEL_BLOCK // (u * TK)
        gd = jd // u
        for hf in range(1, nb // mr):
            qa_ref[hf, 0:LANES, :] = qa_ref[0, 0:LANES, :]
        for hf in range(nb // mr):
            for rr in range(r):
                qa_ref[hf, LANES:LANES + mr, rr * tq:(rr + 1) * tq] = mb_ref[hf * mr:(hf + 1) * mr, :]
        p_refs[1][...] = jnp.zeros(p_refs[1].shape, BF16)
        mt_refs[1][...] = jnp.full(mt_refs[1].shape, M_INIT, F32)
        macc_ref[...] = jnp.full((1, lw), M_INIT, F32)

        def scores(gi, slot):
            row0 = pl.multiple_of(gi * (u * TK), u * TK)
            eoff = pl.multiple_of(row0 % (mr * SEL_BLOCK), u * TK)
            qa = qa_ref[gi // gph]
            for i in range(u):
                kt = jnp.concatenate([k_ref[pl.ds(row0 + i * TK, TK), :], e_ref[pl.ds(eoff + i * TK, TK), :]], axis=1)
                s = jnp.dot(kt, qa, preferred_element_type=F32)
                s_refs[slot][i] = s
                cm_refs[slot][i] = jnp.max(s, axis=0, keepdims=True)

        def softmax(gi, slot, masked):
            m = m_ref[...]
            for i in range(u):
                if masked:
                    t_io = start + (lax.broadcasted_iota(jnp.int32, (TK, lw), 1) & (tq - 1))
                    kpos = (gi * u + i) * TK + lax.broadcasted_iota(jnp.int32, (TK, lw), 0)
                    s = jnp.where(kpos <= t_io, s_refs[slot][i], NEG)
                    m = jnp.maximum(m, jnp.max(s, axis=0, keepdims=True))
                else:
                    m = jnp.maximum(m, cm_refs[slot][i])
                    s = s_refs[slot][i]
                p_refs[slot][i] = jnp.exp2(s - m).astype(BF16)
                mt_refs[slot][i] = m
            m_ref[...] = m

        def values(gi, slot):
            acc, ma = acc_ref[...], macc_ref[...]
            for i in range(u):
                mt = mt_refs[slot][i]
                acc = acc * jnp.exp2(ma - mt) + jnp.dot(v_ref[gi * u + i], p_refs[slot][i],
                                                        preferred_element_type=F32)
                ma = mt
            acc_ref[...], macc_ref[...] = acc, ma

        def trip(gi, slot):
            scores(gi + 1, 1 - slot)
            softmax(gi, slot, False)
            values(jnp.maximum(gi - 1, 0), 1 - slot)

        def last(slot):
            softmax(gd, slot, True)
            values(jnp.maximum(gd - 1, 0), 1 - slot)
            values(gd, slot)

        scores(0, 0)

        def body(pi, c):
            trip(2 * pi, 0)
            trip(2 * pi + 1, 1)
            return c
        lax.fori_loop(0, gd // 2, body, 0)

        @pl.when(gd % 2 == 1)
        def _():
            trip(gd - 1, 0)
            last(1)

        @pl.when(gd % 2 == 0)
        def _():
            last(0)
    else:
        nt = max((s0 + tq - 1) // TK - (s0 - window + 1) // TK + 1 for s0 in range(8 * window, 8 * window + TK, tq))
        step(jnp.maximum(jd - (nt - 1), 0), nt, True)

    _store_heads(o_ref, acc_ref[...], gate_ref, r, tq)


def _attn_call(mode, qt, k, vt, batch, seq, g, gates=None, branch=None, mb=None, emat=None, sinks=None,
               window=None):
    r = N_HEADS // g
    nb = seq // SEL_BLOCK
    mr = emat.shape[1] if mode == "sel" else 0
    nkt = seq // TK
    kc = LANES + mr
    in_specs = [pl.BlockSpec((None, r * HEAD_DIM, TQ), lambda b, gg, q: (b, gg, q)),
                pl.BlockSpec((None, seq, LANES), lambda b, gg, q: (b, 0, gg)),
                pl.BlockSpec((None, nkt, VROWS, TK), lambda b, gg, q: (b, 0, gg, 0))]
    args = [qt, k, vt]
    if mode in ("sel", "win"):
        in_specs.append(pl.BlockSpec((None, None, None, r, TQ), lambda b, gg, q: (b, branch, gg, 0, q)))
        args.append(gates)
    if mode == "sel":
        in_specs += [pl.BlockSpec((None, None, nb, TQ), lambda b, gg, q: (b, gg, 0, q)), _const_spec(emat.shape)]
        args += [mb, emat]
    if mode == "swa":
        in_specs.append(pl.BlockSpec((None, 1, r * TQ), lambda b, gg, q: (gg, 0, 0)))
        args.append(sinks)
    lw = r * TQ
    scratch = [pltpu.VMEM((kc, lw), BF16), pltpu.VMEM((1, lw), F32), pltpu.VMEM((VROWS, lw), F32)]
    if mode == "sel":
        u = min(SEL_GROUP, mr * SEL_BLOCK // TK)
        scratch[0] = pltpu.VMEM((nb // mr, kc, lw), BF16)
        scratch += ([pltpu.VMEM((1, lw), F32)] + [pltpu.VMEM((u, TK, lw), F32)] * 2
                    + [pltpu.VMEM((u, TK, lw), BF16)] * 2 + [pltpu.VMEM((u, 1, lw), F32)] * 4)
    return pl.pallas_call(
        functools.partial(_attn_body, mode=mode, tq=TQ, r=r, window=window, mr=mr, nb=nb),
        grid=(batch, g, seq // TQ),
        in_specs=in_specs,
        out_specs=pl.BlockSpec((None, TQ, r * HEAD_DIM), lambda b, gg, q: (b, q, gg)),
        out_shape=jax.ShapeDtypeStruct((batch, seq, N_HEADS * HEAD_DIM), F32),
        scratch_shapes=scratch,
        compiler_params=_cparams(("parallel", "parallel", "arbitrary")),
        name=mode + "_attn",
    )(*args)


def _ffn_body(*refs, n_o, final, tm, ns, nchunk):
    it = iter(refs)
    h_ref = next(it)
    o_refs = [next(it) for _ in range(n_o)]
    wo_ref, g_ref, wa_ref, wg_ref, cw_ref, wout_ref = (next(it) for _ in range(6))
    gf_ref = next(it) if final else None
    out_ref, hn_ref, y_ref, prev_ref = (next(it) for _ in range(4))

    osum = o_refs[0][...]
    for o_ref in o_refs[1:]:
        osum = osum + o_ref[...]
    h = h_ref[...] + jnp.dot(osum.astype(BF16), wo_ref[...], preferred_element_type=F32)
    hn_ref[...] = _rms(h, g_ref[...]).astype(BF16)
    y_ref[...] = jnp.zeros_like(y_ref)

    @pl.when(pl.program_id(0) % ns == 0)
    def _():
        prev_ref[...] = jnp.zeros_like(prev_ref)

    rid = lax.broadcasted_iota(jnp.int32, (8, FF_CHUNK), 0)

    def conv(u, p8, w):
        u1 = pltpu.roll(u, 1, axis=0)
        u2 = pltpu.roll(u, 2, axis=0)
        f1 = jnp.where(rid < 1, pltpu.roll(p8, 1, axis=0), u1[0:8])
        f2 = jnp.where(rid < 2, pltpu.roll(p8, 2, axis=0), u2[0:8])
        u1 = jnp.concatenate([f1, u1[8:]], axis=0)
        u2 = jnp.concatenate([f2, u2[8:]], axis=0)
        return w[3:4] + w[0:1] * u2 + w[1:2] * u1 + w[2:3] * u

    def chunk(c, carry):
        hn = hn_ref[...]
        ua = jnp.dot(hn, wa_ref[c], preferred_element_type=F32)
        ug = jnp.dot(hn, wg_ref[c], preferred_element_type=F32)
        cw = cw_ref[c]
        pa, pg = prev_ref[c, 0:8], prev_ref[c, 8:16]
        prev_ref[c, 0:8] = ua[tm - 8:tm]
        prev_ref[c, 8:16] = ug[tm - 8:tm]
        ca = conv(ua, pa, cw[0:4])
        cg = conv(ug, pg, cw[4:8])
        act = (ca * jax.nn.sigmoid(ca) * cg).astype(BF16)
        y_ref[...] += jnp.dot(act, wout_ref[c], preferred_element_type=F32)
        return carry

    lax.fori_loop(0, nchunk, chunk, 0)
    out = h + y_ref[...]
    if final:
        out = _rms(out, gf_ref[...])
    out_ref[...] = out


def _ffn_call(h, o_list, wo, gain, wa, wg, cw, wout, seq, final_gain=None):
    t_tokens, d = h.shape
    tm = min(TM, seq)
    ns = seq // tm
    nchunk = wa.shape[0]
    tile = pl.BlockSpec((tm, d), lambda i: (i, 0))
    in_specs = [tile] + [tile] * len(o_list) + [
        _const_spec(wo.shape), _const_spec((1, d)), _const_spec(wa.shape), _const_spec(wg.shape),
        _const_spec(cw.shape), _const_spec(wout.shape)]
    args = [h] + list(o_list) + [wo, gain.reshape(1, d), wa, wg, cw, wout]
    if final_gain is not None:
        in_specs.append(_const_spec((1, d)))
        args.append(final_gain.reshape(1, d))
    return pl.pallas_call(
        functools.partial(_ffn_body, n_o=len(o_list), final=final_gain is not None, tm=tm, ns=ns, nchunk=nchunk),
        grid=(t_tokens // tm,),
        in_specs=in_specs, out_specs=tile,
        out_shape=jax.ShapeDtypeStruct((t_tokens, d), F32),
        scratch_shapes=[pltpu.VMEM((tm, d), BF16), pltpu.VMEM((tm, d), F32),
                        pltpu.VMEM((nchunk, 16, FF_CHUNK), F32)],
        compiler_params=_cparams(("arbitrary",)),
        name="attn_out_ffn",
    )(*args)


def _pad_heads(w, g, width):
    d = w.shape[0]
    w3 = w.reshape(d, g, HEAD_DIM)
    return jnp.pad(w3, ((0, 0), (0, 0), (0, width - HEAD_DIM))).reshape(d, g * width)


def _rot_heads(w, g):
    d = w.shape[0]
    w3 = w.reshape(d, g, HEAD_DIM)
    return jnp.concatenate([-w3[..., HALF:], w3[..., :HALF]], axis=-1).reshape(d, g * HEAD_DIM)


def _rope_k_weights(w, g):
    return jnp.concatenate([_pad_heads(w, g, LANES), _pad_heads(_rot_heads(w, g), g, LANES)], axis=1)


def _rope_tables(seq):
    inv = jnp.float32(ROPE_THETA) ** (-jnp.arange(HALF, dtype=F32) / HALF)

    def cs(pos):
        ang = pos.astype(F32)[:, None] * inv[None, :]
        return jnp.cos(ang), jnp.sin(ang)

    def nat(c):
        return jnp.concatenate([c, c, jnp.zeros((c.shape[0], LANES - HEAD_DIM), F32)], axis=1)

    cos, sin = cs(jnp.arange(seq))
    qscale = HEAD_DIM ** -0.5 * LOG2E
    cc, sc = cs(jnp.arange(seq // CMP_STRIDE) * CMP_STRIDE + CMP_BLOCK - 1)
    return {"cn": nat(cos), "sn": nat(sin), "ct": (cos * qscale).T, "st": (sin * qscale).T,
            "cc": nat(cc), "sc": nat(sc)}


def _ffn_weights(w_in, conv_w, conv_b, w_out):
    d, two_ff = w_in.shape
    dff = two_ff // 2
    nchunk = dff // FF_CHUNK

    def chunks(w):
        return w.reshape(d, nchunk, FF_CHUNK).transpose(1, 0, 2).astype(BF16)

    wa, wg = chunks(w_in[:, :dff]), chunks(w_in[:, dff:])
    taps = jnp.concatenate([conv_w, conv_b[None, :]], axis=0)
    cw = jnp.concatenate([taps[:, :dff].reshape(4, nchunk, FF_CHUNK), taps[:, dff:].reshape(4, nchunk, FF_CHUNK)],
                         axis=0).transpose(1, 0, 2)
    return wa, wg, cw, w_out.reshape(nchunk, FF_CHUNK, d).astype(BF16)


def _nsa_attention(h, gain, w_in, cmp_pos, cmp_w1, cmp_w2, tabs, emat, batch, seq):
    g, r = A_KV_HEADS, N_HEADS // A_KV_HEADS
    d = h.shape[1]
    kvw = g * HEAD_DIM
    nq = N_HEADS * HEAD_DIM
    wq, wkc, wvc, wks, wvs, wkw, wvw, wgl = jnp.split(
        w_in, [nq, nq + kvw, nq + 2 * kvw, nq + 3 * kvw, nq + 4 * kvw, nq + 5 * kvw, nq + 6 * kvw], axis=1)
    wn = jnp.concatenate([_rope_k_weights(wks, g), _rope_k_weights(wkw, g), wkc, wvc], axis=1).astype(BF16)
    wgl = wgl.reshape(d, N_HEADS, 3).transpose(0, 2, 1).reshape(d, 3 * N_HEADS)
    wt = jnp.concatenate([wq, wgl, _pad_heads(wvs, g, VROWS), _pad_heads(wvw, g, VROWS)], axis=1).T.astype(BF16)
    kpad = g * LANES
    nat_plan = [("rope", 0, kpad), ("rope", 2 * kpad, kpad), ("plain", 4 * kpad, kvw), ("plain", 4 * kpad + kvw, kvw)]
    ng = 3 * N_HEADS
    tr_plan = [("ropeq", 0, nq), ("sigmoid", nq, ng), ("vaug", nq + ng, g * VROWS),
               ("vaug", nq + ng + g * VROWS, g * VROWS)]
    ks, kw, kc, vc, qt, gates, vst, vwt = _proj_call(
        h, gain, tabs, wn, wt, nat_plan, tr_plan, [BF16, BF16, F32, F32], batch, seq)
    ks = ks.reshape(batch, seq, kpad)
    kw = kw.reshape(batch, seq, kpad)
    gates = gates.reshape(batch, 3, g, r, seq)

    nseg = seq // CMP_STRIDE
    nb = seq // SEL_BLOCK
    seg_f = CMP_STRIDE * HEAD_DIM

    def segs(x):
        return x.reshape(batch, nseg, CMP_STRIDE, g, HEAD_DIM).transpose(0, 3, 1, 2, 4).reshape(batch, g, nseg, seg_f)

    pos = cmp_pos.reshape(2, 2, 1, seg_f)
    w1 = cmp_w1.reshape(2, 2, seg_f, cmp_w1.shape[-1]).astype(BF16)
    w2k = jnp.stack([_pad_heads(cmp_w2[0], 1, LANES), _pad_heads(_rot_heads(cmp_w2[0], 1), 1, LANES)]).astype(BF16)
    w2v = _pad_heads(cmp_w2[1], 1, VROWS).T.astype(BF16)
    kcc, vcc = _compress_call(segs(kc), segs(vc), pos, w1, w2k, w2v, tabs["cc"], tabs["sc"])
    kcc = kcc.reshape(batch, g, nb, 4, LANES).transpose(0, 1, 3, 2, 4).reshape(batch, g, nseg, LANES)
    vcc = vcc.reshape(batch, g, VROWS, nb, 4).transpose(0, 1, 2, 4, 3).reshape(batch, g, VROWS, nseg)

    o_c, mb = _cmp_call(qt, kcc, vcc, gates, batch, seq)
    o_s = _attn_call("sel", qt, ks, vst, batch, seq, g, gates=gates, branch=1, mb=mb, emat=emat)
    o_w = _attn_call("win", qt, kw, vwt, batch, seq, g, gates=gates, branch=2, window=WINDOW_A)
    t_tokens = batch * seq
    return [o.reshape(t_tokens, nq) for o in (o_c, o_s, o_w)]


def kernel(x, norm_attn, norm_ffn, a_w_in, a_cmp_pos, a_cmp_w1, a_cmp_w2, a_w_out, kv_norm, b_w_kv, b_w_q, b_sinks,
           b_w_out, ffn_w_in, ffn_conv_w, ffn_conv_b, ffn_w_out, final_norm):
    batch, seq, d = x.shape
    depth = norm_attn.shape[0]
    n_a = a_w_in.shape[0]
    tabs = _rope_tables(seq)
    mr = min(MASK_ROWS, seq // SEL_BLOCK)
    emat = (jnp.arange(mr * SEL_BLOCK)[:, None] // SEL_BLOCK == jnp.arange(mr)[None, :]).astype(BF16)
    h = x.reshape(batch * seq, d)
    k_sh = v_sh = None
    gb, rb = B_KV_HEADS, N_HEADS // B_KV_HEADS
    for layer in range(depth):
        if layer < n_a:
            o_list = _nsa_attention(h, norm_attn[layer], a_w_in[layer], a_cmp_pos[layer], a_cmp_w1[layer],
                                    a_cmp_w2[layer], tabs, emat, batch, seq)
            wo = a_w_out[layer]
        else:
            j = layer - n_a
            if k_sh is None:
                wk, wv = jnp.split(b_w_kv, 2, axis=1)
                kpad = gb * LANES
                k_sh, v_sh = _proj_call(
                    h, kv_norm, tabs, _rope_k_weights(wk, gb).astype(BF16), _pad_heads(wv, gb, VROWS).T.astype(BF16),
                    [("rope", 0, kpad)], [("vaug", 0, gb * VROWS)], [BF16], batch, seq)
                k_sh = k_sh.reshape(batch, seq, kpad)
            (qt,) = _proj_call(h, norm_attn[layer], tabs, None, b_w_q[j].T.astype(BF16), [],
                               [("ropeq", 0, N_HEADS * HEAD_DIM)], [], batch, seq)
            sinks = jnp.broadcast_to((b_sinks[j] * LOG2E).reshape(gb, 1, rb, 1), (gb, 1, rb, TQ)).reshape(gb, 1, rb * TQ)
            o = _attn_call("swa", qt, k_sh, v_sh, batch, seq, gb, sinks=sinks, window=WINDOW_B)
            o_list = [o.reshape(batch * seq, N_HEADS * HEAD_DIM)]
            wo = b_w_out[j]
        wa, wg, cw, wout = _ffn_weights(ffn_w_in[layer], ffn_conv_w[layer], ffn_conv_b[layer], ffn_w_out[layer])
        h = _ffn_call(h, o_list, wo.astype(BF16), norm_ffn[layer], wa, wg, cw, wout, seq,
                      final_gain=final_norm if layer == depth - 1 else None)
    return h.reshape(batch, seq, d)
```

```python
import functools
import math

import jax
import jax.numpy as jnp
from jax import lax
from jax.experimental import pallas as pl
from jax.experimental.pallas import tpu as pltpu

F32 = jnp.float32
BF16 = jnp.bfloat16

HEAD_DIM = 64
HALF = HEAD_DIM // 2
N_HEADS = 16
A_KV_HEADS = 4
B_KV_HEADS = 2
CMP_BLOCK = 32
CMP_STRIDE = 16
SEL_BLOCK = 64
N_SELECT = 16
WINDOW_A = 512
WINDOW_B = 128
CONV_WIDTH = 3
ROPE_THETA = 10000.0
EPS = 1e-6
FORCE = 1e6

LANES = 128
BF16_SUBLANES = 16
VMEM_LIMIT = 56 * 1024 * 1024
TQ = 128
TQ_WINDOW = 256
TK = 256
SEL_GROUP = 4
TM = 512
FF_CHUNK = 256
VROWS = HEAD_DIM + BF16_SUBLANES
MASK_ROWS = 128

LOG2E = 1.4426950408889634
NEG = -1e30
M_INIT = -1e29
REMOVED = -3e38

_NT = (((1,), (1,)), ((), ()))


def _cparams(sem):
    return pltpu.CompilerParams(dimension_semantics=sem, vmem_limit_bytes=VMEM_LIMIT)


def _const_spec(shape):
    n = len(shape)
    return pl.BlockSpec(shape, lambda *_: (0,) * n, pipeline_mode=pl.Buffered(1))


def _rms(x, g):
    ms = jnp.mean(x * x, axis=-1, keepdims=True)
    return x * lax.rsqrt(ms + EPS) * g


def _proj_body(*refs, nat_plan, tr_plan, tm):
    it = iter(refs)
    h_ref, g_ref = next(it), next(it)
    if nat_plan:
        cn_ref, sn_ref, wn_ref = next(it), next(it), next(it)
    ct_ref, st_ref, wt_ref = next(it), next(it), next(it)
    outs = list(it)

    hn = _rms(h_ref[...], g_ref[...]).astype(BF16)
    oi = 0
    for kind, c0, n in nat_plan:
        o_ref = outs[oi]
        oi += 1
        y = jnp.dot(hn, wn_ref[:, c0:c0 + n], preferred_element_type=F32)
        if kind == "rope":
            yr = jnp.dot(hn, wn_ref[:, c0 + n:c0 + 2 * n], preferred_element_type=F32)
            c, s = cn_ref[...], sn_ref[...]
            for g in range(n // LANES):
                sl = slice(LANES * g, LANES * (g + 1))
                o_ref[:, sl] = (y[:, sl] * c + yr[:, sl] * s).astype(o_ref.dtype)
        else:
            o_ref[...] = y.astype(o_ref.dtype)
    for kind, r0, n in tr_plan:
        o_ref = outs[oi]
        oi += 1
        y = lax.dot_general(wt_ref[r0:r0 + n, :], hn, _NT, preferred_element_type=F32)
        if kind == "ropeq":
            c, s = ct_ref[...], st_ref[...]
            for hd in range(n // HEAD_DIM):
                a = HEAD_DIM * hd
                y1, y2 = y[a:a + HALF], y[a + HALF:a + HEAD_DIM]
                o_ref[a:a + HALF, :] = (y1 * c - y2 * s).astype(o_ref.dtype)
                o_ref[a + HALF:a + HEAD_DIM, :] = (y2 * c + y1 * s).astype(o_ref.dtype)
        elif kind == "sigmoid":
            o_ref[...] = jax.nn.sigmoid(y)
        else:
            rows = lax.broadcasted_iota(jnp.int32, (VROWS, tm), 0)
            for g in range(n // VROWS):
                yg = jnp.where(rows == HEAD_DIM, 1.0, y[VROWS * g:VROWS * (g + 1)]).astype(o_ref.dtype)
                for t in range(tm // TK):
                    o_ref[t, VROWS * g:VROWS * (g + 1), :] = yg[:, t * TK:(t + 1) * TK]


def _proj_call(h, gain, tabs, wn, wt, nat_plan, tr_plan, nat_dtypes, batch, seq):
    t_tokens, d = h.shape
    tm = min(TM, seq)
    ns = seq // tm
    in_specs = [pl.BlockSpec((tm, d), lambda i: (i, 0)), _const_spec((1, d))]
    args = [h, gain.reshape(1, d)]
    if nat_plan:
        in_specs += [pl.BlockSpec((tm, LANES), lambda i: (i % ns, 0)),
                     pl.BlockSpec((tm, LANES), lambda i: (i % ns, 0)),
                     _const_spec(wn.shape)]
        args += [tabs["cn"], tabs["sn"], wn]
    in_specs += [pl.BlockSpec((HALF, tm), lambda i: (0, i % ns)),
                 pl.BlockSpec((HALF, tm), lambda i: (0, i % ns)),
                 _const_spec(wt.shape)]
    args += [tabs["ct"], tabs["st"], wt]
    out_shape, out_specs = [], []
    for (kind, _, n), dt in zip(nat_plan, nat_dtypes):
        out_shape.append(jax.ShapeDtypeStruct((t_tokens, n), dt))
        out_specs.append(pl.BlockSpec((tm, n), lambda i: (i, 0)))
    for kind, _, n in tr_plan:
        if kind == "vaug":
            out_shape.append(jax.ShapeDtypeStruct((batch, seq // TK, n, TK), BF16))
            out_specs.append(pl.BlockSpec((None, tm // TK, n, TK), lambda i: (i // ns, i % ns, 0, 0)))
        else:
            dt = F32 if kind == "sigmoid" else BF16
            out_shape.append(jax.ShapeDtypeStruct((batch, n, seq), dt))
            out_specs.append(pl.BlockSpec((None, n, tm), lambda i: (i // ns, 0, i % ns)))
    return pl.pallas_call(
        functools.partial(_proj_body, nat_plan=tuple(nat_plan), tr_plan=tuple(tr_plan), tm=tm),
        grid=(t_tokens // tm,),
        in_specs=in_specs, out_specs=out_specs, out_shape=out_shape,
        compiler_params=_cparams(("parallel",)),
        name="norm_proj",
    )(*args)


def _gelu_tanh(x):
    c = math.sqrt(2.0 / math.pi)
    return x * (0.5 * (1.0 + jnp.tanh(c * (x + 0.044715 * (x * x * x)))))


def _compress_body(sk_ref, sv_ref, pos_ref, w1_ref, w2k_ref, w2v_ref, cc_ref, sc_ref, ok_ref, ov_ref, *, ncp):
    def hidden(seg, kv):
        xa = (seg + pos_ref[kv, 0]).astype(BF16)
        xb = (seg + pos_ref[kv, 1]).astype(BF16)
        a = jnp.dot(xa, w1_ref[kv, 0], preferred_element_type=F32)
        b = jnp.dot(xb, w1_ref[kv, 1], preferred_element_type=F32)
        return _gelu_tanh(a + pltpu.roll(b, ncp - 1, axis=0)).astype(BF16)

    gk = hidden(sk_ref[...], 0)
    k = jnp.dot(gk, w2k_ref[0], preferred_element_type=F32)
    kr = jnp.dot(gk, w2k_ref[1], preferred_element_type=F32)
    ok_ref[...] = (k * cc_ref[...] + kr * sc_ref[...]).astype(ok_ref.dtype)
    gv = hidden(sv_ref[...], 1)
    vt = lax.dot_general(w2v_ref[...], gv, _NT, preferred_element_type=F32)
    rows = lax.broadcasted_iota(jnp.int32, (VROWS, ncp), 0)
    ov_ref[...] = jnp.where(rows == HEAD_DIM, 1.0, vt).astype(ov_ref.dtype)


def _compress_call(segk, segv, pos, w1, w2k, w2v, cc, sc):
    b, g, ncp, f = segk.shape
    seg_spec = pl.BlockSpec((None, None, ncp, f), lambda i, j: (i, j, 0, 0))
    return pl.pallas_call(
        functools.partial(_compress_body, ncp=ncp),
        grid=(b, g),
        in_specs=[seg_spec, seg_spec, _const_spec(pos.shape), _const_spec(w1.shape), _const_spec(w2k.shape),
                  _const_spec(w2v.shape), _const_spec(cc.shape), _const_spec(sc.shape)],
        out_specs=[pl.BlockSpec((None, None, ncp, LANES), lambda i, j: (i, j, 0, 0)),
                   pl.BlockSpec((None, None, VROWS, ncp), lambda i, j: (i, j, 0, 0))],
        out_shape=[jax.ShapeDtypeStruct((b, g, ncp, LANES), BF16),
                   jax.ShapeDtypeStruct((b, g, VROWS, ncp), BF16)],
        compiler_params=_cparams(("parallel", "parallel")),
        name="compress",
    )(segk, segv, pos, w1, w2k, w2v, cc, sc)


def _load_queries(q_ref, qa_ref, r, tq):
    for rr in range(r):
        qa_ref[0:HEAD_DIM, rr * tq:(rr + 1) * tq] = q_ref[HEAD_DIM * rr:HEAD_DIM * (rr + 1), :]
    qa_ref[HEAD_DIM:LANES, :] = jnp.zeros((LANES - HEAD_DIM, r * tq), BF16)


def _store_heads(o_ref, acc, gate_ref, r, tq, colscale=None):
    inv = 1.0 / acc[HEAD_DIM:HEAD_DIM + 1, :]
    if colscale is not None:
        inv = inv * colscale
    o = acc[0:HEAD_DIM, :] * inv
    parts = []
    for rr in range(r):
        z = o[:, rr * tq:(rr + 1) * tq]
        if gate_ref is not None:
            z = z * gate_ref[rr:rr + 1, :]
        parts.append(z)
    o_ref[...] = jnp.concatenate(parts, axis=0).T


def _cmp_body(q_ref, k_ref, v_ref, gate_ref, o_ref, mb_ref, qa_ref, *, tq, nb, r):
    qi = pl.program_id(2)
    lw = r * tq
    _load_queries(q_ref, qa_ref, r, tq)
    s = jnp.dot(k_ref[...], qa_ref[...], preferred_element_type=F32)
    j_io = lax.broadcasted_iota(jnp.int32, (nb, tq), 0)
    tt = qi * tq + lax.broadcasted_iota(jnp.int32, (nb, tq), 1)
    lim = tt - (CMP_BLOCK - 1)
    sm = []
    for m in range(4):
        valid = SEL_BLOCK * j_io + CMP_STRIDE * m <= lim
        sm.append(jnp.concatenate(
            [jnp.where(valid, s[m * nb:(m + 1) * nb, rr * tq:(rr + 1) * tq], NEG) for rr in range(r)], axis=1))
    mx = jnp.max(jnp.maximum(jnp.maximum(sm[0], sm[1]), jnp.maximum(sm[2], sm[3])), axis=0, keepdims=True)
    pm = [jnp.exp2(sm[m] - mx) for m in range(4)]
    den = jnp.sum(pm[0] + pm[1] + pm[2] + pm[3], axis=0, keepdims=True)
    t_lane = qi * tq + (lax.broadcasted_iota(jnp.int32, (1, lw), 1) & (tq - 1))
    has_key = jnp.where(t_lane >= CMP_BLOCK - 1, 1.0, 0.0)
    acc = jnp.dot(v_ref[...], jnp.concatenate(pm, axis=0).astype(BF16), preferred_element_type=F32)
    _store_heads(o_ref, acc, gate_ref, r, tq, has_key)

    inv = has_key / den
    ps = []
    for m in range(4):
        pn = pm[m] * inv
        acc_h = pn[:, 0:tq]
        for rr in range(1, r):
            acc_h = acc_h + pn[:, rr * tq:(rr + 1) * tq]
        ps.append(acc_h)
    prev3 = jnp.where(j_io == 0, 0.0, pltpu.roll(ps[3], 1, axis=0))
    imp = prev3 + 2.0 * (ps[0] + ps[1] + ps[2]) + ps[3]
    cur = tt >> (SEL_BLOCK.bit_length() - 1)
    forced = (j_io == 0) | (j_io == cur) | (j_io == cur - 1)
    causal = j_io * SEL_BLOCK <= tt
    v = jnp.where(forced, FORCE, jnp.where(causal, imp, -FORCE))
    jf = j_io.astype(F32)
    for _ in range(min(N_SELECT, nb)):
        top = jnp.max(v, axis=0, keepdims=True)
        idx = jnp.min(jnp.where(v == top, jf, float(nb)), axis=0, keepdims=True)
        v = jnp.where(jf == idx, REMOVED, v)
    mb_ref[...] = jnp.where(v == REMOVED, 0.0, NEG).astype(mb_ref.dtype)


def _cmp_call(qt, kc, vct, gates, batch, seq):
    g, r = A_KV_HEADS, N_HEADS // A_KV_HEADS
    nb = seq // SEL_BLOCK
    ncp = 4 * nb
    tq = TQ_WINDOW
    return pl.pallas_call(
        functools.partial(_cmp_body, tq=tq, nb=nb, r=r),
        grid=(batch, g, seq // tq),
        in_specs=[pl.BlockSpec((None, r * HEAD_DIM, tq), lambda b, gg, q: (b, gg, q)),
                  pl.BlockSpec((None, None, ncp, LANES), lambda b, gg, q: (b, gg, 0, 0)),
                  pl.BlockSpec((None, None, VROWS, ncp), lambda b, gg, q: (b, gg, 0, 0)),
                  pl.BlockSpec((None, None, None, r, tq), lambda b, gg, q: (b, 0, gg, 0, q))],
        out_specs=[pl.BlockSpec((None, tq, r * HEAD_DIM), lambda b, gg, q: (b, q, gg)),
                   pl.BlockSpec((None, None, nb, tq), lambda b, gg, q: (b, gg, 0, q))],
        out_shape=[jax.ShapeDtypeStruct((batch, seq, N_HEADS * HEAD_DIM), F32),
                   jax.ShapeDtypeStruct((batch, g, nb, seq), BF16)],
        scratch_shapes=[pltpu.VMEM((LANES, r * tq), BF16)],
        compiler_params=_cparams(("parallel", "parallel", "parallel")),
        name="cmp_attn_topk",
    )(qt, kc, vct, gates)


def _attn_body(*refs, mode, tq, r, window, mr, nb):
    if mode == "sel":
        (q_ref, k_ref, v_ref, gate_ref, mb_ref, e_ref, o_ref, qa_ref, m_ref, acc_ref, macc_ref,
         s0_ref, s1_ref, p0_ref, p1_ref, mt0_ref, mt1_ref, cm0_ref, cm1_ref) = refs
        s_refs, p_refs, mt_refs, cm_refs = (s0_ref, s1_ref), (p0_ref, p1_ref), (mt0_ref, mt1_ref), (cm0_ref, cm1_ref)
    elif mode == "win":
        q_ref, k_ref, v_ref, gate_ref, o_ref, qa_ref, m_ref, acc_ref = refs
    else:
        q_ref, k_ref, v_ref, sink_ref, o_ref, qa_ref, m_ref, acc_ref = refs
        gate_ref = None
    qi = pl.program_id(2)
    lw = r * tq
    start = qi * tq
    jd = start // TK
    _load_queries(q_ref, qa_ref.at[0] if mode == "sel" else qa_ref, r, tq)
    if mode == "swa":
        m_ref[...] = sink_ref[...]
        rows = lax.broadcasted_iota(jnp.int32, (VROWS, lw), 0)
        acc_ref[...] = jnp.where(rows == HEAD_DIM, 1.0, 0.0)
    else:
        m_ref[...] = jnp.full((1, lw), M_INIT, F32)
        acc_ref[...] = jnp.zeros((VROWS, lw), F32)

    def step(j0, u, checks):
        row0 = pl.multiple_of(j0 * TK, TK)
        kt = k_ref[pl.ds(row0, u * TK), :]
        vt = jnp.concatenate([v_ref[j0 + i] for i in range(u)], axis=1)
        d0 = (start - row0) + (lax.broadcasted_iota(jnp.int32, (TK, tq), 1)
                               - lax.broadcasted_iota(jnp.int32, (TK, tq), 0))
        oks = []
        for i in range(u):
            lower, upper = checks[i]
            if lower and upper:
                oks.append((d0 >= i * TK) & (d0 < window + i * TK))
            elif lower:
                oks.append(d0 >= i * TK)
            elif upper:
                oks.append(d0 < window + i * TK)
            else:
                oks.append(None)
        chains = [slice(c0, c0 + tq) for c0 in range(0, lw, tq)]
        scores = [jnp.dot(kt, qa_ref[:, ch], preferred_element_type=F32) for ch in chains]
        for ch, s in zip(chains, scores):
            s = jnp.concatenate(
                [s[i * TK:(i + 1) * TK] if oks[i] is None else jnp.where(oks[i], s[i * TK:(i + 1) * TK], NEG)
                 for i in range(u)], axis=0)
            m = m_ref[:, ch]
            m_new = jnp.maximum(m, jnp.max(s, axis=0, keepdims=True))
            p = jnp.exp2(s - m_new).astype(BF16)
            acc_ref[:, ch] = acc_ref[:, ch] * jnp.exp2(m - m_new) + jnp.dot(vt, p, preferred_element_type=F32)
            m_ref[:, ch] = m_new

    if mode == "sel":
        u = p0_ref.shape[0]
        gph = mr * SEL_BLOCK // (u * TK)
        gd = jd // u
        for hf in range(1, nb // mr):
            qa_ref[hf, 0:LANES, :] = qa_ref[0, 0:LANES, :]
        for hf in range(nb // mr):
            for rr in range(r):
                qa_ref[hf, LANES:LANES + mr, rr * tq:(rr + 1) * tq] = mb_ref[hf * mr:(hf + 1) * mr, :]
        p_refs[1][...] = jnp.zeros(p_refs[1].shape, BF16)
        mt_refs[1][...] = jnp.full(mt_refs[1].shape, M_INIT, F32)
        macc_ref[...] = jnp.full((1, lw), M_INIT, F32)

        def scores(gi, slot):
            row0 = pl.multiple_of(gi * (u * TK), u * TK)
            eoff = pl.multiple_of(row0 % (mr * SEL_BLOCK), u * TK)
            qa = qa_ref[gi // gph]
            for i in range(u):
                kt = jnp.concatenate([k_ref[pl.ds(row0 + i * TK, TK), :], e_ref[pl.ds(eoff + i * TK, TK), :]], axis=1)
                s = jnp.dot(kt, qa, preferred_element_type=F32)
                s_refs[slot][i] = s
                cm_refs[slot][i] = jnp.max(s, axis=0, keepdims=True)

        def softmax(gi, slot, masked):
            m = m_ref[...]
            for i in range(u):
                if masked:
                    t_io = start + (lax.broadcasted_iota(jnp.int32, (TK, lw), 1) & (tq - 1))
                    kpos = (gi * u + i) * TK + lax.broadcasted_iota(jnp.int32, (TK, lw), 0)
                    s = jnp.where(kpos <= t_io, s_refs[slot][i], NEG)
                    m = jnp.maximum(m, jnp.max(s, axis=0, keepdims=True))
                else:
                    m = jnp.maximum(m, cm_refs[slot][i])
                    s = s_refs[slot][i]
                p_refs[slot][i] = jnp.exp2(s - m).astype(BF16)
                mt_refs[slot][i] = m
            m_ref[...] = m

        def values(gi, slot):
            acc, ma = acc_ref[...], macc_ref[...]
            for i in range(u):
                mt = mt_refs[slot][i]
                acc = acc * jnp.exp2(ma - mt) + jnp.dot(v_ref[gi * u + i], p_refs[slot][i],
                                                        preferred_element_type=F32)
                ma = mt
            acc_ref[...], macc_ref[...] = acc, ma

        def trip(gi, slot):
            scores(gi + 1, 1 - slot)
            softmax(gi, slot, False)
            values(jnp.maximum(gi - 1, 0), 1 - slot)

        def last(slot):
            softmax(gd, slot, True)
            values(jnp.maximum(gd - 1, 0), 1 - slot)
            values(gd, slot)

        scores(0, 0)

        def body(pi, c):
            trip(2 * pi, 0)
            trip(2 * pi + 1, 1)
            return c
        lax.fori_loop(0, gd // 2, body, 0)

        @pl.when(gd % 2 == 1)
        def _():
            trip(gd - 1, 0)
            last(1)

        @pl.when(gd % 2 == 0)
        def _():
            last(0)
    else:
        nt = max((s0 + tq - 1) // TK - (s0 - window + 1) // TK + 1 for s0 in range(8 * window, 8 * window + TK, tq))
        roles = tuple((i == nt - 1, (nt - i) * TK - 1 >= window) for i in range(nt))

        @pl.when(jd >= nt - 1)
        def _():
            step(jd - (nt - 1), nt, roles)

        @pl.when(jd < nt - 1)
        def _():
            step(0, nt, ((True, True),) * nt)

    _store_heads(o_ref, acc_ref[...], gate_ref, r, tq)


def _attn_call(mode, qt, k, vt, batch, seq, g, gates=None, branch=None, mb=None, emat=None, sinks=None,
               window=None):
    r = N_HEADS // g
    nb = seq // SEL_BLOCK
    mr = emat.shape[1] if mode == "sel" else 0
    nkt = seq // TK
    kc = LANES + mr
    tq = TQ if mode == "sel" else TQ_WINDOW
    in_specs = [pl.BlockSpec((None, r * HEAD_DIM, tq), lambda b, gg, q: (b, gg, q)),
                pl.BlockSpec((None, seq, LANES), lambda b, gg, q: (b, 0, gg)),
                pl.BlockSpec((None, nkt, VROWS, TK), lambda b, gg, q: (b, 0, gg, 0))]
    args = [qt, k, vt]
    if mode in ("sel", "win"):
        in_specs.append(pl.BlockSpec((None, None, None, r, tq), lambda b, gg, q: (b, branch, gg, 0, q)))
        args.append(gates)
    if mode == "sel":
        in_specs += [pl.BlockSpec((None, None, nb, tq), lambda b, gg, q: (b, gg, 0, q)), _const_spec(emat.shape)]
        args += [mb, emat]
    if mode == "swa":
        in_specs.append(pl.BlockSpec((None, 1, r * tq), lambda b, gg, q: (gg, 0, 0)))
        args.append(sinks)
    lw = r * tq
    scratch = [pltpu.VMEM((kc, lw), BF16), pltpu.VMEM((1, lw), F32), pltpu.VMEM((VROWS, lw), F32)]
    if mode == "sel":
        u = min(SEL_GROUP, mr * SEL_BLOCK // TK)
        scratch[0] = pltpu.VMEM((nb // mr, kc, lw), BF16)
        scratch += ([pltpu.VMEM((1, lw), F32)] + [pltpu.VMEM((u, TK, lw), F32)] * 2
                    + [pltpu.VMEM((u, TK, lw), BF16)] * 2 + [pltpu.VMEM((u, 1, lw), F32)] * 4)
    return pl.pallas_call(
        functools.partial(_attn_body, mode=mode, tq=tq, r=r, window=window, mr=mr, nb=nb),
        grid=(batch, g, seq // tq),
        in_specs=in_specs,
        out_specs=pl.BlockSpec((None, tq, r * HEAD_DIM), lambda b, gg, q: (b, q, gg)),
        out_shape=jax.ShapeDtypeStruct((batch, seq, N_HEADS * HEAD_DIM), F32),
        scratch_shapes=scratch,
        compiler_params=_cparams(("parallel", "parallel", "arbitrary")),
        name=mode + "_attn",
    )(*args)


def _ffn_body(*refs, n_o, final, tm, ns, nchunk):
    it = iter(refs)
    h_ref = next(it)
    o_refs = [next(it) for _ in range(n_o)]
    wo_ref, g_ref, wa_ref, wg_ref, cw_ref, wout_ref = (next(it) for _ in range(6))
    gf_ref = next(it) if final else None
    out_ref, hn_ref, y_ref, prev_ref = (next(it) for _ in range(4))
    u_refs = [next(it) for _ in range(3)]
    act_refs = [next(it) for _ in range(3)]

    osum = o_refs[0][...]
    for o_ref in o_refs[1:]:
        osum = osum + o_ref[...]
    h = h_ref[...] + jnp.dot(osum.astype(BF16), wo_ref[...], preferred_element_type=F32)
    hn_ref[...] = _rms(h, g_ref[...]).astype(BF16)
    y_ref[...] = h

    @pl.when(pl.program_id(0) % ns == 0)
    def _():
        prev_ref[...] = jnp.zeros_like(prev_ref)

    rid = lax.broadcasted_iota(jnp.int32, (8, FF_CHUNK), 0)

    def conv(u, p8, w):
        u1 = pltpu.roll(u, 1, axis=0)
        u2 = pltpu.roll(u, 2, axis=0)
        f1 = jnp.where(rid < 1, pltpu.roll(p8, 1, axis=0), u1[0:8])
        f2 = jnp.where(rid < 2, pltpu.roll(p8, 2, axis=0), u2[0:8])
        u1 = jnp.concatenate([f1, u1[8:]], axis=0)
        u2 = jnp.concatenate([f2, u2[8:]], axis=0)
        return w[3:4] + w[0:1] * u2 + w[1:2] * u1 + w[2:3] * u

    def proj_in(c, slot):
        hn = hn_ref[...]
        u_refs[slot][0] = jnp.dot(hn, wa_ref[c], preferred_element_type=F32)
        u_refs[slot][1] = jnp.dot(hn, wg_ref[c], preferred_element_type=F32)

    def gate(c, slot):
        ua, ug = u_refs[slot][0], u_refs[slot][1]
        cw = cw_ref[c]
        pa, pg = prev_ref[c, 0:8], prev_ref[c, 8:16]
        prev_ref[c, 0:8] = ua[tm - 8:tm]
        prev_ref[c, 8:16] = ug[tm - 8:tm]
        ca = conv(ua, pa, cw[0:4])
        cg = conv(ug, pg, cw[4:8])
        act_refs[slot][...] = (ca * jax.nn.sigmoid(ca) * cg).astype(BF16)

    def proj_out(c, slot):
        y_ref[...] += jnp.dot(act_refs[slot][...], wout_ref[c], preferred_element_type=F32)

    proj_in(0, 0)
    for c in range(nchunk):
        if c + 1 < nchunk:
            proj_in(c + 1, (c + 1) % 3)
        gate(c, c % 3)
        if c >= 1:
            proj_out(c - 1, (c - 1) % 3)
    proj_out(nchunk - 1, (nchunk - 1) % 3)
    out = y_ref[...]
    if final:
        out = _rms(out, gf_ref[...])
    out_ref[...] = out


def _ffn_call(h, o_list, wo, gain, wa, wg, cw, wout, seq, final_gain=None):
    t_tokens, d = h.shape
    tm = min(TM, seq)
    ns = seq // tm
    nchunk = wa.shape[0]
    tile = pl.BlockSpec((tm, d), lambda i: (i, 0))
    in_specs = [tile] + [tile] * len(o_list) + [
        _const_spec(wo.shape), _const_spec((1, d)), _const_spec(wa.shape), _const_spec(wg.shape),
        _const_spec(cw.shape), _const_spec(wout.shape)]
    args = [h] + list(o_list) + [wo, gain.reshape(1, d), wa, wg, cw, wout]
    if final_gain is not None:
        in_specs.append(_const_spec((1, d)))
        args.append(final_gain.reshape(1, d))
    return pl.pallas_call(
        functools.partial(_ffn_body, n_o=len(o_list), final=final_gain is not None, tm=tm, ns=ns, nchunk=nchunk),
        grid=(t_tokens // tm,),
        in_specs=in_specs, out_specs=tile,
        out_shape=jax.ShapeDtypeStruct((t_tokens, d), F32),
        scratch_shapes=[pltpu.VMEM((tm, d), BF16), pltpu.VMEM((tm, d), F32),
                        pltpu.VMEM((nchunk, 16, FF_CHUNK), F32)]
        + [pltpu.VMEM((2, tm, FF_CHUNK), F32)] * 3 + [pltpu.VMEM((tm, FF_CHUNK), BF16)] * 3,
        compiler_params=_cparams(("arbitrary",)),
        name="attn_out_ffn",
    )(*args)


def _pad_heads(w, g, width):
    d = w.shape[0]
    w3 = w.reshape(d, g, HEAD_DIM)
    return jnp.pad(w3, ((0, 0), (0, 0), (0, width - HEAD_DIM))).reshape(d, g * width)


def _rot_heads(w, g):
    d = w.shape[0]
    w3 = w.reshape(d, g, HEAD_DIM)
    return jnp.concatenate([-w3[..., HALF:], w3[..., :HALF]], axis=-1).reshape(d, g * HEAD_DIM)


def _rope_k_weights(w, g):
    return jnp.concatenate([_pad_heads(w, g, LANES), _pad_heads(_rot_heads(w, g), g, LANES)], axis=1)


def _rope_tables(seq):
    inv = jnp.float32(ROPE_THETA) ** (-jnp.arange(HALF, dtype=F32) / HALF)

    def cs(pos):
        ang = pos.astype(F32)[:, None] * inv[None, :]
        return jnp.cos(ang), jnp.sin(ang)

    def nat(c):
        return jnp.concatenate([c, c, jnp.zeros((c.shape[0], LANES - HEAD_DIM), F32)], axis=1)

    cos, sin = cs(jnp.arange(seq))
    qscale = HEAD_DIM ** -0.5 * LOG2E
    cc, sc = cs(jnp.arange(seq // CMP_STRIDE) * CMP_STRIDE + CMP_BLOCK - 1)
    return {"cn": nat(cos), "sn": nat(sin), "ct": (cos * qscale).T, "st": (sin * qscale).T,
            "cc": nat(cc), "sc": nat(sc)}


def _ffn_weights(w_in, conv_w, conv_b, w_out):
    d, two_ff = w_in.shape
    dff = two_ff // 2
    nchunk = dff // FF_CHUNK

    def chunks(w):
        return w.reshape(d, nchunk, FF_CHUNK).transpose(1, 0, 2).astype(BF16)

    wa, wg = chunks(w_in[:, :dff]), chunks(w_in[:, dff:])
    taps = jnp.concatenate([conv_w, conv_b[None, :]], axis=0)
    cw = jnp.concatenate([taps[:, :dff].reshape(4, nchunk, FF_CHUNK), taps[:, dff:].reshape(4, nchunk, FF_CHUNK)],
                         axis=0).transpose(1, 0, 2)
    return wa, wg, cw, w_out.reshape(nchunk, FF_CHUNK, d).astype(BF16)


def _nsa_attention(h, gain, w_in, cmp_pos, cmp_w1, cmp_w2, tabs, emat, batch, seq):
    g, r = A_KV_HEADS, N_HEADS // A_KV_HEADS
    d = h.shape[1]
    kvw = g * HEAD_DIM
    nq = N_HEADS * HEAD_DIM
    wq, wkc, wvc, wks, wvs, wkw, wvw, wgl = jnp.split(
        w_in, [nq, nq + kvw, nq + 2 * kvw, nq + 3 * kvw, nq + 4 * kvw, nq + 5 * kvw, nq + 6 * kvw], axis=1)
    wn = jnp.concatenate([_rope_k_weights(wks, g), _rope_k_weights(wkw, g), wkc, wvc], axis=1).astype(BF16)
    wgl = wgl.reshape(d, N_HEADS, 3).transpose(0, 2, 1).reshape(d, 3 * N_HEADS)
    wt = jnp.concatenate([wq, wgl, _pad_heads(wvs, g, VROWS), _pad_heads(wvw, g, VROWS)], axis=1).T.astype(BF16)
    kpad = g * LANES
    nat_plan = [("rope", 0, kpad), ("rope", 2 * kpad, kpad), ("plain", 4 * kpad, kvw), ("plain", 4 * kpad + kvw, kvw)]
    ng = 3 * N_HEADS
    tr_plan = [("ropeq", 0, nq), ("sigmoid", nq, ng), ("vaug", nq + ng, g * VROWS),
               ("vaug", nq + ng + g * VROWS, g * VROWS)]
    ks, kw, kc, vc, qt, gates, vst, vwt = _proj_call(
        h, gain, tabs, wn, wt, nat_plan, tr_plan, [BF16, BF16, F32, F32], batch, seq)
    ks = ks.reshape(batch, seq, kpad)
    kw = kw.reshape(batch, seq, kpad)
    gates = gates.reshape(batch, 3, g, r, seq)

    nseg = seq // CMP_STRIDE
    nb = seq // SEL_BLOCK
    seg_f = CMP_STRIDE * HEAD_DIM

    def segs(x):
        return x.reshape(batch, nseg, CMP_STRIDE, g, HEAD_DIM).transpose(0, 3, 1, 2, 4).reshape(batch, g, nseg, seg_f)

    pos = cmp_pos.reshape(2, 2, 1, seg_f)
    w1 = cmp_w1.reshape(2, 2, seg_f, cmp_w1.shape[-1]).astype(BF16)
    w2k = jnp.stack([_pad_heads(cmp_w2[0], 1, LANES), _pad_heads(_rot_heads(cmp_w2[0], 1), 1, LANES)]).astype(BF16)
    w2v = _pad_heads(cmp_w2[1], 1, VROWS).T.astype(BF16)
    kcc, vcc = _compress_call(segs(kc), segs(vc), pos, w1, w2k, w2v, tabs["cc"], tabs["sc"])
    kcc = kcc.reshape(batch, g, nb, 4, LANES).transpose(0, 1, 3, 2, 4).reshape(batch, g, nseg, LANES)
    vcc = vcc.reshape(batch, g, VROWS, nb, 4).transpose(0, 1, 2, 4, 3).reshape(batch, g, VROWS, nseg)

    o_c, mb = _cmp_call(qt, kcc, vcc, gates, batch, seq)
    o_s = _attn_call("sel", qt, ks, vst, batch, seq, g, gates=gates, branch=1, mb=mb, emat=emat)
    o_w = _attn_call("win", qt, kw, vwt, batch, seq, g, gates=gates, branch=2, window=WINDOW_A)
    t_tokens = batch * seq
    return [o.reshape(t_tokens, nq) for o in (o_c, o_s, o_w)]


def kernel(x, norm_attn, norm_ffn, a_w_in, a_cmp_pos, a_cmp_w1, a_cmp_w2, a_w_out, kv_norm, b_w_kv, b_w_q, b_sinks,
           b_w_out, ffn_w_in, ffn_conv_w, ffn_conv_b, ffn_w_out, final_norm):
    batch, seq, d = x.shape
    depth = norm_attn.shape[0]
    n_a = a_w_in.shape[0]
    tabs = _rope_tables(seq)
    mr = min(MASK_ROWS, seq // SEL_BLOCK)
    emat = (jnp.arange(mr * SEL_BLOCK)[:, None] // SEL_BLOCK == jnp.arange(mr)[None, :]).astype(BF16)
    h = x.reshape(batch * seq, d)
    k_sh = v_sh = None
    gb, rb = B_KV_HEADS, N_HEADS // B_KV_HEADS
    for layer in range(depth):
        if layer < n_a:
            o_list = _nsa_attention(h, norm_attn[layer], a_w_in[layer], a_cmp_pos[layer], a_cmp_w1[layer],
                                    a_cmp_w2[layer], tabs, emat, batch, seq)
            wo = a_w_out[layer]
        else:
            j = layer - n_a
            if k_sh is None:
                wk, wv = jnp.split(b_w_kv, 2, axis=1)
                kpad = gb * LANES
                k_sh, v_sh = _proj_call(
                    h, kv_norm, tabs, _rope_k_weights(wk, gb).astype(BF16), _pad_heads(wv, gb, VROWS).T.astype(BF16),
                    [("rope", 0, kpad)], [("vaug", 0, gb * VROWS)], [BF16], batch, seq)
                k_sh = k_sh.reshape(batch, seq, kpad)
            (qt,) = _proj_call(h, norm_attn[layer], tabs, None, b_w_q[j].T.astype(BF16), [],
                               [("ropeq", 0, N_HEADS * HEAD_DIM)], [], batch, seq)
            sinks = jnp.broadcast_to((b_sinks[j] * LOG2E).reshape(gb, 1, rb, 1),
                                     (gb, 1, rb, TQ_WINDOW)).reshape(gb, 1, rb * TQ_WINDOW)
            o = _attn_call("swa", qt, k_sh, v_sh, batch, seq, gb, sinks=sinks, window=WINDOW_B)
            o_list = [o.reshape(batch * seq, N_HEADS * HEAD_DIM)]
            wo = b_w_out[j]
        wa, wg, cw, wout = _ffn_weights(ffn_w_in[layer], ffn_conv_w[layer], ffn_conv_b[layer], ffn_w_out[layer])
        h = _ffn_call(h, o_list, wo.astype(BF16), norm_ffn[layer], wa, wg, cw, wout, seq,
                      final_gain=final_norm if layer == depth - 1 else None)
    return h.reshape(batch, seq, d)
```

```python
import functools
import math

import jax
import jax.numpy as jnp
from jax import lax
from jax.experimental import pallas as pl
from jax.experimental.pallas import tpu as pltpu

F32 = jnp.float32
BF16 = jnp.bfloat16

HEAD_DIM = 64
HALF = HEAD_DIM // 2
N_HEADS = 16
A_KV_HEADS = 4
B_KV_HEADS = 2
CMP_BLOCK = 32
CMP_STRIDE = 16
SEL_BLOCK = 64
N_SELECT = 16
WINDOW_A = 512
WINDOW_B = 128
CONV_WIDTH = 3
ROPE_THETA = 10000.0
EPS = 1e-6
FORCE = 1e6

LANES = 128
BF16_SUBLANES = 16
VMEM_LIMIT = 56 * 1024 * 1024
TQ = 128
TQ_WINDOW = 256
TK = 256
SEL_GROUP = 4
CMP_BUCKET_ALIGN = LANES
TM = 512
FF_CHUNK = 256
VROWS = HEAD_DIM + BF16_SUBLANES
MASK_ROWS = 128

LOG2E = 1.4426950408889634
NEG = -1e30
M_INIT = -1e29
REMOVED = -3e38

_NT = (((1,), (1,)), ((), ()))


def _cparams(sem):
    return pltpu.CompilerParams(dimension_semantics=sem, vmem_limit_bytes=VMEM_LIMIT)


def _const_spec(shape):
    n = len(shape)
    return pl.BlockSpec(shape, lambda *_: (0,) * n, pipeline_mode=pl.Buffered(1))


def _rms(x, g):
    ms = jnp.mean(x * x, axis=-1, keepdims=True)
    return x * lax.rsqrt(ms + EPS) * g


def _proj_body(*refs, nat_plan, tr_plan, tm):
    it = iter(refs)
    h_ref, g_ref = next(it), next(it)
    if nat_plan:
        cn_ref, sn_ref, wn_ref = next(it), next(it), next(it)
    ct_ref, st_ref, wt_ref = next(it), next(it), next(it)
    outs = list(it)

    hn = _rms(h_ref[...], g_ref[...]).astype(BF16)
    oi = 0
    for kind, c0, n in nat_plan:
        o_ref = outs[oi]
        oi += 1
        y = jnp.dot(hn, wn_ref[:, c0:c0 + n], preferred_element_type=F32)
        if kind == "rope":
            yr = jnp.dot(hn, wn_ref[:, c0 + n:c0 + 2 * n], preferred_element_type=F32)
            c, s = cn_ref[...], sn_ref[...]
            for g in range(n // LANES):
                sl = slice(LANES * g, LANES * (g + 1))
                o_ref[:, sl] = (y[:, sl] * c + yr[:, sl] * s).astype(o_ref.dtype)
        else:
            o_ref[...] = y.astype(o_ref.dtype)
    for kind, r0, n in tr_plan:
        o_ref = outs[oi]
        oi += 1
        y = lax.dot_general(wt_ref[r0:r0 + n, :], hn, _NT, preferred_element_type=F32)
        if kind == "ropeq":
            c, s = ct_ref[...], st_ref[...]
            for hd in range(n // HEAD_DIM):
                a = HEAD_DIM * hd
                y1, y2 = y[a:a + HALF], y[a + HALF:a + HEAD_DIM]
                o_ref[a:a + HALF, :] = (y1 * c - y2 * s).astype(o_ref.dtype)
                o_ref[a + HALF:a + HEAD_DIM, :] = (y2 * c + y1 * s).astype(o_ref.dtype)
        elif kind == "sigmoid":
            o_ref[...] = jax.nn.sigmoid(y)
        else:
            rows = lax.broadcasted_iota(jnp.int32, (VROWS, tm), 0)
            for g in range(n // VROWS):
                yg = jnp.where(rows == HEAD_DIM, 1.0, y[VROWS * g:VROWS * (g + 1)]).astype(o_ref.dtype)
                for t in range(tm // TK):
                    o_ref[t, VROWS * g:VROWS * (g + 1), :] = yg[:, t * TK:(t + 1) * TK]


def _proj_call(h, gain, tabs, wn, wt, nat_plan, tr_plan, nat_dtypes, batch, seq):
    t_tokens, d = h.shape
    tm = min(TM, seq)
    ns = seq // tm
    in_specs = [pl.BlockSpec((tm, d), lambda i: (i, 0)), _const_spec((1, d))]
    args = [h, gain.reshape(1, d)]
    if nat_plan:
        in_specs += [pl.BlockSpec((tm, LANES), lambda i: (i % ns, 0)),
                     pl.BlockSpec((tm, LANES), lambda i: (i % ns, 0)),
                     _const_spec(wn.shape)]
        args += [tabs["cn"], tabs["sn"], wn]
    in_specs += [pl.BlockSpec((HALF, tm), lambda i: (0, i % ns)),
                 pl.BlockSpec((HALF, tm), lambda i: (0, i % ns)),
                 _const_spec(wt.shape)]
    args += [tabs["ct"], tabs["st"], wt]
    out_shape, out_specs = [], []
    for (kind, _, n), dt in zip(nat_plan, nat_dtypes):
        out_shape.append(jax.ShapeDtypeStruct((t_tokens, n), dt))
        out_specs.append(pl.BlockSpec((tm, n), lambda i: (i, 0)))
    for kind, _, n in tr_plan:
        if kind == "vaug":
            out_shape.append(jax.ShapeDtypeStruct((batch, seq // TK, n, TK), BF16))
            out_specs.append(pl.BlockSpec((None, tm // TK, n, TK), lambda i: (i // ns, i % ns, 0, 0)))
        else:
            dt = F32 if kind == "sigmoid" else BF16
            out_shape.append(jax.ShapeDtypeStruct((batch, n, seq), dt))
            out_specs.append(pl.BlockSpec((None, n, tm), lambda i: (i // ns, 0, i % ns)))
    return pl.pallas_call(
        functools.partial(_proj_body, nat_plan=tuple(nat_plan), tr_plan=tuple(tr_plan), tm=tm),
        grid=(t_tokens // tm,),
        in_specs=in_specs, out_specs=out_specs, out_shape=out_shape,
        compiler_params=_cparams(("parallel",)),
        name="norm_proj",
    )(*args)


def _gelu_tanh(x):
    c = math.sqrt(2.0 / math.pi)
    return x * (0.5 * (1.0 + jnp.tanh(c * (x + 0.044715 * (x * x * x)))))


def _compress_body(sk_ref, sv_ref, pos_ref, w1_ref, w2k_ref, w2v_ref, cc_ref, sc_ref, ok_ref, ov_ref, *, ncp):
    def hidden(seg, kv):
        xa = (seg + pos_ref[kv, 0]).astype(BF16)
        xb = (seg + pos_ref[kv, 1]).astype(BF16)
        a = jnp.dot(xa, w1_ref[kv, 0], preferred_element_type=F32)
        b = jnp.dot(xb, w1_ref[kv, 1], preferred_element_type=F32)
        return _gelu_tanh(a + pltpu.roll(b, ncp - 1, axis=0)).astype(BF16)

    gk = hidden(sk_ref[...], 0)
    k = jnp.dot(gk, w2k_ref[0], preferred_element_type=F32)
    kr = jnp.dot(gk, w2k_ref[1], preferred_element_type=F32)
    ok_ref[...] = (k * cc_ref[...] + kr * sc_ref[...]).astype(ok_ref.dtype)
    gv = hidden(sv_ref[...], 1)
    vt = lax.dot_general(w2v_ref[...], gv, _NT, preferred_element_type=F32)
    rows = lax.broadcasted_iota(jnp.int32, (VROWS, ncp), 0)
    ov_ref[...] = jnp.where(rows == HEAD_DIM, 1.0, vt).astype(ov_ref.dtype)


def _compress_call(segk, segv, pos, w1, w2k, w2v, cc, sc):
    b, g, ncp, f = segk.shape
    seg_spec = pl.BlockSpec((None, None, ncp, f), lambda i, j: (i, j, 0, 0))
    return pl.pallas_call(
        functools.partial(_compress_body, ncp=ncp),
        grid=(b, g),
        in_specs=[seg_spec, seg_spec, _const_spec(pos.shape), _const_spec(w1.shape), _const_spec(w2k.shape),
                  _const_spec(w2v.shape), _const_spec(cc.shape), _const_spec(sc.shape)],
        out_specs=[pl.BlockSpec((None, None, ncp, LANES), lambda i, j: (i, j, 0, 0)),
                   pl.BlockSpec((None, None, VROWS, ncp), lambda i, j: (i, j, 0, 0))],
        out_shape=[jax.ShapeDtypeStruct((b, g, ncp, LANES), BF16),
                   jax.ShapeDtypeStruct((b, g, VROWS, ncp), BF16)],
        compiler_params=_cparams(("parallel", "parallel")),
        name="compress",
    )(segk, segv, pos, w1, w2k, w2v, cc, sc)


def _load_queries(q_ref, qa_ref, r, tq):
    for rr in range(r):
        qa_ref[0:HEAD_DIM, rr * tq:(rr + 1) * tq] = q_ref[HEAD_DIM * rr:HEAD_DIM * (rr + 1), :]
    qa_ref[HEAD_DIM:LANES, :] = jnp.zeros((LANES - HEAD_DIM, r * tq), BF16)


def _gated_heads(acc, gate_ref, r, tq, colscale=None):
    inv = 1.0 / acc[HEAD_DIM:HEAD_DIM + 1, :]
    if colscale is not None:
        inv = inv * colscale
    o = acc[0:HEAD_DIM, :] * inv
    parts = []
    for rr in range(r):
        z = o[:, rr * tq:(rr + 1) * tq]
        if gate_ref is not None:
            z = z * gate_ref[rr:rr + 1, :]
        parts.append(z)
    return jnp.concatenate(parts, axis=0)


def _store_heads(o_ref, acc, gate_ref, r, tq, colscale=None):
    o_ref[...] = _gated_heads(acc, gate_ref, r, tq, colscale).T


def _window_tiles(window, tq):
    nt = max((s0 + tq - 1) // TK - (s0 - window + 1) // TK + 1 for s0 in range(8 * window, 8 * window + TK, tq))
    return nt, tuple((i == nt - 1, (nt - i) * TK - 1 >= window) for i in range(nt))


def _window_attend(qa_of, k_ref, v_ref, j0, u, checks, *, window, start, tq, r, m0_of=None, acc0=None):
    row0 = pl.multiple_of(j0 * TK, TK)
    kt = k_ref[pl.ds(row0, u * TK), :]
    vt = jnp.concatenate([v_ref[j0 + i] for i in range(u)], axis=1)
    d0 = (start - row0) + (lax.broadcasted_iota(jnp.int32, (TK, tq), 1)
                           - lax.broadcasted_iota(jnp.int32, (TK, tq), 0))
    oks = []
    for i in range(u):
        lower, upper = checks[i]
        if lower and upper:
            oks.append((d0 >= i * TK) & (d0 < window + i * TK))
        elif lower:
            oks.append(d0 >= i * TK)
        elif upper:
            oks.append(d0 < window + i * TK)
        else:
            oks.append(None)
    cw = max(tq, 2 * LANES)
    chains = [slice(c0, c0 + cw) for c0 in range(0, r * tq, cw)]
    scores = [jnp.dot(kt, qa_of(ch), preferred_element_type=F32) for ch in chains]

    def masked(si, ok):
        if ok is None:
            return si
        return jnp.concatenate([jnp.where(ok, si[:, c0:c0 + tq], NEG) for c0 in range(0, cw, tq)], axis=1)

    outs = []
    for ch, s in zip(chains, scores):
        s = jnp.concatenate([masked(s[i * TK:(i + 1) * TK], oks[i]) for i in range(u)], axis=0)
        m = jnp.max(s, axis=0, keepdims=True)
        if m0_of is not None:
            m = jnp.maximum(m, m0_of(ch))
        acc = jnp.dot(vt, jnp.exp2(s - m).astype(BF16), preferred_element_type=F32)
        if acc0 is not None:
            acc = acc + acc0 * jnp.exp2(m0_of(ch) - m)
        outs.append(acc)
    return jnp.concatenate(outs, axis=1)


def _cmp_body(q_ref, k_ref, v_ref, gate_ref, o_ref, mb_ref, qa_ref, *, tq, nb, r):
    qi = pl.program_id(2)
    _load_queries(q_ref, qa_ref, r, tq)
    buckets = [b for b in (nb // 4, nb // 2, 3 * nb // 4) if b >= N_SELECT and b % CMP_BUCKET_ALIGN == 0] + [nb]
    lo = 0
    for rows in buckets:
        hi = rows * SEL_BLOCK // tq if rows < nb else pl.num_programs(2)

        @pl.when((qi >= lo) & (qi < hi))
        def _(rows=rows):
            _cmp_rows(q_ref, k_ref, v_ref, gate_ref, o_ref, mb_ref, qa_ref, qi, tq=tq, nb=nb, r=r, rows=rows)
        lo = hi


def _cmp_rows(q_ref, k_ref, v_ref, gate_ref, o_ref, mb_ref, qa_ref, qi, *, tq, nb, r, rows):
    lw = r * tq
    kt = jnp.concatenate([k_ref[m * nb:m * nb + rows, :] for m in range(4)], axis=0)
    vt = jnp.concatenate([v_ref[:, m * nb:m * nb + rows] for m in range(4)], axis=1)
    s = jnp.dot(kt, qa_ref[...], preferred_element_type=F32)
    if rows < nb:
        mb_ref[rows:nb, :] = jnp.full((nb - rows, tq), NEG, mb_ref.dtype)
    nb = rows
    j_io = lax.broadcasted_iota(jnp.int32, (nb, tq), 0)
    tt = qi * tq + lax.broadcasted_iota(jnp.int32, (nb, tq), 1)
    lim = tt - (CMP_BLOCK - 1)
    sm = []
    for m in range(4):
        valid = SEL_BLOCK * j_io + CMP_STRIDE * m <= lim
        sm.append(jnp.concatenate(
            [jnp.where(valid, s[m * nb:(m + 1) * nb, rr * tq:(rr + 1) * tq], NEG) for rr in range(r)], axis=1))
    mx = jnp.max(jnp.maximum(jnp.maximum(sm[0], sm[1]), jnp.maximum(sm[2], sm[3])), axis=0, keepdims=True)
    pm = [jnp.exp2(sm[m] - mx) for m in range(4)]
    den = jnp.sum(pm[0] + pm[1] + pm[2] + pm[3], axis=0, keepdims=True)
    t_lane = qi * tq + (lax.broadcasted_iota(jnp.int32, (1, lw), 1) & (tq - 1))
    has_key = jnp.where(t_lane >= CMP_BLOCK - 1, 1.0, 0.0)
    acc = jnp.dot(vt, jnp.concatenate(pm, axis=0).astype(BF16), preferred_element_type=F32)
    _store_heads(o_ref, acc, gate_ref, r, tq, has_key)

    inv = has_key / den
    ps = []
    for m in range(4):
        pn = pm[m] * inv
        acc_h = pn[:, 0:tq]
        for rr in range(1, r):
            acc_h = acc_h + pn[:, rr * tq:(rr + 1) * tq]
        ps.append(acc_h)
    prev3 = jnp.where(j_io == 0, 0.0, pltpu.roll(ps[3], 1, axis=0))
    imp = prev3 + 2.0 * (ps[0] + ps[1] + ps[2]) + ps[3]
    cur = tt >> (SEL_BLOCK.bit_length() - 1)
    forced = (j_io == 0) | (j_io == cur) | (j_io == cur - 1)
    causal = j_io * SEL_BLOCK <= tt
    v = jnp.where(forced, FORCE, jnp.where(causal, imp, -FORCE))
    jf = j_io.astype(F32)
    for _ in range(min(N_SELECT, nb)):
        top = jnp.max(v, axis=0, keepdims=True)
        idx = jnp.min(jnp.where(v == top, jf, float(nb)), axis=0, keepdims=True)
        v = jnp.where(jf == idx, REMOVED, v)
    mb_ref[0:nb, :] = jnp.where(v == REMOVED, 0.0, NEG).astype(mb_ref.dtype)


def _cmp_call(qt, kc, vct, gates, batch, seq):
    g, r = A_KV_HEADS, N_HEADS // A_KV_HEADS
    nb = seq // SEL_BLOCK
    ncp = 4 * nb
    tq = TQ_WINDOW
    return pl.pallas_call(
        functools.partial(_cmp_body, tq=tq, nb=nb, r=r),
        grid=(batch, g, seq // tq),
        in_specs=[pl.BlockSpec((None, r * HEAD_DIM, tq), lambda b, gg, q: (b, gg, q)),
                  pl.BlockSpec((None, None, ncp, LANES), lambda b, gg, q: (b, gg, 0, 0)),
                  pl.BlockSpec((None, None, VROWS, ncp), lambda b, gg, q: (b, gg, 0, 0)),
                  pl.BlockSpec((None, None, None, r, tq), lambda b, gg, q: (b, 0, gg, 0, q))],
        out_specs=[pl.BlockSpec((None, tq, r * HEAD_DIM), lambda b, gg, q: (b, q, gg)),
                   pl.BlockSpec((None, None, nb, tq), lambda b, gg, q: (b, gg, 0, q))],
        out_shape=[jax.ShapeDtypeStruct((batch, seq, N_HEADS * HEAD_DIM), F32),
                   jax.ShapeDtypeStruct((batch, g, nb, seq), BF16)],
        scratch_shapes=[pltpu.VMEM((LANES, r * tq), BF16)],
        compiler_params=_cparams(("parallel", "parallel", "parallel")),
        name="cmp_attn_topk",
    )(qt, kc, vct, gates)


def _sel_body(q_ref, k_ref, v_ref, gate_ref, mb_ref, e_ref, kw_ref, vw_ref, gatew_ref, o_ref,
              qa_ref, acc_ref, accw_ref, m_ref, macc_ref,
              s0_ref, s1_ref, p0_ref, p1_ref, mt0_ref, mt1_ref, cm0_ref, cm1_ref, *, tq, r, mr, nb, window):
    s_refs, p_refs, mt_refs, cm_refs = (s0_ref, s1_ref), (p0_ref, p1_ref), (mt0_ref, mt1_ref), (cm0_ref, cm1_ref)
    qi = pl.program_id(2)
    lw = r * tq
    start = qi * tq
    u = p0_ref.shape[0]
    gph = mr * SEL_BLOCK // (u * TK)
    gd = (start // TK) // u
    nh = nb // mr

    _load_queries(q_ref, qa_ref.at[0], r, tq)
    for hf in range(1, nh):
        qa_ref[hf, 0:LANES, :] = qa_ref[0, 0:LANES, :]
    for hf in range(nh):
        for rr in range(r):
            qa_ref[hf, LANES:LANES + mr, rr * tq:(rr + 1) * tq] = mb_ref[hf * mr:(hf + 1) * mr, :]
    acc_ref[...] = jnp.zeros((VROWS, lw), F32)

    nt, _ = _window_tiles(window, tq)
    accw_ref[...] = _window_attend(lambda ch: qa_ref[0, 0:LANES, ch], kw_ref, vw_ref,
                                   jnp.maximum(start // TK - (nt - 1), 0), nt, ((True, True),) * nt,
                                   window=window, start=start, tq=tq, r=r)

    def key_tile(gi, i):
        row0 = pl.multiple_of(gi * (u * TK), u * TK)
        eoff = pl.multiple_of(row0 % (mr * SEL_BLOCK), u * TK)
        return jnp.concatenate([k_ref[pl.ds(row0 + i * TK, TK), :], e_ref[pl.ds(eoff + i * TK, TK), :]], axis=1)

    def causal(gi, i):
        t_io = start + (lax.broadcasted_iota(jnp.int32, (TK, lw), 1) & (tq - 1))
        return (gi * u + i) * TK + lax.broadcasted_iota(jnp.int32, (TK, lw), 0) <= t_io

    def pipeline(stage_a, stage_b, stage_c):
        def trip(gi, slot):
            stage_a(gi + 1, 1 - slot)
            stage_b(gi, slot, False)
            stage_c(gi - 1, 1 - slot)

        def last(slot):
            stage_b(gd, slot, True)
            stage_c(gd - 1, 1 - slot)
            stage_c(gd, slot)

        stage_a(0, 0)

        @pl.when(gd == 0)
        def _():
            stage_b(0, 0, True)
            stage_c(0, 0)

        @pl.when(gd > 0)
        def _():
            stage_a(1, 1)
            stage_b(0, 0, False)
            n = gd - 1

            def body(pi, c):
                trip(1 + 2 * pi, 1)
                trip(2 + 2 * pi, 0)
                return c
            lax.fori_loop(0, n // 2, body, 0)

            @pl.when(n % 2 == 1)
            def _():
                trip(gd - 1, 1)
                last(0)

            @pl.when(n % 2 == 0)
            def _():
                last(1)

    m_ref[...] = jnp.full((1, lw), M_INIT, F32)
    macc_ref[...] = jnp.full((1, lw), M_INIT, F32)

    def scores(gi, slot):
        qa = qa_ref[gi // gph]
        for i in range(u):
            s = jnp.dot(key_tile(gi, i), qa, preferred_element_type=F32)
            s_refs[slot][i] = s
            cm_refs[slot][i] = jnp.max(s, axis=0, keepdims=True)

    def softmax(gi, slot, masked):
        m = m_ref[...]
        for i in range(u):
            if masked:
                s = jnp.where(causal(gi, i), s_refs[slot][i], NEG)
                m = jnp.maximum(m, jnp.max(s, axis=0, keepdims=True))
            else:
                m = jnp.maximum(m, cm_refs[slot][i])
                s = s_refs[slot][i]
            p_refs[slot][i] = jnp.exp2(s - m).astype(BF16)
            mt_refs[slot][i] = m
        m_ref[...] = m

    def values(gi, slot):
        acc, ma = acc_ref[...], macc_ref[...]
        for i in range(u):
            mt = mt_refs[slot][i]
            acc = acc * jnp.exp2(ma - mt) + jnp.dot(v_ref[gi * u + i], p_refs[slot][i],
                                                    preferred_element_type=F32)
            ma = mt
        acc_ref[...], macc_ref[...] = acc, ma

    pipeline(scores, softmax, values)
    o_ref[...] = (_gated_heads(acc_ref[...], gate_ref, r, tq) + _gated_heads(accw_ref[...], gatew_ref, r, tq)).T


def _attn_body(*refs, mode, tq, r, window):
    if mode == "win":
        q_ref, k_ref, v_ref, gate_ref, o_ref, qa_ref, acc_ref = refs
        m0_of = acc0 = None
    else:
        q_ref, k_ref, v_ref, sink_ref, o_ref, qa_ref, acc_ref = refs
        gate_ref = None
        m0_of = lambda ch: sink_ref[:, ch]
        acc0 = jnp.where(lax.broadcasted_iota(jnp.int32, (VROWS, 1), 0) == HEAD_DIM, 1.0, 0.0)
    qi = pl.program_id(2)
    start = qi * tq
    jd = start // TK
    _load_queries(q_ref, qa_ref, r, tq)

    def step(j0, checks):
        acc_ref[...] = _window_attend(lambda ch: qa_ref[:, ch], k_ref, v_ref, j0, nt, checks, window=window,
                                      start=start, tq=tq, r=r, m0_of=m0_of, acc0=acc0)

    nt, roles = _window_tiles(window, tq)

    @pl.when(jd >= nt - 1)
    def _():
        step(jd - (nt - 1), roles)

    @pl.when(jd < nt - 1)
    def _():
        step(0, ((True, True),) * nt)

    _store_heads(o_ref, acc_ref[...], gate_ref, r, tq)


def _attn_call(mode, qt, k, vt, batch, seq, g, gates=None, branch=None, mb=None, emat=None, sinks=None,
               window=None, kw=None, vwt=None):
    r = N_HEADS // g
    nb = seq // SEL_BLOCK
    mr = emat.shape[1] if mode == "sel" else 0
    nkt = seq // TK
    kc = LANES + mr
    tq = TQ if mode == "sel" else TQ_WINDOW
    k_spec = pl.BlockSpec((None, seq, LANES), lambda b, gg, q: (b, 0, gg))
    v_spec = pl.BlockSpec((None, nkt, VROWS, TK), lambda b, gg, q: (b, 0, gg, 0))
    in_specs = [pl.BlockSpec((None, r * HEAD_DIM, tq), lambda b, gg, q: (b, gg, q)), k_spec, v_spec]
    args = [qt, k, vt]
    if mode in ("sel", "win"):
        in_specs.append(pl.BlockSpec((None, None, None, r, tq), lambda b, gg, q: (b, branch, gg, 0, q)))
        args.append(gates)
    if mode == "sel":
        in_specs += [pl.BlockSpec((None, None, nb, tq), lambda b, gg, q: (b, gg, 0, q)), _const_spec(emat.shape),
                     k_spec, v_spec,
                     pl.BlockSpec((None, None, None, r, tq), lambda b, gg, q: (b, branch + 1, gg, 0, q))]
        args += [mb, emat, kw, vwt, gates]
    if mode == "swa":
        in_specs.append(pl.BlockSpec((None, 1, r * tq), lambda b, gg, q: (gg, 0, 0)))
        args.append(sinks)
    lw = r * tq
    scratch = [pltpu.VMEM((kc, lw), BF16), pltpu.VMEM((VROWS, lw), F32)]
    if mode == "sel":
        u = min(SEL_GROUP, mr * SEL_BLOCK // TK)
        scratch[0] = pltpu.VMEM((nb // mr, kc, lw), BF16)
        scratch += ([pltpu.VMEM((VROWS, lw), F32)] + [pltpu.VMEM((1, lw), F32)] * 2
                    + [pltpu.VMEM((u, TK, lw), F32)] * 2
                    + [pltpu.VMEM((u, TK, lw), BF16)] * 2 + [pltpu.VMEM((u, 1, lw), F32)] * 4)
        body = functools.partial(_sel_body, tq=tq, r=r, mr=mr, nb=nb, window=window)
    else:
        body = functools.partial(_attn_body, mode=mode, tq=tq, r=r, window=window)
    return pl.pallas_call(
        body,
        grid=(batch, g, seq // tq),
        in_specs=in_specs,
        out_specs=pl.BlockSpec((None, tq, r * HEAD_DIM), lambda b, gg, q: (b, q, gg)),
        out_shape=jax.ShapeDtypeStruct((batch, seq, N_HEADS * HEAD_DIM), F32),
        scratch_shapes=scratch,
        compiler_params=_cparams(("parallel", "parallel", "arbitrary")),
        name=mode + "_attn",
    )(*args)


def _ffn_body(*refs, n_o, final, tm, ns, nchunk):
    it = iter(refs)
    h_ref = next(it)
    o_refs = [next(it) for _ in range(n_o)]
    wo_ref, g_ref, wa_ref, wg_ref, cw_ref, wout_ref = (next(it) for _ in range(6))
    gf_ref = next(it) if final else None
    out_ref, hn_ref, y_ref, prev_ref = (next(it) for _ in range(4))
    u_refs = [next(it) for _ in range(3)]
    act_refs = [next(it) for _ in range(3)]

    osum = o_refs[0][...]
    for o_ref in o_refs[1:]:
        osum = osum + o_ref[...]
    h = h_ref[...] + jnp.dot(osum.astype(BF16), wo_ref[...], preferred_element_type=F32)
    hn_ref[...] = _rms(h, g_ref[...]).astype(BF16)
    y_ref[...] = h

    @pl.when(pl.program_id(0) % ns == 0)
    def _():
        prev_ref[...] = jnp.zeros_like(prev_ref)

    rid = lax.broadcasted_iota(jnp.int32, (8, FF_CHUNK), 0)

    def conv(u, p8, w):
        u1 = pltpu.roll(u, 1, axis=0)
        u2 = pltpu.roll(u, 2, axis=0)
        f1 = jnp.where(rid < 1, pltpu.roll(p8, 1, axis=0), u1[0:8])
        f2 = jnp.where(rid < 2, pltpu.roll(p8, 2, axis=0), u2[0:8])
        u1 = jnp.concatenate([f1, u1[8:]], axis=0)
        u2 = jnp.concatenate([f2, u2[8:]], axis=0)
        return w[3:4] + w[0:1] * u2 + w[1:2] * u1 + w[2:3] * u

    def proj_in(c, slot):
        hn = hn_ref[...]
        u_refs[slot][0] = jnp.dot(hn, wa_ref[c], preferred_element_type=F32)
        u_refs[slot][1] = jnp.dot(hn, wg_ref[c], preferred_element_type=F32)

    def gate(c, slot):
        ua, ug = u_refs[slot][0], u_refs[slot][1]
        cw = cw_ref[c]
        pa, pg = prev_ref[c, 0:8], prev_ref[c, 8:16]
        prev_ref[c, 0:8] = ua[tm - 8:tm]
        prev_ref[c, 8:16] = ug[tm - 8:tm]
        ca = conv(ua, pa, cw[0:4])
        cg = conv(ug, pg, cw[4:8])
        act_refs[slot][...] = (ca * jax.nn.sigmoid(ca) * cg).astype(BF16)

    def proj_out(c, slot):
        y_ref[...] += jnp.dot(act_refs[slot][...], wout_ref[c], preferred_element_type=F32)

    proj_in(0, 0)
    for c in range(nchunk):
        if c + 1 < nchunk:
            proj_in(c + 1, (c + 1) % 3)
        gate(c, c % 3)
        if c >= 1:
            proj_out(c - 1, (c - 1) % 3)
    proj_out(nchunk - 1, (nchunk - 1) % 3)
    out = y_ref[...]
    if final:
        out = _rms(out, gf_ref[...])
    out_ref[...] = out


def _ffn_call(h, o_list, wo, gain, wa, wg, cw, wout, seq, final_gain=None):
    t_tokens, d = h.shape
    tm = min(TM, seq)
    ns = seq // tm
    nchunk = wa.shape[0]
    tile = pl.BlockSpec((tm, d), lambda i: (i, 0))
    in_specs = [tile] + [tile] * len(o_list) + [
        _const_spec(wo.shape), _const_spec((1, d)), _const_spec(wa.shape), _const_spec(wg.shape),
        _const_spec(cw.shape), _const_spec(wout.shape)]
    args = [h] + list(o_list) + [wo, gain.reshape(1, d), wa, wg, cw, wout]
    if final_gain is not None:
        in_specs.append(_const_spec((1, d)))
        args.append(final_gain.reshape(1, d))
    return pl.pallas_call(
        functools.partial(_ffn_body, n_o=len(o_list), final=final_gain is not None, tm=tm, ns=ns, nchunk=nchunk),
        grid=(t_tokens // tm,),
        in_specs=in_specs, out_specs=tile,
        out_shape=jax.ShapeDtypeStruct((t_tokens, d), F32),
        scratch_shapes=[pltpu.VMEM((tm, d), BF16), pltpu.VMEM((tm, d), F32),
                        pltpu.VMEM((nchunk, 16, FF_CHUNK), F32)]
        + [pltpu.VMEM((2, tm, FF_CHUNK), F32)] * 3 + [pltpu.VMEM((tm, FF_CHUNK), BF16)] * 3,
        compiler_params=_cparams(("arbitrary",)),
        name="attn_out_ffn",
    )(*args)


def _pad_heads(w, g, width):
    d = w.shape[0]
    w3 = w.reshape(d, g, HEAD_DIM)
    return jnp.pad(w3, ((0, 0), (0, 0), (0, width - HEAD_DIM))).reshape(d, g * width)


def _rot_heads(w, g):
    d = w.shape[0]
    w3 = w.reshape(d, g, HEAD_DIM)
    return jnp.concatenate([-w3[..., HALF:], w3[..., :HALF]], axis=-1).reshape(d, g * HEAD_DIM)


def _rope_k_weights(w, g):
    return jnp.concatenate([_pad_heads(w, g, LANES), _pad_heads(_rot_heads(w, g), g, LANES)], axis=1)


def _rope_tables(seq):
    inv = jnp.float32(ROPE_THETA) ** (-jnp.arange(HALF, dtype=F32) / HALF)

    def cs(pos):
        ang = pos.astype(F32)[:, None] * inv[None, :]
        return jnp.cos(ang), jnp.sin(ang)

    def nat(c):
        return jnp.concatenate([c, c, jnp.zeros((c.shape[0], LANES - HEAD_DIM), F32)], axis=1)

    cos, sin = cs(jnp.arange(seq))
    qscale = HEAD_DIM ** -0.5 * LOG2E
    cc, sc = cs(jnp.arange(seq // CMP_STRIDE) * CMP_STRIDE + CMP_BLOCK - 1)
    return {"cn": nat(cos), "sn": nat(sin), "ct": (cos * qscale).T, "st": (sin * qscale).T,
            "cc": nat(cc), "sc": nat(sc)}


def _ffn_weights(w_in, conv_w, conv_b, w_out):
    d, two_ff = w_in.shape
    dff = two_ff // 2
    nchunk = dff // FF_CHUNK

    def chunks(w):
        return w.reshape(d, nchunk, FF_CHUNK).transpose(1, 0, 2).astype(BF16)

    wa, wg = chunks(w_in[:, :dff]), chunks(w_in[:, dff:])
    taps = jnp.concatenate([conv_w, conv_b[None, :]], axis=0)
    cw = jnp.concatenate([taps[:, :dff].reshape(4, nchunk, FF_CHUNK), taps[:, dff:].reshape(4, nchunk, FF_CHUNK)],
                         axis=0).transpose(1, 0, 2)
    return wa, wg, cw, w_out.reshape(nchunk, FF_CHUNK, d).astype(BF16)


def _nsa_attention(h, gain, w_in, cmp_pos, cmp_w1, cmp_w2, tabs, emat, batch, seq):
    g, r = A_KV_HEADS, N_HEADS // A_KV_HEADS
    d = h.shape[1]
    kvw = g * HEAD_DIM
    nq = N_HEADS * HEAD_DIM
    wq, wkc, wvc, wks, wvs, wkw, wvw, wgl = jnp.split(
        w_in, [nq, nq + kvw, nq + 2 * kvw, nq + 3 * kvw, nq + 4 * kvw, nq + 5 * kvw, nq + 6 * kvw], axis=1)
    wn = jnp.concatenate([_rope_k_weights(wks, g), _rope_k_weights(wkw, g), wkc, wvc], axis=1).astype(BF16)
    wgl = wgl.reshape(d, N_HEADS, 3).transpose(0, 2, 1).reshape(d, 3 * N_HEADS)
    wt = jnp.concatenate([wq, wgl, _pad_heads(wvs, g, VROWS), _pad_heads(wvw, g, VROWS)], axis=1).T.astype(BF16)
    kpad = g * LANES
    nat_plan = [("rope", 0, kpad), ("rope", 2 * kpad, kpad), ("plain", 4 * kpad, kvw), ("plain", 4 * kpad + kvw, kvw)]
    ng = 3 * N_HEADS
    tr_plan = [("ropeq", 0, nq), ("sigmoid", nq, ng), ("vaug", nq + ng, g * VROWS),
               ("vaug", nq + ng + g * VROWS, g * VROWS)]
    ks, kw, kc, vc, qt, gates, vst, vwt = _proj_call(
        h, gain, tabs, wn, wt, nat_plan, tr_plan, [BF16, BF16, F32, F32], batch, seq)
    ks = ks.reshape(batch, seq, kpad)
    kw = kw.reshape(batch, seq, kpad)
    gates = gates.reshape(batch, 3, g, r, seq)

    nseg = seq // CMP_STRIDE
    nb = seq // SEL_BLOCK
    seg_f = CMP_STRIDE * HEAD_DIM

    def segs(x):
        return x.reshape(batch, nseg, CMP_STRIDE, g, HEAD_DIM).transpose(0, 3, 1, 2, 4).reshape(batch, g, nseg, seg_f)

    pos = cmp_pos.reshape(2, 2, 1, seg_f)
    w1 = cmp_w1.reshape(2, 2, seg_f, cmp_w1.shape[-1]).astype(BF16)
    w2k = jnp.stack([_pad_heads(cmp_w2[0], 1, LANES), _pad_heads(_rot_heads(cmp_w2[0], 1), 1, LANES)]).astype(BF16)
    w2v = _pad_heads(cmp_w2[1], 1, VROWS).T.astype(BF16)
    kcc, vcc = _compress_call(segs(kc), segs(vc), pos, w1, w2k, w2v, tabs["cc"], tabs["sc"])
    kcc = kcc.reshape(batch, g, nb, 4, LANES).transpose(0, 1, 3, 2, 4).reshape(batch, g, nseg, LANES)
    vcc = vcc.reshape(batch, g, VROWS, nb, 4).transpose(0, 1, 2, 4, 3).reshape(batch, g, VROWS, nseg)

    o_c, mb = _cmp_call(qt, kcc, vcc, gates, batch, seq)
    o_sw = _attn_call("sel", qt, ks, vst, batch, seq, g, gates=gates, branch=1, mb=mb, emat=emat,
                      kw=kw, vwt=vwt, window=WINDOW_A)
    t_tokens = batch * seq
    return [o.reshape(t_tokens, nq) for o in (o_c, o_sw)]


def kernel(x, norm_attn, norm_ffn, a_w_in, a_cmp_pos, a_cmp_w1, a_cmp_w2, a_w_out, kv_norm, b_w_kv, b_w_q, b_sinks,
           b_w_out, ffn_w_in, ffn_conv_w, ffn_conv_b, ffn_w_out, final_norm):
    batch, seq, d = x.shape
    depth = norm_attn.shape[0]
    n_a = a_w_in.shape[0]
    tabs = _rope_tables(seq)
    mr = min(MASK_ROWS, seq // SEL_BLOCK)
    emat = (jnp.arange(mr * SEL_BLOCK)[:, None] // SEL_BLOCK == jnp.arange(mr)[None, :]).astype(BF16)
    h = x.reshape(batch * seq, d)
    k_sh = v_sh = None
    gb, rb = B_KV_HEADS, N_HEADS // B_KV_HEADS
    for layer in range(depth):
        if layer < n_a:
            o_list = _nsa_attention(h, norm_attn[layer], a_w_in[layer], a_cmp_pos[layer], a_cmp_w1[layer],
                                    a_cmp_w2[layer], tabs, emat, batch, seq)
            wo = a_w_out[layer]
        else:
            j = layer - n_a
            if k_sh is None:
                wk, wv = jnp.split(b_w_kv, 2, axis=1)
                kpad = gb * LANES
                k_sh, v_sh = _proj_call(
                    h, kv_norm, tabs, _rope_k_weights(wk, gb).astype(BF16), _pad_heads(wv, gb, VROWS).T.astype(BF16),
                    [("rope", 0, kpad)], [("vaug", 0, gb * VROWS)], [BF16], batch, seq)
                k_sh = k_sh.reshape(batch, seq, kpad)
            (qt,) = _proj_call(h, norm_attn[layer], tabs, None, b_w_q[j].T.astype(BF16), [],
                               [("ropeq", 0, N_HEADS * HEAD_DIM)], [], batch, seq)
            sinks = jnp.broadcast_to((b_sinks[j] * LOG2E).reshape(gb, 1, rb, 1),
                                     (gb, 1, rb, TQ_WINDOW)).reshape(gb, 1, rb * TQ_WINDOW)
            o = _attn_call("swa", qt, k_sh, v_sh, batch, seq, gb, sinks=sinks, window=WINDOW_B)
            o_list = [o.reshape(batch * seq, N_HEADS * HEAD_DIM)]
            wo = b_w_out[j]
        wa, wg, cw, wout = _ffn_weights(ffn_w_in[layer], ffn_conv_w[layer], ffn_conv_b[layer], ffn_w_out[layer])
        h = _ffn_call(h, o_list, wo.astype(BF16), norm_ffn[layer], wa, wg, cw, wout, seq,
                      final_gain=final_norm if layer == depth - 1 else None)
    return h.reshape(batch, seq, d)
```

```python
import functools
import math

import jax
import jax.numpy as jnp
from jax import lax
from jax.experimental import pallas as pl
from jax.experimental.pallas import tpu as pltpu

F32 = jnp.float32
BF16 = jnp.bfloat16

HEAD_DIM = 64
HALF = HEAD_DIM // 2
N_HEADS = 16
A_KV_HEADS = 4
B_KV_HEADS = 2
CMP_BLOCK = 32
CMP_STRIDE = 16
SEL_BLOCK = 64
N_SELECT = 16
WINDOW_A = 512
WINDOW_B = 128
CONV_WIDTH = 3
ROPE_THETA = 10000.0
EPS = 1e-6
FORCE = 1e6

LANES = 128
BF16_SUBLANES = 16
VMEM_LIMIT = 56 * 1024 * 1024
TQ = 256
TQ_WINDOW = 256
TK = 256
SEL_GROUP = 4
CMP_BUCKET_ALIGN = LANES
TM = 512
FF_CHUNK = 256
VROWS = HEAD_DIM + BF16_SUBLANES
MASK_ROWS = 128

LOG2E = 1.4426950408889634
NEG = -1e30
M_INIT = -1e29
REMOVED = -3e38

_NT = (((1,), (1,)), ((), ()))


def _cparams(sem):
    return pltpu.CompilerParams(dimension_semantics=sem, vmem_limit_bytes=VMEM_LIMIT)


def _const_spec(shape):
    n = len(shape)
    return pl.BlockSpec(shape, lambda *_: (0,) * n, pipeline_mode=pl.Buffered(1))


def _rms(x, g):
    ms = jnp.mean(x * x, axis=-1, keepdims=True)
    return x * lax.rsqrt(ms + EPS) * g


def _proj_body(*refs, nat_plan, tr_plan, tm):
    it = iter(refs)
    h_ref, g_ref = next(it), next(it)
    if nat_plan:
        cn_ref, sn_ref, wn_ref = next(it), next(it), next(it)
    ct_ref, st_ref, wt_ref = next(it), next(it), next(it)
    outs = list(it)

    hn = _rms(h_ref[...], g_ref[...]).astype(BF16)
    oi = 0
    for kind, c0, n in nat_plan:
        o_ref = outs[oi]
        oi += 1
        y = jnp.dot(hn, wn_ref[:, c0:c0 + n], preferred_element_type=F32)
        if kind == "rope":
            yr = jnp.dot(hn, wn_ref[:, c0 + n:c0 + 2 * n], preferred_element_type=F32)
            c, s = cn_ref[...], sn_ref[...]
            for g in range(n // LANES):
                sl = slice(LANES * g, LANES * (g + 1))
                o_ref[:, sl] = (y[:, sl] * c + yr[:, sl] * s).astype(o_ref.dtype)
        else:
            o_ref[...] = y.astype(o_ref.dtype)
    for kind, r0, n in tr_plan:
        o_ref = outs[oi]
        oi += 1
        y = lax.dot_general(wt_ref[r0:r0 + n, :], hn, _NT, preferred_element_type=F32)
        if kind == "ropeq":
            c, s = ct_ref[...], st_ref[...]
            for hd in range(n // HEAD_DIM):
                a = HEAD_DIM * hd
                y1, y2 = y[a:a + HALF], y[a + HALF:a + HEAD_DIM]
                o_ref[a:a + HALF, :] = (y1 * c - y2 * s).astype(o_ref.dtype)
                o_ref[a + HALF:a + HEAD_DIM, :] = (y2 * c + y1 * s).astype(o_ref.dtype)
        elif kind == "sigmoid":
            o_ref[...] = jax.nn.sigmoid(y)
        else:
            rows = lax.broadcasted_iota(jnp.int32, (VROWS, tm), 0)
            for g in range(n // VROWS):
                yg = jnp.where(rows == HEAD_DIM, 1.0, y[VROWS * g:VROWS * (g + 1)]).astype(o_ref.dtype)
                for t in range(tm // TK):
                    o_ref[t, VROWS * g:VROWS * (g + 1), :] = yg[:, t * TK:(t + 1) * TK]


def _proj_call(h, gain, tabs, wn, wt, nat_plan, tr_plan, nat_dtypes, batch, seq):
    t_tokens, d = h.shape
    tm = min(TM, seq)
    ns = seq // tm
    in_specs = [pl.BlockSpec((tm, d), lambda i: (i, 0)), _const_spec((1, d))]
    args = [h, gain.reshape(1, d)]
    if nat_plan:
        in_specs += [pl.BlockSpec((tm, LANES), lambda i: (i % ns, 0)),
                     pl.BlockSpec((tm, LANES), lambda i: (i % ns, 0)),
                     _const_spec(wn.shape)]
        args += [tabs["cn"], tabs["sn"], wn]
    in_specs += [pl.BlockSpec((HALF, tm), lambda i: (0, i % ns)),
                 pl.BlockSpec((HALF, tm), lambda i: (0, i % ns)),
                 _const_spec(wt.shape)]
    args += [tabs["ct"], tabs["st"], wt]
    out_shape, out_specs = [], []
    for (kind, _, n), dt in zip(nat_plan, nat_dtypes):
        out_shape.append(jax.ShapeDtypeStruct((t_tokens, n), dt))
        out_specs.append(pl.BlockSpec((tm, n), lambda i: (i, 0)))
    for kind, _, n in tr_plan:
        if kind == "vaug":
            out_shape.append(jax.ShapeDtypeStruct((batch, seq // TK, n, TK), BF16))
            out_specs.append(pl.BlockSpec((None, tm // TK, n, TK), lambda i: (i // ns, i % ns, 0, 0)))
        else:
            dt = F32 if kind == "sigmoid" else BF16
            out_shape.append(jax.ShapeDtypeStruct((batch, n, seq), dt))
            out_specs.append(pl.BlockSpec((None, n, tm), lambda i: (i // ns, 0, i % ns)))
    return pl.pallas_call(
        functools.partial(_proj_body, nat_plan=tuple(nat_plan), tr_plan=tuple(tr_plan), tm=tm),
        grid=(t_tokens // tm,),
        in_specs=in_specs, out_specs=out_specs, out_shape=out_shape,
        compiler_params=_cparams(("parallel",)),
        name="norm_proj",
    )(*args)


def _gelu_tanh(x):
    c = math.sqrt(2.0 / math.pi)
    return x * (0.5 * (1.0 + jnp.tanh(c * (x + 0.044715 * (x * x * x)))))


def _compress_body(sk_ref, sv_ref, pos_ref, w1_ref, w2k_ref, w2v_ref, cc_ref, sc_ref, ok_ref, ov_ref, *, ncp):
    def hidden(seg, kv):
        xa = (seg + pos_ref[kv, 0]).astype(BF16)
        xb = (seg + pos_ref[kv, 1]).astype(BF16)
        a = jnp.dot(xa, w1_ref[kv, 0], preferred_element_type=F32)
        b = jnp.dot(xb, w1_ref[kv, 1], preferred_element_type=F32)
        return _gelu_tanh(a + pltpu.roll(b, ncp - 1, axis=0)).astype(BF16)

    gk = hidden(sk_ref[...], 0)
    k = jnp.dot(gk, w2k_ref[0], preferred_element_type=F32)
    kr = jnp.dot(gk, w2k_ref[1], preferred_element_type=F32)
    ok_ref[...] = (k * cc_ref[...] + kr * sc_ref[...]).astype(ok_ref.dtype)
    gv = hidden(sv_ref[...], 1)
    vt = lax.dot_general(w2v_ref[...], gv, _NT, preferred_element_type=F32)
    rows = lax.broadcasted_iota(jnp.int32, (VROWS, ncp), 0)
    ov_ref[...] = jnp.where(rows == HEAD_DIM, 1.0, vt).astype(ov_ref.dtype)


def _compress_call(segk, segv, pos, w1, w2k, w2v, cc, sc):
    b, g, ncp, f = segk.shape
    seg_spec = pl.BlockSpec((None, None, ncp, f), lambda i, j: (i, j, 0, 0))
    return pl.pallas_call(
        functools.partial(_compress_body, ncp=ncp),
        grid=(b, g),
        in_specs=[seg_spec, seg_spec, _const_spec(pos.shape), _const_spec(w1.shape), _const_spec(w2k.shape),
                  _const_spec(w2v.shape), _const_spec(cc.shape), _const_spec(sc.shape)],
        out_specs=[pl.BlockSpec((None, None, ncp, LANES), lambda i, j: (i, j, 0, 0)),
                   pl.BlockSpec((None, None, VROWS, ncp), lambda i, j: (i, j, 0, 0))],
        out_shape=[jax.ShapeDtypeStruct((b, g, ncp, LANES), BF16),
                   jax.ShapeDtypeStruct((b, g, VROWS, ncp), BF16)],
        compiler_params=_cparams(("parallel", "parallel")),
        name="compress",
    )(segk, segv, pos, w1, w2k, w2v, cc, sc)


def _load_queries(q_ref, qa_ref, r, tq):
    for rr in range(r):
        qa_ref[0:HEAD_DIM, rr * tq:(rr + 1) * tq] = q_ref[HEAD_DIM * rr:HEAD_DIM * (rr + 1), :]
    qa_ref[HEAD_DIM:LANES, :] = jnp.zeros((LANES - HEAD_DIM, r * tq), BF16)


def _gated_heads(acc, gate_ref, r, tq, colscale=None):
    inv = 1.0 / acc[HEAD_DIM:HEAD_DIM + 1, :]
    if colscale is not None:
        inv = inv * colscale
    o = acc[0:HEAD_DIM, :] * inv
    parts = []
    for rr in range(r):
        z = o[:, rr * tq:(rr + 1) * tq]
        if gate_ref is not None:
            z = z * gate_ref[rr:rr + 1, :]
        parts.append(z)
    return jnp.concatenate(parts, axis=0)


def _store_heads(o_ref, acc, gate_ref, r, tq, colscale=None):
    o_ref[...] = _gated_heads(acc, gate_ref, r, tq, colscale).T


def _window_tiles(window, tq):
    nt = max((s0 + tq - 1) // TK - (s0 - window + 1) // TK + 1 for s0 in range(8 * window, 8 * window + TK, tq))
    return nt, tuple((i == nt - 1, (nt - i) * TK - 1 >= window) for i in range(nt))


def _window_attend(qa_of, k_ref, v_ref, j0, u, checks, *, window, start, tq, r, m0_of=None, acc0=None):
    row0 = pl.multiple_of(j0 * TK, TK)
    kt = k_ref[pl.ds(row0, u * TK), :]
    vt = jnp.concatenate([v_ref[j0 + i] for i in range(u)], axis=1)
    d0 = (start - row0) + (lax.broadcasted_iota(jnp.int32, (TK, tq), 1)
                           - lax.broadcasted_iota(jnp.int32, (TK, tq), 0))
    oks = []
    for i in range(u):
        lower, upper = checks[i]
        if lower and upper:
            oks.append((d0 >= i * TK) & (d0 < window + i * TK))
        elif lower:
            oks.append(d0 >= i * TK)
        elif upper:
            oks.append(d0 < window + i * TK)
        else:
            oks.append(None)
    cw = max(tq, 2 * LANES)
    chains = [slice(c0, c0 + cw) for c0 in range(0, r * tq, cw)]
    scores = [jnp.dot(kt, qa_of(ch), preferred_element_type=F32) for ch in chains]

    def masked(si, ok):
        if ok is None:
            return si
        return jnp.concatenate([jnp.where(ok, si[:, c0:c0 + tq], NEG) for c0 in range(0, cw, tq)], axis=1)

    outs = []
    for ch, s in zip(chains, scores):
        s = jnp.concatenate([masked(s[i * TK:(i + 1) * TK], oks[i]) for i in range(u)], axis=0)
        m = jnp.max(s, axis=0, keepdims=True)
        if m0_of is not None:
            m = jnp.maximum(m, m0_of(ch))
        acc = jnp.dot(vt, jnp.exp2(s - m).astype(BF16), preferred_element_type=F32)
        if acc0 is not None:
            acc = acc + acc0 * jnp.exp2(m0_of(ch) - m)
        outs.append(acc)
    return jnp.concatenate(outs, axis=1)


def _cmp_body(q_ref, k_ref, v_ref, gate_ref, o_ref, mb_ref, qa_ref, *, tq, nb, r):
    qi = pl.program_id(2)
    _load_queries(q_ref, qa_ref, r, tq)
    buckets = [b for b in (nb // 4, nb // 2, 3 * nb // 4) if b >= N_SELECT and b % CMP_BUCKET_ALIGN == 0] + [nb]
    args = (q_ref, k_ref, v_ref, gate_ref, o_ref, mb_ref, qa_ref, qi)

    @pl.when(qi == 0)
    def _():
        _cmp_rows(*args, tq=tq, nb=nb, r=r, rows=buckets[0], forced_distinct=False)

    lo = 1
    for rows in buckets:
        hi = rows * SEL_BLOCK // tq if rows < nb else pl.num_programs(2)

        @pl.when((qi >= lo) & (qi < hi))
        def _(rows=rows):
            _cmp_rows(*args, tq=tq, nb=nb, r=r, rows=rows, forced_distinct=True)
        lo = hi


def _cmp_rows(q_ref, k_ref, v_ref, gate_ref, o_ref, mb_ref, qa_ref, qi, *, tq, nb, r, rows, forced_distinct):
    lw = r * tq
    kt = jnp.concatenate([k_ref[m * nb:m * nb + rows, :] for m in range(4)], axis=0)
    vt = jnp.concatenate([v_ref[:, m * nb:m * nb + rows] for m in range(4)], axis=1)
    s = jnp.dot(kt, qa_ref[...], preferred_element_type=F32)
    if rows < nb:
        mb_ref[rows:nb, :] = jnp.full((nb - rows, tq), NEG, mb_ref.dtype)
    nb = rows
    j_io = lax.broadcasted_iota(jnp.int32, (nb, tq), 0)
    tt = qi * tq + lax.broadcasted_iota(jnp.int32, (nb, tq), 1)
    lim = tt - (CMP_BLOCK - 1)
    sm = []
    for m in range(4):
        valid = SEL_BLOCK * j_io + CMP_STRIDE * m <= lim
        sm.append(jnp.concatenate(
            [jnp.where(valid, s[m * nb:(m + 1) * nb, rr * tq:(rr + 1) * tq], NEG) for rr in range(r)], axis=1))
    mx = jnp.max(jnp.maximum(jnp.maximum(sm[0], sm[1]), jnp.maximum(sm[2], sm[3])), axis=0, keepdims=True)
    pm = [jnp.exp2(sm[m] - mx) for m in range(4)]
    den = jnp.sum(pm[0] + pm[1] + pm[2] + pm[3], axis=0, keepdims=True)
    t_lane = qi * tq + (lax.broadcasted_iota(jnp.int32, (1, lw), 1) & (tq - 1))
    has_key = jnp.where(t_lane >= CMP_BLOCK - 1, 1.0, 0.0)
    acc = jnp.dot(vt, jnp.concatenate(pm, axis=0).astype(BF16), preferred_element_type=F32)
    _store_heads(o_ref, acc, gate_ref, r, tq, has_key)

    inv = has_key / den
    ps = []
    for m in range(4):
        pn = pm[m] * inv
        acc_h = pn[:, 0:tq]
        for rr in range(1, r):
            acc_h = acc_h + pn[:, rr * tq:(rr + 1) * tq]
        ps.append(acc_h)
    prev3 = jnp.where(j_io == 0, 0.0, pltpu.roll(ps[3], 1, axis=0))
    imp = prev3 + 2.0 * (ps[0] + ps[1] + ps[2]) + ps[3]
    cur = tt >> (SEL_BLOCK.bit_length() - 1)
    forced = (j_io == 0) | (j_io == cur) | (j_io == cur - 1)
    causal = j_io * SEL_BLOCK <= tt
    v = jnp.where(forced, REMOVED if forced_distinct else FORCE, jnp.where(causal, imp, -FORCE))
    jf = j_io.astype(F32)
    for _ in range(min(N_SELECT, nb) - (3 if forced_distinct else 0)):
        top = jnp.max(v, axis=0, keepdims=True)
        idx = jnp.min(jnp.where(v == top, jf, float(nb)), axis=0, keepdims=True)
        v = jnp.where(jf == idx, REMOVED, v)
    mb_ref[0:nb, :] = jnp.where(v == REMOVED, 0.0, NEG).astype(mb_ref.dtype)


def _cmp_call(qt, kc, vct, gates, batch, seq):
    g, r = A_KV_HEADS, N_HEADS // A_KV_HEADS
    nb = seq // SEL_BLOCK
    ncp = 4 * nb
    tq = TQ_WINDOW
    return pl.pallas_call(
        functools.partial(_cmp_body, tq=tq, nb=nb, r=r),
        grid=(batch, g, seq // tq),
        in_specs=[pl.BlockSpec((None, r * HEAD_DIM, tq), lambda b, gg, q: (b, gg, q)),
                  pl.BlockSpec((None, None, ncp, LANES), lambda b, gg, q: (b, gg, 0, 0)),
                  pl.BlockSpec((None, None, VROWS, ncp), lambda b, gg, q: (b, gg, 0, 0)),
                  pl.BlockSpec((None, None, None, r, tq), lambda b, gg, q: (b, 0, gg, 0, q))],
        out_specs=[pl.BlockSpec((None, tq, r * HEAD_DIM), lambda b, gg, q: (b, q, gg)),
                   pl.BlockSpec((None, None, nb, tq), lambda b, gg, q: (b, gg, 0, q))],
        out_shape=[jax.ShapeDtypeStruct((batch, seq, N_HEADS * HEAD_DIM), F32),
                   jax.ShapeDtypeStruct((batch, g, nb, seq), BF16)],
        scratch_shapes=[pltpu.VMEM((LANES, r * tq), BF16)],
        compiler_params=_cparams(("parallel", "parallel", "parallel")),
        name="cmp_attn_topk",
    )(qt, kc, vct, gates)


def _sel_body(q_ref, k_ref, v_ref, gate_ref, mb_ref, e_ref, kw_ref, vw_ref, gatew_ref, o_ref,
              qa_ref, acc_ref, accw_ref, m_ref, macc_ref,
              s0_ref, s1_ref, p0_ref, p1_ref, mt0_ref, mt1_ref, cm0_ref, cm1_ref, *, tq, r, mr, nb, window):
    s_refs, p_refs, mt_refs, cm_refs = (s0_ref, s1_ref), (p0_ref, p1_ref), (mt0_ref, mt1_ref), (cm0_ref, cm1_ref)
    qi = pl.program_id(2)
    lw = r * tq
    start = qi * tq
    u = p0_ref.shape[0]
    gph = mr * SEL_BLOCK // (u * TK)
    gd = (start // TK) // u
    nh = nb // mr

    _load_queries(q_ref, qa_ref.at[0], r, tq)
    for hf in range(1, nh):
        qa_ref[hf, 0:LANES, :] = qa_ref[0, 0:LANES, :]
    for hf in range(nh):
        for rr in range(r):
            qa_ref[hf, LANES:LANES + mr, rr * tq:(rr + 1) * tq] = mb_ref[hf * mr:(hf + 1) * mr, :]
    acc_ref[...] = jnp.zeros((VROWS, lw), F32)

    nt, _ = _window_tiles(window, tq)
    accw_ref[...] = _window_attend(lambda ch: qa_ref[0, 0:LANES, ch], kw_ref, vw_ref,
                                   jnp.maximum(start // TK - (nt - 1), 0), nt, ((True, True),) * nt,
                                   window=window, start=start, tq=tq, r=r)

    def key_tile(gi, i):
        row0 = pl.multiple_of(gi * (u * TK), u * TK)
        eoff = pl.multiple_of(row0 % (mr * SEL_BLOCK), u * TK)
        return jnp.concatenate([k_ref[pl.ds(row0 + i * TK, TK), :], e_ref[pl.ds(eoff + i * TK, TK), :]], axis=1)

    def causal(gi, i):
        t_io = start + (lax.broadcasted_iota(jnp.int32, (TK, lw), 1) & (tq - 1))
        return (gi * u + i) * TK + lax.broadcasted_iota(jnp.int32, (TK, lw), 0) <= t_io

    def pipeline(stage_a, stage_b, stage_c):
        def trip(gi, slot):
            stage_a(gi + 1, 1 - slot)
            stage_b(gi, slot, False)
            stage_c(gi - 1, 1 - slot)

        def last(slot):
            stage_b(gd, slot, True)
            stage_c(gd - 1, 1 - slot)
            stage_c(gd, slot)

        stage_a(0, 0)

        @pl.when(gd == 0)
        def _():
            stage_b(0, 0, True)
            stage_c(0, 0)

        @pl.when(gd > 0)
        def _():
            stage_a(1, 1)
            stage_b(0, 0, False)
            n = gd - 1

            def body(pi, c):
                trip(1 + 2 * pi, 1)
                trip(2 + 2 * pi, 0)
                return c
            lax.fori_loop(0, n // 2, body, 0)

            @pl.when(n % 2 == 1)
            def _():
                trip(gd - 1, 1)
                last(0)

            @pl.when(n % 2 == 0)
            def _():
                last(1)

    m_ref[...] = jnp.full((1, lw), M_INIT, F32)
    macc_ref[...] = jnp.full((1, lw), M_INIT, F32)

    def scores(gi, slot):
        qa = qa_ref[gi // gph]
        for i in range(u):
            s = jnp.dot(key_tile(gi, i), qa, preferred_element_type=F32)
            s_refs[slot][i] = s
            cm_refs[slot][i] = jnp.max(s, axis=0, keepdims=True)

    def softmax(gi, slot, masked):
        m = m_ref[...]
        for i in range(u):
            if masked:
                s = jnp.where(causal(gi, i), s_refs[slot][i], NEG)
                m = jnp.maximum(m, jnp.max(s, axis=0, keepdims=True))
            else:
                m = jnp.maximum(m, cm_refs[slot][i])
                s = s_refs[slot][i]
            p_refs[slot][i] = jnp.exp2(s - m).astype(BF16)
            mt_refs[slot][i] = m
        m_ref[...] = m

    def values(gi, slot):
        acc, ma = acc_ref[...], macc_ref[...]
        for i in range(u):
            mt = mt_refs[slot][i]
            acc = acc * jnp.exp2(ma - mt) + jnp.dot(v_ref[gi * u + i], p_refs[slot][i],
                                                    preferred_element_type=F32)
            ma = mt
        acc_ref[...], macc_ref[...] = acc, ma

    pipeline(scores, softmax, values)
    o_ref[...] = (_gated_heads(acc_ref[...], gate_ref, r, tq) + _gated_heads(accw_ref[...], gatew_ref, r, tq)).T


def _attn_body(*refs, mode, tq, r, window):
    if mode == "win":
        q_ref, k_ref, v_ref, gate_ref, o_ref, qa_ref, acc_ref = refs
        m0_of = acc0 = None
    else:
        q_ref, k_ref, v_ref, sink_ref, o_ref, qa_ref, acc_ref = refs
        gate_ref = None
        m0_of = lambda ch: sink_ref[:, ch]
        acc0 = jnp.where(lax.broadcasted_iota(jnp.int32, (VROWS, 1), 0) == HEAD_DIM, 1.0, 0.0)
    qi = pl.program_id(2)
    start = qi * tq
    jd = start // TK
    _load_queries(q_ref, qa_ref, r, tq)

    def step(j0, checks):
        acc_ref[...] = _window_attend(lambda ch: qa_ref[:, ch], k_ref, v_ref, j0, nt, checks, window=window,
                                      start=start, tq=tq, r=r, m0_of=m0_of, acc0=acc0)

    nt, roles = _window_tiles(window, tq)

    @pl.when(jd >= nt - 1)
    def _():
        step(jd - (nt - 1), roles)

    @pl.when(jd < nt - 1)
    def _():
        step(0, ((True, True),) * nt)

    _store_heads(o_ref, acc_ref[...], gate_ref, r, tq)


def _attn_call(mode, qt, k, vt, batch, seq, g, gates=None, branch=None, mb=None, emat=None, sinks=None,
               window=None, kw=None, vwt=None):
    r = N_HEADS // g
    nb = seq // SEL_BLOCK
    mr = emat.shape[1] if mode == "sel" else 0
    nkt = seq // TK
    kc = LANES + mr
    tq = TQ if mode == "sel" else TQ_WINDOW
    k_spec = pl.BlockSpec((None, seq, LANES), lambda b, gg, q: (b, 0, gg))
    v_spec = pl.BlockSpec((None, nkt, VROWS, TK), lambda b, gg, q: (b, 0, gg, 0))
    in_specs = [pl.BlockSpec((None, r * HEAD_DIM, tq), lambda b, gg, q: (b, gg, q)), k_spec, v_spec]
    args = [qt, k, vt]
    if mode in ("sel", "win"):
        in_specs.append(pl.BlockSpec((None, None, None, r, tq), lambda b, gg, q: (b, branch, gg, 0, q)))
        args.append(gates)
    if mode == "sel":
        in_specs += [pl.BlockSpec((None, None, nb, tq), lambda b, gg, q: (b, gg, 0, q)), _const_spec(emat.shape),
                     k_spec, v_spec,
                     pl.BlockSpec((None, None, None, r, tq), lambda b, gg, q: (b, branch + 1, gg, 0, q))]
        args += [mb, emat, kw, vwt, gates]
    if mode == "swa":
        in_specs.append(pl.BlockSpec((None, 1, r * tq), lambda b, gg, q: (gg, 0, 0)))
        args.append(sinks)
    lw = r * tq
    scratch = [pltpu.VMEM((kc, lw), BF16), pltpu.VMEM((VROWS, lw), F32)]
    if mode == "sel":
        u = min(SEL_GROUP, mr * SEL_BLOCK // TK)
        scratch[0] = pltpu.VMEM((nb // mr, kc, lw), BF16)
        scratch += ([pltpu.VMEM((VROWS, lw), F32)] + [pltpu.VMEM((1, lw), F32)] * 2
                    + [pltpu.VMEM((u, TK, lw), F32)] * 2
                    + [pltpu.VMEM((u, TK, lw), BF16)] * 2 + [pltpu.VMEM((u, 1, lw), F32)] * 4)
        body = functools.partial(_sel_body, tq=tq, r=r, mr=mr, nb=nb, window=window)
    else:
        body = functools.partial(_attn_body, mode=mode, tq=tq, r=r, window=window)
    return pl.pallas_call(
        body,
        grid=(batch, g, seq // tq),
        in_specs=in_specs,
        out_specs=pl.BlockSpec((None, tq, r * HEAD_DIM), lambda b, gg, q: (b, q, gg)),
        out_shape=jax.ShapeDtypeStruct((batch, seq, N_HEADS * HEAD_DIM), F32),
        scratch_shapes=scratch,
        compiler_params=_cparams(("parallel", "parallel", "arbitrary")),
        name=mode + "_attn",
    )(*args)


def _ffn_body(*refs, n_o, final, tm, ns, nchunk):
    it = iter(refs)
    h_ref = next(it)
    o_refs = [next(it) for _ in range(n_o)]
    wo_ref, g_ref, wa_ref, wg_ref, cw_ref, wout_ref = (next(it) for _ in range(6))
    gf_ref = next(it) if final else None
    out_ref, hn_ref, y_ref, prev_ref = (next(it) for _ in range(4))
    u_refs = [next(it) for _ in range(3)]
    act_refs = [next(it) for _ in range(3)]

    osum = o_refs[0][...]
    for o_ref in o_refs[1:]:
        osum = osum + o_ref[...]
    h = h_ref[...] + jnp.dot(osum.astype(BF16), wo_ref[...], preferred_element_type=F32)
    hn_ref[...] = _rms(h, g_ref[...]).astype(BF16)
    y_ref[...] = h

    @pl.when(pl.program_id(0) % ns == 0)
    def _():
        prev_ref[...] = jnp.zeros_like(prev_ref)

    rid = lax.broadcasted_iota(jnp.int32, (8, FF_CHUNK), 0)

    def conv(u, p8, w):
        u1 = pltpu.roll(u, 1, axis=0)
        u2 = pltpu.roll(u, 2, axis=0)
        f1 = jnp.where(rid < 1, pltpu.roll(p8, 1, axis=0), u1[0:8])
        f2 = jnp.where(rid < 2, pltpu.roll(p8, 2, axis=0), u2[0:8])
        u1 = jnp.concatenate([f1, u1[8:]], axis=0)
        u2 = jnp.concatenate([f2, u2[8:]], axis=0)
        return w[3:4] + w[0:1] * u2 + w[1:2] * u1 + w[2:3] * u

    def proj_in(c, slot):
        hn = hn_ref[...]
        u_refs[slot][0] = jnp.dot(hn, wa_ref[c], preferred_element_type=F32)
        u_refs[slot][1] = jnp.dot(hn, wg_ref[c], preferred_element_type=F32)

    def gate(c, slot):
        ua, ug = u_refs[slot][0], u_refs[slot][1]
        cw = cw_ref[c]
        pa, pg = prev_ref[c, 0:8], prev_ref[c, 8:16]
        prev_ref[c, 0:8] = ua[tm - 8:tm]
        prev_ref[c, 8:16] = ug[tm - 8:tm]
        ca = conv(ua, pa, cw[0:4])
        cg = conv(ug, pg, cw[4:8])
        act_refs[slot][...] = (ca * jax.nn.sigmoid(ca) * cg).astype(BF16)

    def proj_out(c, slot):
        y_ref[...] += jnp.dot(act_refs[slot][...], wout_ref[c], preferred_element_type=F32)

    proj_in(0, 0)
    for c in range(nchunk):
        if c + 1 < nchunk:
            proj_in(c + 1, (c + 1) % 3)
        gate(c, c % 3)
        if c >= 1:
            proj_out(c - 1, (c - 1) % 3)
    proj_out(nchunk - 1, (nchunk - 1) % 3)
    out = y_ref[...]
    if final:
        out = _rms(out, gf_ref[...])
    out_ref[...] = out


def _ffn_call(h, o_list, wo, gain, wa, wg, cw, wout, seq, final_gain=None):
    t_tokens, d = h.shape
    tm = min(TM, seq)
    ns = seq // tm
    nchunk = wa.shape[0]
    tile = pl.BlockSpec((tm, d), lambda i: (i, 0))
    in_specs = [tile] + [tile] * len(o_list) + [
        _const_spec(wo.shape), _const_spec((1, d)), _const_spec(wa.shape), _const_spec(wg.shape),
        _const_spec(cw.shape), _const_spec(wout.shape)]
    args = [h] + list(o_list) + [wo, gain.reshape(1, d), wa, wg, cw, wout]
    if final_gain is not None:
        in_specs.append(_const_spec((1, d)))
        args.append(final_gain.reshape(1, d))
    return pl.pallas_call(
        functools.partial(_ffn_body, n_o=len(o_list), final=final_gain is not None, tm=tm, ns=ns, nchunk=nchunk),
        grid=(t_tokens // tm,),
        in_specs=in_specs, out_specs=tile,
        out_shape=jax.ShapeDtypeStruct((t_tokens, d), F32),
        scratch_shapes=[pltpu.VMEM((tm, d), BF16), pltpu.VMEM((tm, d), F32),
                        pltpu.VMEM((nchunk, 16, FF_CHUNK), F32)]
        + [pltpu.VMEM((2, tm, FF_CHUNK), F32)] * 3 + [pltpu.VMEM((tm, FF_CHUNK), BF16)] * 3,
        compiler_params=_cparams(("arbitrary",)),
        name="attn_out_ffn",
    )(*args)


def _pad_heads(w, g, width):
    d = w.shape[0]
    w3 = w.reshape(d, g, HEAD_DIM)
    return jnp.pad(w3, ((0, 0), (0, 0), (0, width - HEAD_DIM))).reshape(d, g * width)


def _rot_heads(w, g):
    d = w.shape[0]
    w3 = w.reshape(d, g, HEAD_DIM)
    return jnp.concatenate([-w3[..., HALF:], w3[..., :HALF]], axis=-1).reshape(d, g * HEAD_DIM)


def _rope_k_weights(w, g):
    return jnp.concatenate([_pad_heads(w, g, LANES), _pad_heads(_rot_heads(w, g), g, LANES)], axis=1)


def _rope_tables(seq):
    inv = jnp.float32(ROPE_THETA) ** (-jnp.arange(HALF, dtype=F32) / HALF)

    def cs(pos):
        ang = pos.astype(F32)[:, None] * inv[None, :]
        return jnp.cos(ang), jnp.sin(ang)

    def nat(c):
        return jnp.concatenate([c, c, jnp.zeros((c.shape[0], LANES - HEAD_DIM), F32)], axis=1)

    cos, sin = cs(jnp.arange(seq))
    qscale = HEAD_DIM ** -0.5 * LOG2E
    cc, sc = cs(jnp.arange(seq // CMP_STRIDE) * CMP_STRIDE + CMP_BLOCK - 1)
    return {"cn": nat(cos), "sn": nat(sin), "ct": (cos * qscale).T, "st": (sin * qscale).T,
            "cc": nat(cc), "sc": nat(sc)}


def _ffn_weights(w_in, conv_w, conv_b, w_out):
    d, two_ff = w_in.shape
    dff = two_ff // 2
    nchunk = dff // FF_CHUNK

    def chunks(w):
        return w.reshape(d, nchunk, FF_CHUNK).transpose(1, 0, 2).astype(BF16)

    wa, wg = chunks(w_in[:, :dff]), chunks(w_in[:, dff:])
    taps = jnp.concatenate([conv_w, conv_b[None, :]], axis=0)
    cw = jnp.concatenate([taps[:, :dff].reshape(4, nchunk, FF_CHUNK), taps[:, dff:].reshape(4, nchunk, FF_CHUNK)],
                         axis=0).transpose(1, 0, 2)
    return wa, wg, cw, w_out.reshape(nchunk, FF_CHUNK, d).astype(BF16)


def _nsa_attention(h, gain, w_in, cmp_pos, cmp_w1, cmp_w2, tabs, emat, batch, seq):
    g, r = A_KV_HEADS, N_HEADS // A_KV_HEADS
    d = h.shape[1]
    kvw = g * HEAD_DIM
    nq = N_HEADS * HEAD_DIM
    wq, wkc, wvc, wks, wvs, wkw, wvw, wgl = jnp.split(
        w_in, [nq, nq + kvw, nq + 2 * kvw, nq + 3 * kvw, nq + 4 * kvw, nq + 5 * kvw, nq + 6 * kvw], axis=1)
    wn = jnp.concatenate([_rope_k_weights(wks, g), _rope_k_weights(wkw, g), wkc, wvc], axis=1).astype(BF16)
    wgl = wgl.reshape(d, N_HEADS, 3).transpose(0, 2, 1).reshape(d, 3 * N_HEADS)
    wt = jnp.concatenate([wq, wgl, _pad_heads(wvs, g, VROWS), _pad_heads(wvw, g, VROWS)], axis=1).T.astype(BF16)
    kpad = g * LANES
    nat_plan = [("rope", 0, kpad), ("rope", 2 * kpad, kpad), ("plain", 4 * kpad, kvw), ("plain", 4 * kpad + kvw, kvw)]
    ng = 3 * N_HEADS
    tr_plan = [("ropeq", 0, nq), ("sigmoid", nq, ng), ("vaug", nq + ng, g * VROWS),
               ("vaug", nq + ng + g * VROWS, g * VROWS)]
    ks, kw, kc, vc, qt, gates, vst, vwt = _proj_call(
        h, gain, tabs, wn, wt, nat_plan, tr_plan, [BF16, BF16, F32, F32], batch, seq)
    ks = ks.reshape(batch, seq, kpad)
    kw = kw.reshape(batch, seq, kpad)
    gates = gates.reshape(batch, 3, g, r, seq)

    nseg = seq // CMP_STRIDE
    nb = seq // SEL_BLOCK
    seg_f = CMP_STRIDE * HEAD_DIM

    def segs(x):
        return x.reshape(batch, nseg, CMP_STRIDE, g, HEAD_DIM).transpose(0, 3, 1, 2, 4).reshape(batch, g, nseg, seg_f)

    pos = cmp_pos.reshape(2, 2, 1, seg_f)
    w1 = cmp_w1.reshape(2, 2, seg_f, cmp_w1.shape[-1]).astype(BF16)
    w2k = jnp.stack([_pad_heads(cmp_w2[0], 1, LANES), _pad_heads(_rot_heads(cmp_w2[0], 1), 1, LANES)]).astype(BF16)
    w2v = _pad_heads(cmp_w2[1], 1, VROWS).T.astype(BF16)
    kcc, vcc = _compress_call(segs(kc), segs(vc), pos, w1, w2k, w2v, tabs["cc"], tabs["sc"])
    kcc = kcc.reshape(batch, g, nb, 4, LANES).transpose(0, 1, 3, 2, 4).reshape(batch, g, nseg, LANES)
    vcc = vcc.reshape(batch, g, VROWS, nb, 4).transpose(0, 1, 2, 4, 3).reshape(batch, g, VROWS, nseg)

    o_c, mb = _cmp_call(qt, kcc, vcc, gates, batch, seq)
    o_sw = _attn_call("sel", qt, ks, vst, batch, seq, g, gates=gates, branch=1, mb=mb, emat=emat,
                      kw=kw, vwt=vwt, window=WINDOW_A)
    t_tokens = batch * seq
    return [o.reshape(t_tokens, nq) for o in (o_c, o_sw)]


def kernel(x, norm_attn, norm_ffn, a_w_in, a_cmp_pos, a_cmp_w1, a_cmp_w2, a_w_out, kv_norm, b_w_kv, b_w_q, b_sinks,
           b_w_out, ffn_w_in, ffn_conv_w, ffn_conv_b, ffn_w_out, final_norm):
    batch, seq, d = x.shape
    depth = norm_attn.shape[0]
    n_a = a_w_in.shape[0]
    tabs = _rope_tables(seq)
    mr = min(MASK_ROWS, seq // SEL_BLOCK)
    emat = (jnp.arange(mr * SEL_BLOCK)[:, None] // SEL_BLOCK == jnp.arange(mr)[None, :]).astype(BF16)
    h = x.reshape(batch * seq, d)
    k_sh = v_sh = None
    gb, rb = B_KV_HEADS, N_HEADS // B_KV_HEADS
    for layer in range(depth):
        if layer < n_a:
            o_list = _nsa_attention(h, norm_attn[layer], a_w_in[layer], a_cmp_pos[layer], a_cmp_w1[layer],
                                    a_cmp_w2[layer], tabs, emat, batch, seq)
            wo = a_w_out[layer]
        else:
            j = layer - n_a
            if k_sh is None:
                wk, wv = jnp.split(b_w_kv, 2, axis=1)
                kpad = gb * LANES
                k_sh, v_sh = _proj_call(
                    h, kv_norm, tabs, _rope_k_weights(wk, gb).astype(BF16), _pad_heads(wv, gb, VROWS).T.astype(BF16),
                    [("rope", 0, kpad)], [("vaug", 0, gb * VROWS)], [BF16], batch, seq)
                k_sh = k_sh.reshape(batch, seq, kpad)
            (qt,) = _proj_call(h, norm_attn[layer], tabs, None, b_w_q[j].T.astype(BF16), [],
                               [("ropeq", 0, N_HEADS * HEAD_DIM)], [], batch, seq)
            sinks = jnp.broadcast_to((b_sinks[j] * LOG2E).reshape(gb, 1, rb, 1),
                                     (gb, 1, rb, TQ_WINDOW)).reshape(gb, 1, rb * TQ_WINDOW)
            o = _attn_call("swa", qt, k_sh, v_sh, batch, seq, gb, sinks=sinks, window=WINDOW_B)
            o_list = [o.reshape(batch * seq, N_HEADS * HEAD_DIM)]
            wo = b_w_out[j]
        wa, wg, cw, wout = _ffn_weights(ffn_w_in[layer], ffn_conv_w[layer], ffn_conv_b[layer], ffn_w_out[layer])
        h = _ffn_call(h, o_list, wo.astype(BF16), norm_ffn[layer], wa, wg, cw, wout, seq,
                      final_gain=final_norm if layer == depth - 1 else None)
    return h.reshape(batch, seq, d)
```

```python
import functools
import math

import jax
import jax.numpy as jnp
from jax import lax
from jax.experimental import pallas as pl
from jax.experimental.pallas import tpu as pltpu

F32 = jnp.float32
BF16 = jnp.bfloat16

HEAD_DIM = 64
HALF = HEAD_DIM // 2
N_HEADS = 16
A_KV_HEADS = 4
B_KV_HEADS = 2
CMP_BLOCK = 32
CMP_STRIDE = 16
SEL_BLOCK = 64
N_SELECT = 16
WINDOW_A = 512
WINDOW_B = 128
CONV_WIDTH = 3
ROPE_THETA = 10000.0
EPS = 1e-6
FORCE = 1e6

LANES = 128
BF16_SUBLANES = 16
VMEM_LIMIT = 56 * 1024 * 1024
TQ = 256
TQ_WINDOW = 256
TK = 256
SEL_GROUP = 4
CMP_BUCKET_ALIGN = LANES
TM = 512
FF_CHUNK = 256
VROWS = HEAD_DIM + BF16_SUBLANES
MASK_ROWS = 128

LOG2E = 1.4426950408889634
NEG = -1e30
M_INIT = -1e29
REMOVED = -3e38

_NT = (((1,), (1,)), ((), ()))


def _cparams(sem):
    return pltpu.CompilerParams(dimension_semantics=sem, vmem_limit_bytes=VMEM_LIMIT)


def _const_spec(shape):
    n = len(shape)
    return pl.BlockSpec(shape, lambda *_: (0,) * n, pipeline_mode=pl.Buffered(1))


def _rms(x, g):
    ms = jnp.mean(x * x, axis=-1, keepdims=True)
    return x * lax.rsqrt(ms + EPS) * g


def _proj_body(*refs, nat_plan, tr_plan, tm):
    it = iter(refs)
    h_ref, g_ref = next(it), next(it)
    if nat_plan:
        cn_ref, sn_ref, wn_ref = next(it), next(it), next(it)
    ct_ref, st_ref, wt_ref = next(it), next(it), next(it)
    outs = list(it)

    hn = _rms(h_ref[...], g_ref[...]).astype(BF16)
    oi = 0
    for kind, c0, n in nat_plan:
        o_ref = outs[oi]
        oi += 1
        y = jnp.dot(hn, wn_ref[:, c0:c0 + n], preferred_element_type=F32)
        if kind == "rope":
            yr = jnp.dot(hn, wn_ref[:, c0 + n:c0 + 2 * n], preferred_element_type=F32)
            c, s = cn_ref[...], sn_ref[...]
            for g in range(n // LANES):
                sl = slice(LANES * g, LANES * (g + 1))
                o_ref[:, sl] = (y[:, sl] * c + yr[:, sl] * s).astype(o_ref.dtype)
        else:
            o_ref[...] = y.astype(o_ref.dtype)
    for kind, r0, n in tr_plan:
        o_ref = outs[oi]
        oi += 1
        y = lax.dot_general(wt_ref[r0:r0 + n, :], hn, _NT, preferred_element_type=F32)
        if kind == "ropeq":
            c, s = ct_ref[...], st_ref[...]
            for hd in range(n // HEAD_DIM):
                a = HEAD_DIM * hd
                y1, y2 = y[a:a + HALF], y[a + HALF:a + HEAD_DIM]
                o_ref[a:a + HALF, :] = (y1 * c - y2 * s).astype(o_ref.dtype)
                o_ref[a + HALF:a + HEAD_DIM, :] = (y2 * c + y1 * s).astype(o_ref.dtype)
        elif kind == "sigmoid":
            o_ref[...] = jax.nn.sigmoid(y)
        else:
            rows = lax.broadcasted_iota(jnp.int32, (VROWS, tm), 0)
            for g in range(n // VROWS):
                yg = jnp.where(rows == HEAD_DIM, 1.0, y[VROWS * g:VROWS * (g + 1)]).astype(o_ref.dtype)
                for t in range(tm // TK):
                    o_ref[t, VROWS * g:VROWS * (g + 1), :] = yg[:, t * TK:(t + 1) * TK]


def _proj_call(h, gain, tabs, wn, wt, nat_plan, tr_plan, nat_dtypes, batch, seq):
    t_tokens, d = h.shape
    tm = min(TM, seq)
    ns = seq // tm
    in_specs = [pl.BlockSpec((tm, d), lambda i: (i, 0)), _const_spec((1, d))]
    args = [h, gain.reshape(1, d)]
    if nat_plan:
        in_specs += [pl.BlockSpec((tm, LANES), lambda i: (i % ns, 0)),
                     pl.BlockSpec((tm, LANES), lambda i: (i % ns, 0)),
                     _const_spec(wn.shape)]
        args += [tabs["cn"], tabs["sn"], wn]
    in_specs += [pl.BlockSpec((HALF, tm), lambda i: (0, i % ns)),
                 pl.BlockSpec((HALF, tm), lambda i: (0, i % ns)),
                 _const_spec(wt.shape)]
    args += [tabs["ct"], tabs["st"], wt]
    out_shape, out_specs = [], []
    for (kind, _, n), dt in zip(nat_plan, nat_dtypes):
        out_shape.append(jax.ShapeDtypeStruct((t_tokens, n), dt))
        out_specs.append(pl.BlockSpec((tm, n), lambda i: (i, 0)))
    for kind, _, n in tr_plan:
        if kind == "vaug":
            out_shape.append(jax.ShapeDtypeStruct((batch, seq // TK, n, TK), BF16))
            out_specs.append(pl.BlockSpec((None, tm // TK, n, TK), lambda i: (i // ns, i % ns, 0, 0)))
        else:
            dt = F32 if kind == "sigmoid" else BF16
            out_shape.append(jax.ShapeDtypeStruct((batch, n, seq), dt))
            out_specs.append(pl.BlockSpec((None, n, tm), lambda i: (i // ns, 0, i % ns)))
    return pl.pallas_call(
        functools.partial(_proj_body, nat_plan=tuple(nat_plan), tr_plan=tuple(tr_plan), tm=tm),
        grid=(t_tokens // tm,),
        in_specs=in_specs, out_specs=out_specs, out_shape=out_shape,
        compiler_params=_cparams(("parallel",)),
        name="norm_proj",
    )(*args)


def _gelu_tanh(x):
    c = math.sqrt(2.0 / math.pi)
    return x * (0.5 * (1.0 + jnp.tanh(c * (x + 0.044715 * (x * x * x)))))


def _compress_body(sk_ref, sv_ref, pos_ref, w1_ref, w2k_ref, w2v_ref, cc_ref, sc_ref, ok_ref, ov_ref, *, ncp):
    def hidden(seg, kv):
        xa = (seg + pos_ref[kv, 0]).astype(BF16)
        xb = (seg + pos_ref[kv, 1]).astype(BF16)
        a = jnp.dot(xa, w1_ref[kv, 0], preferred_element_type=F32)
        b = jnp.dot(xb, w1_ref[kv, 1], preferred_element_type=F32)
        return _gelu_tanh(a + pltpu.roll(b, ncp - 1, axis=0)).astype(BF16)

    gk = hidden(sk_ref[...], 0)
    k = jnp.dot(gk, w2k_ref[0], preferred_element_type=F32)
    kr = jnp.dot(gk, w2k_ref[1], preferred_element_type=F32)
    ok_ref[...] = (k * cc_ref[...] + kr * sc_ref[...]).astype(ok_ref.dtype)
    gv = hidden(sv_ref[...], 1)
    vt = lax.dot_general(w2v_ref[...], gv, _NT, preferred_element_type=F32)
    rows = lax.broadcasted_iota(jnp.int32, (VROWS, ncp), 0)
    ov_ref[...] = jnp.where(rows == HEAD_DIM, 1.0, vt).astype(ov_ref.dtype)


def _compress_call(segk, segv, pos, w1, w2k, w2v, cc, sc):
    b, g, ncp, f = segk.shape
    seg_spec = pl.BlockSpec((None, None, ncp, f), lambda i, j: (i, j, 0, 0))
    return pl.pallas_call(
        functools.partial(_compress_body, ncp=ncp),
        grid=(b, g),
        in_specs=[seg_spec, seg_spec, _const_spec(pos.shape), _const_spec(w1.shape), _const_spec(w2k.shape),
                  _const_spec(w2v.shape), _const_spec(cc.shape), _const_spec(sc.shape)],
        out_specs=[pl.BlockSpec((None, None, ncp, LANES), lambda i, j: (i, j, 0, 0)),
                   pl.BlockSpec((None, None, VROWS, ncp), lambda i, j: (i, j, 0, 0))],
        out_shape=[jax.ShapeDtypeStruct((b, g, ncp, LANES), BF16),
                   jax.ShapeDtypeStruct((b, g, VROWS, ncp), BF16)],
        compiler_params=_cparams(("parallel", "parallel")),
        name="compress",
    )(segk, segv, pos, w1, w2k, w2v, cc, sc)


def _load_queries(q_ref, qa_ref, r, tq):
    for rr in range(r):
        qa_ref[0:HEAD_DIM, rr * tq:(rr + 1) * tq] = q_ref[HEAD_DIM * rr:HEAD_DIM * (rr + 1), :]
    qa_ref[HEAD_DIM:LANES, :] = jnp.zeros((LANES - HEAD_DIM, r * tq), BF16)


def _gated_heads(acc, gate_ref, r, tq, colscale=None):
    inv = 1.0 / acc[HEAD_DIM:HEAD_DIM + 1, :]
    if colscale is not None:
        inv = inv * colscale
    o = acc[0:HEAD_DIM, :] * inv
    parts = []
    for rr in range(r):
        z = o[:, rr * tq:(rr + 1) * tq]
        if gate_ref is not None:
            z = z * gate_ref[rr:rr + 1, :]
        parts.append(z)
    return jnp.concatenate(parts, axis=0)


def _store_heads(o_ref, acc, gate_ref, r, tq, colscale=None):
    o_ref[...] = _gated_heads(acc, gate_ref, r, tq, colscale).T


def _window_tiles(window, tq):
    nt = max((s0 + tq - 1) // TK - (s0 - window + 1) // TK + 1 for s0 in range(8 * window, 8 * window + TK, tq))
    return nt, tuple((i == nt - 1, (nt - i) * TK - 1 >= window) for i in range(nt))


def _window_attend(qa_of, k_ref, v_ref, j0, u, checks, *, window, start, tq, r, m0_of=None, acc0=None,
                   between=None):
    row0 = pl.multiple_of(j0 * TK, TK)
    kt = k_ref[pl.ds(row0, u * TK), :]
    vt = jnp.concatenate([v_ref[j0 + i] for i in range(u)], axis=1)
    d0 = (start - row0) + (lax.broadcasted_iota(jnp.int32, (TK, tq), 1)
                           - lax.broadcasted_iota(jnp.int32, (TK, tq), 0))
    oks = []
    for i in range(u):
        lower, upper = checks[i]
        if lower and upper:
            oks.append((d0 >= i * TK) & (d0 < window + i * TK))
        elif lower:
            oks.append(d0 >= i * TK)
        elif upper:
            oks.append(d0 < window + i * TK)
        else:
            oks.append(None)
    cw = max(tq, 2 * LANES)
    chains = [slice(c0, c0 + cw) for c0 in range(0, r * tq, cw)]
    scores = [jnp.dot(kt, qa_of(ch), preferred_element_type=F32) for ch in chains]
    if between is not None:
        between()

    def masked(si, ok):
        if ok is None:
            return si
        return jnp.concatenate([jnp.where(ok, si[:, c0:c0 + tq], NEG) for c0 in range(0, cw, tq)], axis=1)

    outs = []
    for ch, s in zip(chains, scores):
        s = jnp.concatenate([masked(s[i * TK:(i + 1) * TK], oks[i]) for i in range(u)], axis=0)
        m = jnp.max(s, axis=0, keepdims=True)
        if m0_of is not None:
            m = jnp.maximum(m, m0_of(ch))
        acc = jnp.dot(vt, jnp.exp2(s - m).astype(BF16), preferred_element_type=F32)
        if acc0 is not None:
            acc = acc + acc0 * jnp.exp2(m0_of(ch) - m)
        outs.append(acc)
    return jnp.concatenate(outs, axis=1)


def _cmp_body(q_ref, k_ref, v_ref, gate_ref, o_ref, mb_ref, qa_ref, *, tq, nb, r):
    qi = pl.program_id(2)
    _load_queries(q_ref, qa_ref, r, tq)
    buckets = [b for b in (nb // 4, nb // 2, 3 * nb // 4) if b >= N_SELECT and b % CMP_BUCKET_ALIGN == 0] + [nb]
    args = (q_ref, k_ref, v_ref, gate_ref, o_ref, mb_ref, qa_ref, qi)

    @pl.when(qi == 0)
    def _():
        _cmp_rows(*args, tq=tq, nb=nb, r=r, rows=buckets[0], forced_distinct=False)

    lo = 1
    for rows in buckets:
        hi = rows * SEL_BLOCK // tq if rows < nb else pl.num_programs(2)

        @pl.when((qi >= lo) & (qi < hi))
        def _(rows=rows):
            _cmp_rows(*args, tq=tq, nb=nb, r=r, rows=rows, forced_distinct=True)
        lo = hi


def _cmp_rows(q_ref, k_ref, v_ref, gate_ref, o_ref, mb_ref, qa_ref, qi, *, tq, nb, r, rows, forced_distinct):
    lw = r * tq
    kt = jnp.concatenate([k_ref[m * nb:m * nb + rows, :] for m in range(4)], axis=0)
    vt = jnp.concatenate([v_ref[:, m * nb:m * nb + rows] for m in range(4)], axis=1)
    s = jnp.dot(kt, qa_ref[...], preferred_element_type=F32)
    if rows < nb:
        mb_ref[rows:nb, :] = jnp.full((nb - rows, tq), NEG, mb_ref.dtype)
    nb = rows
    j_io = lax.broadcasted_iota(jnp.int32, (nb, tq), 0)
    tt = qi * tq + lax.broadcasted_iota(jnp.int32, (nb, tq), 1)
    lim = tt - (CMP_BLOCK - 1)
    sm = []
    for m in range(4):
        valid = SEL_BLOCK * j_io + CMP_STRIDE * m <= lim
        sm.append(jnp.concatenate(
            [jnp.where(valid, s[m * nb:(m + 1) * nb, rr * tq:(rr + 1) * tq], NEG) for rr in range(r)], axis=1))
    mx = jnp.max(jnp.maximum(jnp.maximum(sm[0], sm[1]), jnp.maximum(sm[2], sm[3])), axis=0, keepdims=True)
    pm = [jnp.exp2(sm[m] - mx) for m in range(4)]
    den = jnp.sum(pm[0] + pm[1] + pm[2] + pm[3], axis=0, keepdims=True)
    t_lane = qi * tq + (lax.broadcasted_iota(jnp.int32, (1, lw), 1) & (tq - 1))
    has_key = jnp.where(t_lane >= CMP_BLOCK - 1, 1.0, 0.0)
    acc = jnp.dot(vt, jnp.concatenate(pm, axis=0).astype(BF16), preferred_element_type=F32)
    _store_heads(o_ref, acc, gate_ref, r, tq, has_key)

    inv = has_key / den
    ps = []
    for m in range(4):
        pn = pm[m] * inv
        acc_h = pn[:, 0:tq]
        for rr in range(1, r):
            acc_h = acc_h + pn[:, rr * tq:(rr + 1) * tq]
        ps.append(acc_h)
    prev3 = jnp.where(j_io == 0, 0.0, pltpu.roll(ps[3], 1, axis=0))
    imp = prev3 + 2.0 * (ps[0] + ps[1] + ps[2]) + ps[3]
    cur = tt >> (SEL_BLOCK.bit_length() - 1)
    forced = (j_io == 0) | (j_io == cur) | (j_io == cur - 1)
    causal = j_io * SEL_BLOCK <= tt
    v = jnp.where(forced, REMOVED if forced_distinct else FORCE, jnp.where(causal, imp, -FORCE))
    jf = j_io.astype(F32)
    for _ in range(min(N_SELECT, nb) - (3 if forced_distinct else 0)):
        top = jnp.max(v, axis=0, keepdims=True)
        idx = jnp.min(jnp.where(v == top, jf, float(nb)), axis=0, keepdims=True)
        v = jnp.where(jf == idx, REMOVED, v)
    mb_ref[0:nb, :] = jnp.where(v == REMOVED, 0.0, NEG).astype(mb_ref.dtype)


def _cmp_call(qt, kc, vct, gates, batch, seq):
    g, r = A_KV_HEADS, N_HEADS // A_KV_HEADS
    nb = seq // SEL_BLOCK
    ncp = 4 * nb
    tq = TQ_WINDOW
    return pl.pallas_call(
        functools.partial(_cmp_body, tq=tq, nb=nb, r=r),
        grid=(batch, g, seq // tq),
        in_specs=[pl.BlockSpec((None, r * HEAD_DIM, tq), lambda b, gg, q: (b, gg, q)),
                  pl.BlockSpec((None, None, ncp, LANES), lambda b, gg, q: (b, gg, 0, 0)),
                  pl.BlockSpec((None, None, VROWS, ncp), lambda b, gg, q: (b, gg, 0, 0)),
                  pl.BlockSpec((None, None, None, r, tq), lambda b, gg, q: (b, 0, gg, 0, q))],
        out_specs=[pl.BlockSpec((None, tq, r * HEAD_DIM), lambda b, gg, q: (b, q, gg)),
                   pl.BlockSpec((None, None, nb, tq), lambda b, gg, q: (b, gg, 0, q))],
        out_shape=[jax.ShapeDtypeStruct((batch, seq, N_HEADS * HEAD_DIM), F32),
                   jax.ShapeDtypeStruct((batch, g, nb, seq), BF16)],
        scratch_shapes=[pltpu.VMEM((LANES, r * tq), BF16)],
        compiler_params=_cparams(("parallel", "parallel", "parallel")),
        name="cmp_attn_topk",
    )(qt, kc, vct, gates)


def _sel_body(q_ref, k_ref, v_ref, gate_ref, mb_ref, e_ref, kw_ref, vw_ref, gatew_ref, o_ref,
              qa_ref, acc_ref, accw_ref, m_ref, macc_ref,
              s0_ref, s1_ref, p0_ref, p1_ref, mt0_ref, mt1_ref, cm0_ref, cm1_ref, *, tq, r, mr, nb, window):
    s_refs, p_refs, mt_refs, cm_refs = (s0_ref, s1_ref), (p0_ref, p1_ref), (mt0_ref, mt1_ref), (cm0_ref, cm1_ref)
    qi = pl.program_id(2)
    lw = r * tq
    start = qi * tq
    u = p0_ref.shape[0]
    gph = mr * SEL_BLOCK // (u * TK)
    gd = (start // TK) // u
    nh = nb // mr

    _load_queries(q_ref, qa_ref.at[0], r, tq)
    for hf in range(1, nh):
        qa_ref[hf, 0:LANES, :] = qa_ref[0, 0:LANES, :]
    for hf in range(nh):
        for rr in range(r):
            qa_ref[hf, LANES:LANES + mr, rr * tq:(rr + 1) * tq] = mb_ref[hf * mr:(hf + 1) * mr, :]
    acc_ref[...] = jnp.zeros((VROWS, lw), F32)

    def key_tile(gi, i):
        row0 = pl.multiple_of(gi * (u * TK), u * TK)
        eoff = pl.multiple_of(row0 % (mr * SEL_BLOCK), u * TK)
        return jnp.concatenate([k_ref[pl.ds(row0 + i * TK, TK), :], e_ref[pl.ds(eoff + i * TK, TK), :]], axis=1)

    def causal(gi, i):
        t_io = start + (lax.broadcasted_iota(jnp.int32, (TK, lw), 1) & (tq - 1))
        return (gi * u + i) * TK + lax.broadcasted_iota(jnp.int32, (TK, lw), 0) <= t_io

    def pipeline(stage_a, stage_b, stage_c):
        def trip(gi, slot):
            stage_a(gi + 1, 1 - slot)
            stage_b(gi, slot, False)
            stage_c(gi - 1, 1 - slot)

        def last(slot):
            stage_b(gd, slot, True)
            stage_c(gd - 1, 1 - slot)
            stage_c(gd, slot)

        nt, _ = _window_tiles(window, tq)
        accw_ref[...] = _window_attend(lambda ch: qa_ref[0, 0:LANES, ch], kw_ref, vw_ref,
                                       jnp.maximum(start // TK - (nt - 1), 0), nt, ((True, True),) * nt,
                                       window=window, start=start, tq=tq, r=r, between=lambda: stage_a(0, 0))

        @pl.when(gd == 0)
        def _():
            stage_b(0, 0, True)
            stage_c(0, 0)

        @pl.when(gd > 0)
        def _():
            stage_a(1, 1)
            stage_b(0, 0, False)
            n = gd - 1

            def body(pi, c):
                trip(1 + 2 * pi, 1)
                trip(2 + 2 * pi, 0)
                return c
            lax.fori_loop(0, n // 2, body, 0)

            @pl.when(n % 2 == 1)
            def _():
                trip(gd - 1, 1)
                last(0)

            @pl.when(n % 2 == 0)
            def _():
                last(1)

    m_ref[...] = jnp.full((1, lw), M_INIT, F32)
    macc_ref[...] = jnp.full((1, lw), M_INIT, F32)

    def scores(gi, slot):
        qa = qa_ref[gi // gph]
        for i in range(u):
            s = jnp.dot(key_tile(gi, i), qa, preferred_element_type=F32)
            s_refs[slot][i] = s
            cm_refs[slot][i] = jnp.max(s, axis=0, keepdims=True)

    def softmax(gi, slot, masked):
        m = m_ref[...]
        for i in range(u):
            if masked:
                s = jnp.where(causal(gi, i), s_refs[slot][i], NEG)
                m = jnp.maximum(m, jnp.max(s, axis=0, keepdims=True))
            else:
                m = jnp.maximum(m, cm_refs[slot][i])
                s = s_refs[slot][i]
            p_refs[slot][i] = jnp.exp2(s - m).astype(BF16)
            mt_refs[slot][i] = m
        m_ref[...] = m

    def values(gi, slot):
        acc, ma = acc_ref[...], macc_ref[...]
        for i in range(u):
            mt = mt_refs[slot][i]
            acc = acc * jnp.exp2(ma - mt) + jnp.dot(v_ref[gi * u + i], p_refs[slot][i],
                                                    preferred_element_type=F32)
            ma = mt
        acc_ref[...], macc_ref[...] = acc, ma

    pipeline(scores, softmax, values)
    o_ref[...] = (_gated_heads(acc_ref[...], gate_ref, r, tq) + _gated_heads(accw_ref[...], gatew_ref, r, tq)).T


def _attn_body(*refs, mode, tq, r, window):
    if mode == "win":
        q_ref, k_ref, v_ref, gate_ref, o_ref, qa_ref, acc_ref = refs
        m0_of = acc0 = None
    else:
        q_ref, k_ref, v_ref, sink_ref, o_ref, qa_ref, acc_ref = refs
        gate_ref = None
        m0_of = lambda ch: sink_ref[:, ch]
        acc0 = jnp.where(lax.broadcasted_iota(jnp.int32, (VROWS, 1), 0) == HEAD_DIM, 1.0, 0.0)
    qi = pl.program_id(2)
    start = qi * tq
    jd = start // TK
    _load_queries(q_ref, qa_ref, r, tq)

    def step(j0, checks):
        acc_ref[...] = _window_attend(lambda ch: qa_ref[:, ch], k_ref, v_ref, j0, nt, checks, window=window,
                                      start=start, tq=tq, r=r, m0_of=m0_of, acc0=acc0)

    nt, roles = _window_tiles(window, tq)

    @pl.when(jd >= nt - 1)
    def _():
        step(jd - (nt - 1), roles)

    @pl.when(jd < nt - 1)
    def _():
        step(0, ((True, True),) * nt)

    _store_heads(o_ref, acc_ref[...], gate_ref, r, tq)


def _attn_call(mode, qt, k, vt, batch, seq, g, gates=None, branch=None, mb=None, emat=None, sinks=None,
               window=None, kw=None, vwt=None):
    r = N_HEADS // g
    nb = seq // SEL_BLOCK
    mr = emat.shape[1] if mode == "sel" else 0
    nkt = seq // TK
    kc = LANES + mr
    tq = TQ if mode == "sel" else TQ_WINDOW
    k_spec = pl.BlockSpec((None, seq, LANES), lambda b, gg, q: (b, 0, gg))
    v_spec = pl.BlockSpec((None, nkt, VROWS, TK), lambda b, gg, q: (b, 0, gg, 0))
    in_specs = [pl.BlockSpec((None, r * HEAD_DIM, tq), lambda b, gg, q: (b, gg, q)), k_spec, v_spec]
    args = [qt, k, vt]
    if mode in ("sel", "win"):
        in_specs.append(pl.BlockSpec((None, None, None, r, tq), lambda b, gg, q: (b, branch, gg, 0, q)))
        args.append(gates)
    if mode == "sel":
        in_specs += [pl.BlockSpec((None, None, nb, tq), lambda b, gg, q: (b, gg, 0, q)), _const_spec(emat.shape),
                     k_spec, v_spec,
                     pl.BlockSpec((None, None, None, r, tq), lambda b, gg, q: (b, branch + 1, gg, 0, q))]
        args += [mb, emat, kw, vwt, gates]
    if mode == "swa":
        in_specs.append(pl.BlockSpec((None, 1, r * tq), lambda b, gg, q: (gg, 0, 0)))
        args.append(sinks)
    lw = r * tq
    scratch = [pltpu.VMEM((kc, lw), BF16), pltpu.VMEM((VROWS, lw), F32)]
    if mode == "sel":
        u = min(SEL_GROUP, mr * SEL_BLOCK // TK)
        scratch[0] = pltpu.VMEM((nb // mr, kc, lw), BF16)
        scratch += ([pltpu.VMEM((VROWS, lw), F32)] + [pltpu.VMEM((1, lw), F32)] * 2
                    + [pltpu.VMEM((u, TK, lw), F32)] * 2
                    + [pltpu.VMEM((u, TK, lw), BF16)] * 2 + [pltpu.VMEM((u, 1, lw), F32)] * 4)
        body = functools.partial(_sel_body, tq=tq, r=r, mr=mr, nb=nb, window=window)
    else:
        body = functools.partial(_attn_body, mode=mode, tq=tq, r=r, window=window)
    return pl.pallas_call(
        body,
        grid=(batch, g, seq // tq),
        in_specs=in_specs,
        out_specs=pl.BlockSpec((None, tq, r * HEAD_DIM), lambda b, gg, q: (b, q, gg)),
        out_shape=jax.ShapeDtypeStruct((batch, seq, N_HEADS * HEAD_DIM), F32),
        scratch_shapes=scratch,
        compiler_params=_cparams(("parallel", "parallel", "arbitrary")),
        name=mode + "_attn",
    )(*args)


def _ffn_body(*refs, n_o, final, tm, ns, nchunk):
    it = iter(refs)
    h_ref = next(it)
    o_refs = [next(it) for _ in range(n_o)]
    wo_ref, g_ref, wa_ref, wg_ref, cw_ref, wout_ref = (next(it) for _ in range(6))
    gf_ref = next(it) if final else None
    out_ref, hn_ref, y_ref, prev_ref = (next(it) for _ in range(4))
    u_refs = [next(it) for _ in range(3)]
    act_refs = [next(it) for _ in range(3)]

    osum = o_refs[0][...]
    for o_ref in o_refs[1:]:
        osum = osum + o_ref[...]
    h = h_ref[...] + jnp.dot(osum.astype(BF16), wo_ref[...], preferred_element_type=F32)
    hn_ref[...] = _rms(h, g_ref[...]).astype(BF16)
    y_ref[...] = h

    @pl.when(pl.program_id(0) % ns == 0)
    def _():
        prev_ref[...] = jnp.zeros_like(prev_ref)

    rid = lax.broadcasted_iota(jnp.int32, (8, FF_CHUNK), 0)

    def conv(u, p8, w):
        u1 = pltpu.roll(u, 1, axis=0)
        u2 = pltpu.roll(u, 2, axis=0)
        f1 = jnp.where(rid < 1, pltpu.roll(p8, 1, axis=0), u1[0:8])
        f2 = jnp.where(rid < 2, pltpu.roll(p8, 2, axis=0), u2[0:8])
        u1 = jnp.concatenate([f1, u1[8:]], axis=0)
        u2 = jnp.concatenate([f2, u2[8:]], axis=0)
        return w[3:4] + w[0:1] * u2 + w[1:2] * u1 + w[2:3] * u

    def proj_in(c, slot):
        hn = hn_ref[...]
        u_refs[slot][0] = jnp.dot(hn, wa_ref[c], preferred_element_type=F32)
        u_refs[slot][1] = jnp.dot(hn, wg_ref[c], preferred_element_type=F32)

    def gate(c, slot):
        ua, ug = u_refs[slot][0], u_refs[slot][1]
        cw = cw_ref[c]
        pa, pg = prev_ref[c, 0:8], prev_ref[c, 8:16]
        prev_ref[c, 0:8] = ua[tm - 8:tm]
        prev_ref[c, 8:16] = ug[tm - 8:tm]
        ca = conv(ua, pa, cw[0:4])
        cg = conv(ug, pg, cw[4:8])
        act_refs[slot][...] = (ca * jax.nn.sigmoid(ca) * cg).astype(BF16)

    def proj_out(c, slot):
        y_ref[...] += jnp.dot(act_refs[slot][...], wout_ref[c], preferred_element_type=F32)

    proj_in(0, 0)
    for c in range(nchunk):
        if c + 1 < nchunk:
            proj_in(c + 1, (c + 1) % 3)
        gate(c, c % 3)
        if c >= 1:
            proj_out(c - 1, (c - 1) % 3)
    proj_out(nchunk - 1, (nchunk - 1) % 3)
    out = y_ref[...]
    if final:
        out = _rms(out, gf_ref[...])
    out_ref[...] = out


def _ffn_call(h, o_list, wo, gain, wa, wg, cw, wout, seq, final_gain=None):
    t_tokens, d = h.shape
    tm = min(TM, seq)
    ns = seq // tm
    nchunk = wa.shape[0]
    tile = pl.BlockSpec((tm, d), lambda i: (i, 0))
    in_specs = [tile] + [tile] * len(o_list) + [
        _const_spec(wo.shape), _const_spec((1, d)), _const_spec(wa.shape), _const_spec(wg.shape),
        _const_spec(cw.shape), _const_spec(wout.shape)]
    args = [h] + list(o_list) + [wo, gain.reshape(1, d), wa, wg, cw, wout]
    if final_gain is not None:
        in_specs.append(_const_spec((1, d)))
        args.append(final_gain.reshape(1, d))
    return pl.pallas_call(
        functools.partial(_ffn_body, n_o=len(o_list), final=final_gain is not None, tm=tm, ns=ns, nchunk=nchunk),
        grid=(t_tokens // tm,),
        in_specs=in_specs, out_specs=tile,
        out_shape=jax.ShapeDtypeStruct((t_tokens, d), F32),
        scratch_shapes=[pltpu.VMEM((tm, d), BF16), pltpu.VMEM((tm, d), F32),
                        pltpu.VMEM((nchunk, 16, FF_CHUNK), F32)]
        + [pltpu.VMEM((2, tm, FF_CHUNK), F32)] * 3 + [pltpu.VMEM((tm, FF_CHUNK), BF16)] * 3,
        compiler_params=_cparams(("arbitrary",)),
        name="attn_out_ffn",
    )(*args)


def _pad_heads(w, g, width):
    d = w.shape[0]
    w3 = w.reshape(d, g, HEAD_DIM)
    return jnp.pad(w3, ((0, 0), (0, 0), (0, width - HEAD_DIM))).reshape(d, g * width)


def _rot_heads(w, g):
    d = w.shape[0]
    w3 = w.reshape(d, g, HEAD_DIM)
    return jnp.concatenate([-w3[..., HALF:], w3[..., :HALF]], axis=-1).reshape(d, g * HEAD_DIM)


def _rope_k_weights(w, g):
    return jnp.concatenate([_pad_heads(w, g, LANES), _pad_heads(_rot_heads(w, g), g, LANES)], axis=1)


def _rope_tables(seq):
    inv = jnp.float32(ROPE_THETA) ** (-jnp.arange(HALF, dtype=F32) / HALF)

    def cs(pos):
        ang = pos.astype(F32)[:, None] * inv[None, :]
        return jnp.cos(ang), jnp.sin(ang)

    def nat(c):
        return jnp.concatenate([c, c, jnp.zeros((c.shape[0], LANES - HEAD_DIM), F32)], axis=1)

    cos, sin = cs(jnp.arange(seq))
    qscale = HEAD_DIM ** -0.5 * LOG2E
    cc, sc = cs(jnp.arange(seq // CMP_STRIDE) * CMP_STRIDE + CMP_BLOCK - 1)
    return {"cn": nat(cos), "sn": nat(sin), "ct": (cos * qscale).T, "st": (sin * qscale).T,
            "cc": nat(cc), "sc": nat(sc)}


def _ffn_weights(w_in, conv_w, conv_b, w_out):
    d, two_ff = w_in.shape
    dff = two_ff // 2
    nchunk = dff // FF_CHUNK

    def chunks(w):
        return w.reshape(d, nchunk, FF_CHUNK).transpose(1, 0, 2).astype(BF16)

    wa, wg = chunks(w_in[:, :dff]), chunks(w_in[:, dff:])
    taps = jnp.concatenate([conv_w, conv_b[None, :]], axis=0)
    cw = jnp.concatenate([taps[:, :dff].reshape(4, nchunk, FF_CHUNK), taps[:, dff:].reshape(4, nchunk, FF_CHUNK)],
                         axis=0).transpose(1, 0, 2)
    return wa, wg, cw, w_out.reshape(nchunk, FF_CHUNK, d).astype(BF16)


def _nsa_attention(h, gain, w_in, cmp_pos, cmp_w1, cmp_w2, tabs, emat, batch, seq):
    g, r = A_KV_HEADS, N_HEADS // A_KV_HEADS
    d = h.shape[1]
    kvw = g * HEAD_DIM
    nq = N_HEADS * HEAD_DIM
    wq, wkc, wvc, wks, wvs, wkw, wvw, wgl = jnp.split(
        w_in, [nq, nq + kvw, nq + 2 * kvw, nq + 3 * kvw, nq + 4 * kvw, nq + 5 * kvw, nq + 6 * kvw], axis=1)
    wn = jnp.concatenate([_rope_k_weights(wks, g), _rope_k_weights(wkw, g), wkc, wvc], axis=1).astype(BF16)
    wgl = wgl.reshape(d, N_HEADS, 3).transpose(0, 2, 1).reshape(d, 3 * N_HEADS)
    wt = jnp.concatenate([wq, wgl, _pad_heads(wvs, g, VROWS), _pad_heads(wvw, g, VROWS)], axis=1).T.astype(BF16)
    kpad = g * LANES
    nat_plan = [("rope", 0, kpad), ("rope", 2 * kpad, kpad), ("plain", 4 * kpad, kvw), ("plain", 4 * kpad + kvw, kvw)]
    ng = 3 * N_HEADS
    tr_plan = [("ropeq", 0, nq), ("sigmoid", nq, ng), ("vaug", nq + ng, g * VROWS),
               ("vaug", nq + ng + g * VROWS, g * VROWS)]
    ks, kw, kc, vc, qt, gates, vst, vwt = _proj_call(
        h, gain, tabs, wn, wt, nat_plan, tr_plan, [BF16, BF16, F32, F32], batch, seq)
    ks = ks.reshape(batch, seq, kpad)
    kw = kw.reshape(batch, seq, kpad)
    gates = gates.reshape(batch, 3, g, r, seq)

    nseg = seq // CMP_STRIDE
    nb = seq // SEL_BLOCK
    seg_f = CMP_STRIDE * HEAD_DIM

    def segs(x):
        return x.reshape(batch, nseg, CMP_STRIDE, g, HEAD_DIM).transpose(0, 3, 1, 2, 4).reshape(batch, g, nseg, seg_f)

    pos = cmp_pos.reshape(2, 2, 1, seg_f)
    w1 = cmp_w1.reshape(2, 2, seg_f, cmp_w1.shape[-1]).astype(BF16)
    w2k = jnp.stack([_pad_heads(cmp_w2[0], 1, LANES), _pad_heads(_rot_heads(cmp_w2[0], 1), 1, LANES)]).astype(BF16)
    w2v = _pad_heads(cmp_w2[1], 1, VROWS).T.astype(BF16)
    kcc, vcc = _compress_call(segs(kc), segs(vc), pos, w1, w2k, w2v, tabs["cc"], tabs["sc"])
    kcc = kcc.reshape(batch, g, nb, 4, LANES).transpose(0, 1, 3, 2, 4).reshape(batch, g, nseg, LANES)
    vcc = vcc.reshape(batch, g, VROWS, nb, 4).transpose(0, 1, 2, 4, 3).reshape(batch, g, VROWS, nseg)

    o_c, mb = _cmp_call(qt, kcc, vcc, gates, batch, seq)
    o_sw = _attn_call("sel", qt, ks, vst, batch, seq, g, gates=gates, branch=1, mb=mb, emat=emat,
                      kw=kw, vwt=vwt, window=WINDOW_A)
    t_tokens = batch * seq
    return [o.reshape(t_tokens, nq) for o in (o_c, o_sw)]


def kernel(x, norm_attn, norm_ffn, a_w_in, a_cmp_pos, a_cmp_w1, a_cmp_w2, a_w_out, kv_norm, b_w_kv, b_w_q, b_sinks,
           b_w_out, ffn_w_in, ffn_conv_w, ffn_conv_b, ffn_w_out, final_norm):
    batch, seq, d = x.shape
    depth = norm_attn.shape[0]
    n_a = a_w_in.shape[0]
    tabs = _rope_tables(seq)
    mr = min(MASK_ROWS, seq // SEL_BLOCK)
    emat = (jnp.arange(mr * SEL_BLOCK)[:, None] // SEL_BLOCK == jnp.arange(mr)[None, :]).astype(BF16)
    h = x.reshape(batch * seq, d)
    k_sh = v_sh = None
    gb, rb = B_KV_HEADS, N_HEADS // B_KV_HEADS
    for layer in range(depth):
        if layer < n_a:
            o_list = _nsa_attention(h, norm_attn[layer], a_w_in[layer], a_cmp_pos[layer], a_cmp_w1[layer],
                                    a_cmp_w2[layer], tabs, emat, batch, seq)
            wo = a_w_out[layer]
        else:
            j = layer - n_a
            if k_sh is None:
                wk, wv = jnp.split(b_w_kv, 2, axis=1)
                kpad = gb * LANES
                k_sh, v_sh = _proj_call(
                    h, kv_norm, tabs, _rope_k_weights(wk, gb).astype(BF16), _pad_heads(wv, gb, VROWS).T.astype(BF16),
                    [("rope", 0, kpad)], [("vaug", 0, gb * VROWS)], [BF16], batch, seq)
                k_sh = k_sh.reshape(batch, seq, kpad)
            (qt,) = _proj_call(h, norm_attn[layer], tabs, None, b_w_q[j].T.astype(BF16), [],
                               [("ropeq", 0, N_HEADS * HEAD_DIM)], [], batch, seq)
            sinks = jnp.broadcast_to((b_sinks[j] * LOG2E).reshape(gb, 1, rb, 1),
                                     (gb, 1, rb, TQ_WINDOW)).reshape(gb, 1, rb * TQ_WINDOW)
            o = _attn_call("swa", qt, k_sh, v_sh, batch, seq, gb, sinks=sinks, window=WINDOW_B)
            o_list = [o.reshape(batch * seq, N_HEADS * HEAD_DIM)]
            wo = b_w_out[j]
        wa, wg, cw, wout = _ffn_weights(ffn_w_in[layer], ffn_conv_w[layer], ffn_conv_b[layer], ffn_w_out[layer])
        h = _ffn_call(h, o_list, wo.astype(BF16), norm_ffn[layer], wa, wg, cw, wout, seq,
                      final_gain=final_norm if layer == depth - 1 else None)
    return h.reshape(batch, seq, d)
```

```python
import functools
import math

import jax
import jax.numpy as jnp
from jax import lax
from jax.experimental import pallas as pl
from jax.experimental.pallas import tpu as pltpu

F32 = jnp.float32
BF16 = jnp.bfloat16

HEAD_DIM = 64
HALF = HEAD_DIM // 2
N_HEADS = 16
A_KV_HEADS = 4
B_KV_HEADS = 2
CMP_BLOCK = 32
CMP_STRIDE = 16
SEL_BLOCK = 64
N_SELECT = 16
WINDOW_A = 512
WINDOW_B = 128
CONV_WIDTH = 3
ROPE_THETA = 10000.0
EPS = 1e-6
FORCE = 1e6

LANES = 128
BF16_SUBLANES = 16
VMEM_LIMIT = 56 * 1024 * 1024
TQ = 256
TQ_WINDOW = 256
TK = 256
SEL_GROUP = 4
CMP_BUCKET_ALIGN = LANES
TM = 512
FF_CHUNK = 256
VROWS = HEAD_DIM + BF16_SUBLANES
MASK_ROWS = 128

LOG2E = 1.4426950408889634
NEG = -1e30
M_INIT = -1e29
REMOVED = -3e38

_NT = (((1,), (1,)), ((), ()))


def _cparams(sem):
    return pltpu.CompilerParams(dimension_semantics=sem, vmem_limit_bytes=VMEM_LIMIT)


def _const_spec(shape):
    n = len(shape)
    return pl.BlockSpec(shape, lambda *_: (0,) * n, pipeline_mode=pl.Buffered(1))


def _rms(x, g):
    ms = jnp.mean(x * x, axis=-1, keepdims=True)
    return x * lax.rsqrt(ms + EPS) * g


def _proj_body(*refs, nat_plan, tr_plan, tm):
    it = iter(refs)
    h_ref, g_ref = next(it), next(it)
    if nat_plan:
        cn_ref, sn_ref, wn_ref = next(it), next(it), next(it)
    ct_ref, st_ref, wt_ref = next(it), next(it), next(it)
    outs = list(it)

    hn = _rms(h_ref[...], g_ref[...]).astype(BF16)
    oi = 0
    for kind, c0, n in nat_plan:
        o_ref = outs[oi]
        oi += 1
        y = jnp.dot(hn, wn_ref[:, c0:c0 + n], preferred_element_type=F32)
        if kind == "rope":
            yr = jnp.dot(hn, wn_ref[:, c0 + n:c0 + 2 * n], preferred_element_type=F32)
            c, s = cn_ref[...], sn_ref[...]
            for g in range(n // LANES):
                sl = slice(LANES * g, LANES * (g + 1))
                o_ref[:, sl] = (y[:, sl] * c + yr[:, sl] * s).astype(o_ref.dtype)
        elif kind == "heads":
            for g in range(n // HEAD_DIM):
                o_ref[g] = y[:, HEAD_DIM * g:HEAD_DIM * (g + 1)].astype(o_ref.dtype)
        else:
            o_ref[...] = y.astype(o_ref.dtype)
    for kind, r0, n in tr_plan:
        o_ref = outs[oi]
        oi += 1
        y = lax.dot_general(wt_ref[r0:r0 + n, :], hn, _NT, preferred_element_type=F32)
        if kind == "ropeq":
            c, s = ct_ref[...], st_ref[...]
            for hd in range(n // HEAD_DIM):
                a = HEAD_DIM * hd
                y1, y2 = y[a:a + HALF], y[a + HALF:a + HEAD_DIM]
                o_ref[a:a + HALF, :] = (y1 * c - y2 * s).astype(o_ref.dtype)
                o_ref[a + HALF:a + HEAD_DIM, :] = (y2 * c + y1 * s).astype(o_ref.dtype)
        elif kind == "sigmoid":
            o_ref[...] = jax.nn.sigmoid(y)
        else:
            rows = lax.broadcasted_iota(jnp.int32, (VROWS, tm), 0)
            for g in range(n // VROWS):
                yg = jnp.where(rows == HEAD_DIM, 1.0, y[VROWS * g:VROWS * (g + 1)]).astype(o_ref.dtype)
                for t in range(tm // TK):
                    o_ref[t, VROWS * g:VROWS * (g + 1), :] = yg[:, t * TK:(t + 1) * TK]


def _proj_call(h, gain, tabs, wn, wt, nat_plan, tr_plan, nat_dtypes, batch, seq):
    t_tokens, d = h.shape
    tm = min(TM, seq)
    ns = seq // tm
    in_specs = [pl.BlockSpec((tm, d), lambda i: (i, 0)), _const_spec((1, d))]
    args = [h, gain.reshape(1, d)]
    if nat_plan:
        in_specs += [pl.BlockSpec((tm, LANES), lambda i: (i % ns, 0)),
                     pl.BlockSpec((tm, LANES), lambda i: (i % ns, 0)),
                     _const_spec(wn.shape)]
        args += [tabs["cn"], tabs["sn"], wn]
    in_specs += [pl.BlockSpec((HALF, tm), lambda i: (0, i % ns)),
                 pl.BlockSpec((HALF, tm), lambda i: (0, i % ns)),
                 _const_spec(wt.shape)]
    args += [tabs["ct"], tabs["st"], wt]
    out_shape, out_specs = [], []
    for (kind, _, n), dt in zip(nat_plan, nat_dtypes):
        if kind == "heads":
            nh = n // HEAD_DIM
            out_shape.append(jax.ShapeDtypeStruct((batch, nh, seq, HEAD_DIM), dt))
            out_specs.append(pl.BlockSpec((None, nh, tm, HEAD_DIM), lambda i: (i // ns, 0, i % ns, 0)))
        else:
            out_shape.append(jax.ShapeDtypeStruct((t_tokens, n), dt))
            out_specs.append(pl.BlockSpec((tm, n), lambda i: (i, 0)))
    for kind, _, n in tr_plan:
        if kind == "vaug":
            out_shape.append(jax.ShapeDtypeStruct((batch, seq // TK, n, TK), BF16))
            out_specs.append(pl.BlockSpec((None, tm // TK, n, TK), lambda i: (i // ns, i % ns, 0, 0)))
        else:
            dt = F32 if kind == "sigmoid" else BF16
            out_shape.append(jax.ShapeDtypeStruct((batch, n, seq), dt))
            out_specs.append(pl.BlockSpec((None, n, tm), lambda i: (i // ns, 0, i % ns)))
    return pl.pallas_call(
        functools.partial(_proj_body, nat_plan=tuple(nat_plan), tr_plan=tuple(tr_plan), tm=tm),
        grid=(t_tokens // tm,),
        in_specs=in_specs, out_specs=out_specs, out_shape=out_shape,
        compiler_params=_cparams(("parallel",)),
        name="norm_proj",
    )(*args)


def _gelu_tanh(x):
    c = math.sqrt(2.0 / math.pi)
    return x * (0.5 * (1.0 + jnp.tanh(c * (x + 0.044715 * (x * x * x)))))


def _compress_body(sk_ref, sv_ref, pos_ref, w1_ref, w2k_ref, w2v_ref, cc_ref, sc_ref, ok_ref, ov_ref, *, ncp):
    def hidden(seg, kv):
        xa = (seg + pos_ref[kv, 0]).astype(BF16)
        xb = (seg + pos_ref[kv, 1]).astype(BF16)
        a = jnp.dot(xa, w1_ref[kv, 0], preferred_element_type=F32)
        b = jnp.dot(xb, w1_ref[kv, 1], preferred_element_type=F32)
        return _gelu_tanh(a + pltpu.roll(b, ncp - 1, axis=0)).astype(BF16)

    gk = hidden(sk_ref[...], 0)
    k = jnp.dot(gk, w2k_ref[0], preferred_element_type=F32)
    kr = jnp.dot(gk, w2k_ref[1], preferred_element_type=F32)
    ok_ref[...] = (k * cc_ref[...] + kr * sc_ref[...]).astype(ok_ref.dtype)
    gv = hidden(sv_ref[...], 1)
    vt = lax.dot_general(w2v_ref[...], gv, _NT, preferred_element_type=F32)
    rows = lax.broadcasted_iota(jnp.int32, (VROWS, ncp), 0)
    ov_ref[...] = jnp.where(rows == HEAD_DIM, 1.0, vt).astype(ov_ref.dtype)


def _compress_call(segk, segv, pos, w1, w2k, w2v, cc, sc):
    b, g, ncp, f = segk.shape
    seg_spec = pl.BlockSpec((None, None, ncp, f), lambda i, j: (i, j, 0, 0))
    return pl.pallas_call(
        functools.partial(_compress_body, ncp=ncp),
        grid=(b, g),
        in_specs=[seg_spec, seg_spec, _const_spec(pos.shape), _const_spec(w1.shape), _const_spec(w2k.shape),
                  _const_spec(w2v.shape), _const_spec(cc.shape), _const_spec(sc.shape)],
        out_specs=[pl.BlockSpec((None, None, ncp, LANES), lambda i, j: (i, j, 0, 0)),
                   pl.BlockSpec((None, None, VROWS, ncp), lambda i, j: (i, j, 0, 0))],
        out_shape=[jax.ShapeDtypeStruct((b, g, ncp, LANES), BF16),
                   jax.ShapeDtypeStruct((b, g, VROWS, ncp), BF16)],
        compiler_params=_cparams(("parallel", "parallel")),
        name="compress",
    )(segk, segv, pos, w1, w2k, w2v, cc, sc)


def _load_queries(q_ref, qa_ref, r, tq):
    for rr in range(r):
        qa_ref[0:HEAD_DIM, rr * tq:(rr + 1) * tq] = q_ref[HEAD_DIM * rr:HEAD_DIM * (rr + 1), :]
    qa_ref[HEAD_DIM:LANES, :] = jnp.zeros((LANES - HEAD_DIM, r * tq), BF16)


def _gated_heads(acc, gate_ref, r, tq, colscale=None):
    inv = 1.0 / acc[HEAD_DIM:HEAD_DIM + 1, :]
    if colscale is not None:
        inv = inv * colscale
    o = acc[0:HEAD_DIM, :] * inv
    parts = []
    for rr in range(r):
        z = o[:, rr * tq:(rr + 1) * tq]
        if gate_ref is not None:
            z = z * gate_ref[rr:rr + 1, :]
        parts.append(z)
    return jnp.concatenate(parts, axis=0)


def _store_heads(o_ref, acc, gate_ref, r, tq, colscale=None):
    o_ref[...] = _gated_heads(acc, gate_ref, r, tq, colscale).T


def _window_tiles(window, tq):
    nt = max((s0 + tq - 1) // TK - (s0 - window + 1) // TK + 1 for s0 in range(8 * window, 8 * window + TK, tq))
    return nt, tuple((i == nt - 1, (nt - i) * TK - 1 >= window) for i in range(nt))


def _window_attend(qa_of, k_ref, v_ref, j0, u, checks, *, window, start, tq, r, m0_of=None, acc0=None,
                   between=None):
    row0 = pl.multiple_of(j0 * TK, TK)
    kt = k_ref[pl.ds(row0, u * TK), :]
    vt = jnp.concatenate([v_ref[j0 + i] for i in range(u)], axis=1)
    d0 = (start - row0) + (lax.broadcasted_iota(jnp.int32, (TK, tq), 1)
                           - lax.broadcasted_iota(jnp.int32, (TK, tq), 0))
    oks = []
    for i in range(u):
        lower, upper = checks[i]
        if lower and upper:
            oks.append((d0 >= i * TK) & (d0 < window + i * TK))
        elif lower:
            oks.append(d0 >= i * TK)
        elif upper:
            oks.append(d0 < window + i * TK)
        else:
            oks.append(None)
    cw = max(tq, 2 * LANES)
    chains = [slice(c0, c0 + cw) for c0 in range(0, r * tq, cw)]
    scores = [jnp.dot(kt, qa_of(ch), preferred_element_type=F32) for ch in chains]
    if between is not None:
        between()

    def masked(si, ok):
        if ok is None:
            return si
        return jnp.concatenate([jnp.where(ok, si[:, c0:c0 + tq], NEG) for c0 in range(0, cw, tq)], axis=1)

    outs = []
    for ch, s in zip(chains, scores):
        s = jnp.concatenate([masked(s[i * TK:(i + 1) * TK], oks[i]) for i in range(u)], axis=0)
        m = jnp.max(s, axis=0, keepdims=True)
        if m0_of is not None:
            m = jnp.maximum(m, m0_of(ch))
        acc = jnp.dot(vt, jnp.exp2(s - m).astype(BF16), preferred_element_type=F32)
        if acc0 is not None:
            acc = acc + acc0 * jnp.exp2(m0_of(ch) - m)
        outs.append(acc)
    return jnp.concatenate(outs, axis=1)


def _cmp_body(q_ref, k_ref, v_ref, gate_ref, o_ref, mb_ref, qa_ref, *, tq, nb, r):
    qi = pl.program_id(2)
    _load_queries(q_ref, qa_ref, r, tq)
    buckets = [b for b in (nb // 4, nb // 2, 3 * nb // 4) if b >= N_SELECT and b % CMP_BUCKET_ALIGN == 0] + [nb]
    args = (q_ref, k_ref, v_ref, gate_ref, o_ref, mb_ref, qa_ref, qi)

    @pl.when(qi == 0)
    def _():
        _cmp_rows(*args, tq=tq, nb=nb, r=r, rows=buckets[0], forced_distinct=False)

    lo = 1
    for rows in buckets:
        hi = rows * SEL_BLOCK // tq if rows < nb else pl.num_programs(2)

        @pl.when((qi >= lo) & (qi < hi))
        def _(rows=rows):
            _cmp_rows(*args, tq=tq, nb=nb, r=r, rows=rows, forced_distinct=True)
        lo = hi


def _cmp_rows(q_ref, k_ref, v_ref, gate_ref, o_ref, mb_ref, qa_ref, qi, *, tq, nb, r, rows, forced_distinct):
    lw = r * tq
    kt = jnp.concatenate([k_ref[m * nb:m * nb + rows, :] for m in range(4)], axis=0)
    vt = jnp.concatenate([v_ref[:, m * nb:m * nb + rows] for m in range(4)], axis=1)
    s = jnp.dot(kt, qa_ref[...], preferred_element_type=F32)
    if rows < nb:
        mb_ref[rows:nb, :] = jnp.full((nb - rows, tq), NEG, mb_ref.dtype)
    nb = rows
    j_io = lax.broadcasted_iota(jnp.int32, (nb, tq), 0)
    tt = qi * tq + lax.broadcasted_iota(jnp.int32, (nb, tq), 1)
    lim = tt - (CMP_BLOCK - 1)
    sm = []
    for m in range(4):
        valid = SEL_BLOCK * j_io + CMP_STRIDE * m <= lim
        sm.append(jnp.concatenate(
            [jnp.where(valid, s[m * nb:(m + 1) * nb, rr * tq:(rr + 1) * tq], NEG) for rr in range(r)], axis=1))
    mx = jnp.max(jnp.maximum(jnp.maximum(sm[0], sm[1]), jnp.maximum(sm[2], sm[3])), axis=0, keepdims=True)
    pm = [jnp.exp2(sm[m] - mx) for m in range(4)]
    den = jnp.sum(pm[0] + pm[1] + pm[2] + pm[3], axis=0, keepdims=True)
    t_lane = qi * tq + (lax.broadcasted_iota(jnp.int32, (1, lw), 1) & (tq - 1))
    has_key = jnp.where(t_lane >= CMP_BLOCK - 1, 1.0, 0.0)
    acc = jnp.dot(vt, jnp.concatenate(pm, axis=0).astype(BF16), preferred_element_type=F32)
    _store_heads(o_ref, acc, gate_ref, r, tq, has_key)

    inv = has_key / den
    ps = []
    for m in range(4):
        pn = pm[m] * inv
        acc_h = pn[:, 0:tq]
        for rr in range(1, r):
            acc_h = acc_h + pn[:, rr * tq:(rr + 1) * tq]
        ps.append(acc_h)
    prev3 = jnp.where(j_io == 0, 0.0, pltpu.roll(ps[3], 1, axis=0))
    imp = prev3 + 2.0 * (ps[0] + ps[1] + ps[2]) + ps[3]
    cur = tt >> (SEL_BLOCK.bit_length() - 1)
    forced = (j_io == 0) | (j_io == cur) | (j_io == cur - 1)
    causal = j_io * SEL_BLOCK <= tt
    v = jnp.where(forced, REMOVED if forced_distinct else FORCE, jnp.where(causal, imp, -FORCE))
    jf = j_io.astype(F32)
    for _ in range(min(N_SELECT, nb) - (3 if forced_distinct else 0)):
        top = jnp.max(v, axis=0, keepdims=True)
        idx = jnp.min(jnp.where(v == top, jf, float(nb)), axis=0, keepdims=True)
        v = jnp.where(jf == idx, REMOVED, v)
    mb_ref[0:nb, :] = jnp.where(v == REMOVED, 0.0, NEG).astype(mb_ref.dtype)


def _cmp_call(qt, kc, vct, gates, batch, seq):
    g, r = A_KV_HEADS, N_HEADS // A_KV_HEADS
    nb = seq // SEL_BLOCK
    ncp = 4 * nb
    tq = TQ_WINDOW
    return pl.pallas_call(
        functools.partial(_cmp_body, tq=tq, nb=nb, r=r),
        grid=(batch, g, seq // tq),
        in_specs=[pl.BlockSpec((None, r * HEAD_DIM, tq), lambda b, gg, q: (b, gg, q)),
                  pl.BlockSpec((None, None, ncp, LANES), lambda b, gg, q: (b, gg, 0, 0)),
                  pl.BlockSpec((None, None, VROWS, ncp), lambda b, gg, q: (b, gg, 0, 0)),
                  pl.BlockSpec((None, None, None, r, tq), lambda b, gg, q: (b, 0, gg, 0, q))],
        out_specs=[pl.BlockSpec((None, tq, r * HEAD_DIM), lambda b, gg, q: (b, q, gg)),
                   pl.BlockSpec((None, None, nb, tq), lambda b, gg, q: (b, gg, 0, q))],
        out_shape=[jax.ShapeDtypeStruct((batch, seq, N_HEADS * HEAD_DIM), F32),
                   jax.ShapeDtypeStruct((batch, g, nb, seq), BF16)],
        scratch_shapes=[pltpu.VMEM((LANES, r * tq), BF16)],
        compiler_params=_cparams(("parallel", "parallel", "parallel")),
        name="cmp_attn_topk",
    )(qt, kc, vct, gates)


def _sel_body(q_ref, k_ref, v_ref, gate_ref, mb_ref, e_ref, kw_ref, vw_ref, gatew_ref, o_ref,
              qa_ref, acc_ref, accw_ref, m_ref, macc_ref,
              s0_ref, s1_ref, p0_ref, p1_ref, mt0_ref, mt1_ref, cm0_ref, cm1_ref, *, tq, r, mr, nb, window):
    s_refs, p_refs, mt_refs, cm_refs = (s0_ref, s1_ref), (p0_ref, p1_ref), (mt0_ref, mt1_ref), (cm0_ref, cm1_ref)
    qi = pl.program_id(2)
    lw = r * tq
    start = qi * tq
    u = p0_ref.shape[0]
    gph = mr * SEL_BLOCK // (u * TK)
    gd = (start // TK) // u
    nh = nb // mr

    _load_queries(q_ref, qa_ref.at[0], r, tq)
    for hf in range(1, nh):
        qa_ref[hf, 0:LANES, :] = qa_ref[0, 0:LANES, :]
    for hf in range(nh):
        for rr in range(r):
            qa_ref[hf, LANES:LANES + mr, rr * tq:(rr + 1) * tq] = mb_ref[hf * mr:(hf + 1) * mr, :]
    acc_ref[...] = jnp.zeros((VROWS, lw), F32)

    def key_tile(gi, i):
        row0 = pl.multiple_of(gi * (u * TK), u * TK)
        eoff = pl.multiple_of(row0 % (mr * SEL_BLOCK), u * TK)
        return jnp.concatenate([k_ref[pl.ds(row0 + i * TK, TK), :], e_ref[pl.ds(eoff + i * TK, TK), :]], axis=1)

    def causal(gi, i):
        t_io = start + (lax.broadcasted_iota(jnp.int32, (TK, lw), 1) & (tq - 1))
        return (gi * u + i) * TK + lax.broadcasted_iota(jnp.int32, (TK, lw), 0) <= t_io

    def pipeline(stage_a, stage_b, stage_c):
        def trip(gi, slot):
            stage_a(gi + 1, 1 - slot)
            stage_b(gi, slot, False)
            stage_c(gi - 1, 1 - slot)

        def last(slot):
            stage_b(gd, slot, True)
            stage_c(gd - 1, 1 - slot)
            stage_c(gd, slot)

        nt, _ = _window_tiles(window, tq)
        accw_ref[...] = _window_attend(lambda ch: qa_ref[0, 0:LANES, ch], kw_ref, vw_ref,
                                       jnp.maximum(start // TK - (nt - 1), 0), nt, ((True, True),) * nt,
                                       window=window, start=start, tq=tq, r=r, between=lambda: stage_a(0, 0))

        @pl.when(gd == 0)
        def _():
            stage_b(0, 0, True)
            stage_c(0, 0)

        @pl.when(gd > 0)
        def _():
            stage_a(1, 1)
            stage_b(0, 0, False)
            n = gd - 1

            def body(pi, c):
                trip(1 + 2 * pi, 1)
                trip(2 + 2 * pi, 0)
                return c
            lax.fori_loop(0, n // 2, body, 0)

            @pl.when(n % 2 == 1)
            def _():
                trip(gd - 1, 1)
                last(0)

            @pl.when(n % 2 == 0)
            def _():
                last(1)

    m_ref[...] = jnp.full((1, lw), M_INIT, F32)
    macc_ref[...] = jnp.full((1, lw), M_INIT, F32)

    def scores(gi, slot):
        qa = qa_ref[gi // gph]
        for i in range(u):
            s = jnp.dot(key_tile(gi, i), qa, preferred_element_type=F32)
            s_refs[slot][i] = s
            cm_refs[slot][i] = jnp.max(s, axis=0, keepdims=True)

    def softmax(gi, slot, masked):
        m = m_ref[...]
        for i in range(u):
            if masked:
                s = jnp.where(causal(gi, i), s_refs[slot][i], NEG)
                m = jnp.maximum(m, jnp.max(s, axis=0, keepdims=True))
            else:
                m = jnp.maximum(m, cm_refs[slot][i])
                s = s_refs[slot][i]
            p_refs[slot][i] = jnp.exp2(s - m).astype(BF16)
            mt_refs[slot][i] = m
        m_ref[...] = m

    def values(gi, slot):
        acc, ma = acc_ref[...], macc_ref[...]
        for i in range(u):
            mt = mt_refs[slot][i]
            acc = acc * jnp.exp2(ma - mt) + jnp.dot(v_ref[gi * u + i], p_refs[slot][i],
                                                    preferred_element_type=F32)
            ma = mt
        acc_ref[...], macc_ref[...] = acc, ma

    pipeline(scores, softmax, values)
    o_ref[...] = (_gated_heads(acc_ref[...], gate_ref, r, tq) + _gated_heads(accw_ref[...], gatew_ref, r, tq)).T


def _attn_body(*refs, mode, tq, r, window):
    if mode == "win":
        q_ref, k_ref, v_ref, gate_ref, o_ref, qa_ref, acc_ref = refs
        m0_of = acc0 = None
    else:
        q_ref, k_ref, v_ref, sink_ref, o_ref, qa_ref, acc_ref = refs
        gate_ref = None
        m0_of = lambda ch: sink_ref[:, ch]
        acc0 = jnp.where(lax.broadcasted_iota(jnp.int32, (VROWS, 1), 0) == HEAD_DIM, 1.0, 0.0)
    qi = pl.program_id(2)
    start = qi * tq
    jd = start // TK
    _load_queries(q_ref, qa_ref, r, tq)

    def step(j0, checks):
        acc_ref[...] = _window_attend(lambda ch: qa_ref[:, ch], k_ref, v_ref, j0, nt, checks, window=window,
                                      start=start, tq=tq, r=r, m0_of=m0_of, acc0=acc0)

    nt, roles = _window_tiles(window, tq)

    @pl.when(jd >= nt - 1)
    def _():
        step(jd - (nt - 1), roles)

    @pl.when(jd < nt - 1)
    def _():
        step(0, ((True, True),) * nt)

    _store_heads(o_ref, acc_ref[...], gate_ref, r, tq)


def _attn_call(mode, qt, k, vt, batch, seq, g, gates=None, branch=None, mb=None, emat=None, sinks=None,
               window=None, kw=None, vwt=None):
    r = N_HEADS // g
    nb = seq // SEL_BLOCK
    mr = emat.shape[1] if mode == "sel" else 0
    nkt = seq // TK
    kc = LANES + mr
    tq = TQ if mode == "sel" else TQ_WINDOW
    k_spec = pl.BlockSpec((None, seq, LANES), lambda b, gg, q: (b, 0, gg))
    v_spec = pl.BlockSpec((None, nkt, VROWS, TK), lambda b, gg, q: (b, 0, gg, 0))
    in_specs = [pl.BlockSpec((None, r * HEAD_DIM, tq), lambda b, gg, q: (b, gg, q)), k_spec, v_spec]
    args = [qt, k, vt]
    if mode in ("sel", "win"):
        in_specs.append(pl.BlockSpec((None, None, None, r, tq), lambda b, gg, q: (b, branch, gg, 0, q)))
        args.append(gates)
    if mode == "sel":
        in_specs += [pl.BlockSpec((None, None, nb, tq), lambda b, gg, q: (b, gg, 0, q)), _const_spec(emat.shape),
                     k_spec, v_spec,
                     pl.BlockSpec((None, None, None, r, tq), lambda b, gg, q: (b, branch + 1, gg, 0, q))]
        args += [mb, emat, kw, vwt, gates]
    if mode == "swa":
        in_specs.append(pl.BlockSpec((None, 1, r * tq), lambda b, gg, q: (gg, 0, 0)))
        args.append(sinks)
    lw = r * tq
    scratch = [pltpu.VMEM((kc, lw), BF16), pltpu.VMEM((VROWS, lw), F32)]
    if mode == "sel":
        u = min(SEL_GROUP, mr * SEL_BLOCK // TK)
        scratch[0] = pltpu.VMEM((nb // mr, kc, lw), BF16)
        scratch += ([pltpu.VMEM((VROWS, lw), F32)] + [pltpu.VMEM((1, lw), F32)] * 2
                    + [pltpu.VMEM((u, TK, lw), F32)] * 2
                    + [pltpu.VMEM((u, TK, lw), BF16)] * 2 + [pltpu.VMEM((u, 1, lw), F32)] * 4)
        body = functools.partial(_sel_body, tq=tq, r=r, mr=mr, nb=nb, window=window)
    else:
        body = functools.partial(_attn_body, mode=mode, tq=tq, r=r, window=window)
    return pl.pallas_call(
        body,
        grid=(batch, g, seq // tq),
        in_specs=in_specs,
        out_specs=pl.BlockSpec((None, tq, r * HEAD_DIM), lambda b, gg, q: (b, q, gg)),
        out_shape=jax.ShapeDtypeStruct((batch, seq, N_HEADS * HEAD_DIM), F32),
        scratch_shapes=scratch,
        compiler_params=_cparams(("parallel", "parallel", "arbitrary")),
        name=mode + "_attn",
    )(*args)


def _ffn_body(*refs, n_o, final, tm, ns, nchunk):
    it = iter(refs)
    h_ref = next(it)
    o_refs = [next(it) for _ in range(n_o)]
    wo_ref, g_ref, wa_ref, wg_ref, cw_ref, wout_ref = (next(it) for _ in range(6))
    gf_ref = next(it) if final else None
    out_ref, hn_ref, y_ref, prev_ref = (next(it) for _ in range(4))
    u_refs = [next(it) for _ in range(3)]
    act_refs = [next(it) for _ in range(3)]

    osum = o_refs[0][...]
    for o_ref in o_refs[1:]:
        osum = osum + o_ref[...]
    h = h_ref[...] + jnp.dot(osum.astype(BF16), wo_ref[...], preferred_element_type=F32)
    hn_ref[...] = _rms(h, g_ref[...]).astype(BF16)
    y_ref[...] = h

    @pl.when(pl.program_id(0) % ns == 0)
    def _():
        prev_ref[...] = jnp.zeros_like(prev_ref)

    rid = lax.broadcasted_iota(jnp.int32, (8, FF_CHUNK), 0)

    def conv(u, p8, w):
        u1 = pltpu.roll(u, 1, axis=0)
        u2 = pltpu.roll(u, 2, axis=0)
        f1 = jnp.where(rid < 1, pltpu.roll(p8, 1, axis=0), u1[0:8])
        f2 = jnp.where(rid < 2, pltpu.roll(p8, 2, axis=0), u2[0:8])
        u1 = jnp.concatenate([f1, u1[8:]], axis=0)
        u2 = jnp.concatenate([f2, u2[8:]], axis=0)
        return w[3:4] + w[0:1] * u2 + w[1:2] * u1 + w[2:3] * u

    def proj_in(c, slot):
        hn = hn_ref[...]
        u_refs[slot][0] = jnp.dot(hn, wa_ref[c], preferred_element_type=F32)
        u_refs[slot][1] = jnp.dot(hn, wg_ref[c], preferred_element_type=F32)

    def gate(c, slot):
        ua, ug = u_refs[slot][0], u_refs[slot][1]
        cw = cw_ref[c]
        pa, pg = prev_ref[c, 0:8], prev_ref[c, 8:16]
        prev_ref[c, 0:8] = ua[tm - 8:tm]
        prev_ref[c, 8:16] = ug[tm - 8:tm]
        ca = conv(ua, pa, cw[0:4])
        cg = conv(ug, pg, cw[4:8])
        act_refs[slot][...] = (ca * jax.nn.sigmoid(ca) * cg).astype(BF16)

    def proj_out(c, slot):
        y_ref[...] += jnp.dot(act_refs[slot][...], wout_ref[c], preferred_element_type=F32)

    proj_in(0, 0)
    for c in range(nchunk):
        if c + 1 < nchunk:
            proj_in(c + 1, (c + 1) % 3)
        gate(c, c % 3)
        if c >= 1:
            proj_out(c - 1, (c - 1) % 3)
    proj_out(nchunk - 1, (nchunk - 1) % 3)
    out = y_ref[...]
    if final:
        out = _rms(out, gf_ref[...])
    out_ref[...] = out


def _ffn_call(h, o_list, wo, gain, wa, wg, cw, wout, seq, final_gain=None):
    t_tokens, d = h.shape
    tm = min(TM, seq)
    ns = seq // tm
    nchunk = wa.shape[0]
    tile = pl.BlockSpec((tm, d), lambda i: (i, 0))
    in_specs = [tile] + [tile] * len(o_list) + [
        _const_spec(wo.shape), _const_spec((1, d)), _const_spec(wa.shape), _const_spec(wg.shape),
        _const_spec(cw.shape), _const_spec(wout.shape)]
    args = [h] + list(o_list) + [wo, gain.reshape(1, d), wa, wg, cw, wout]
    if final_gain is not None:
        in_specs.append(_const_spec((1, d)))
        args.append(final_gain.reshape(1, d))
    return pl.pallas_call(
        functools.partial(_ffn_body, n_o=len(o_list), final=final_gain is not None, tm=tm, ns=ns, nchunk=nchunk),
        grid=(t_tokens // tm,),
        in_specs=in_specs, out_specs=tile,
        out_shape=jax.ShapeDtypeStruct((t_tokens, d), F32),
        scratch_shapes=[pltpu.VMEM((tm, d), BF16), pltpu.VMEM((tm, d), F32),
                        pltpu.VMEM((nchunk, 16, FF_CHUNK), F32)]
        + [pltpu.VMEM((2, tm, FF_CHUNK), F32)] * 3 + [pltpu.VMEM((tm, FF_CHUNK), BF16)] * 3,
        compiler_params=_cparams(("arbitrary",)),
        name="attn_out_ffn",
    )(*args)


def _pad_heads(w, g, width):
    d = w.shape[0]
    w3 = w.reshape(d, g, HEAD_DIM)
    return jnp.pad(w3, ((0, 0), (0, 0), (0, width - HEAD_DIM))).reshape(d, g * width)


def _rot_heads(w, g):
    d = w.shape[0]
    w3 = w.reshape(d, g, HEAD_DIM)
    return jnp.concatenate([-w3[..., HALF:], w3[..., :HALF]], axis=-1).reshape(d, g * HEAD_DIM)


def _rope_k_weights(w, g):
    return jnp.concatenate([_pad_heads(w, g, LANES), _pad_heads(_rot_heads(w, g), g, LANES)], axis=1)


def _rope_tables(seq):
    inv = jnp.float32(ROPE_THETA) ** (-jnp.arange(HALF, dtype=F32) / HALF)

    def cs(pos):
        ang = pos.astype(F32)[:, None] * inv[None, :]
        return jnp.cos(ang), jnp.sin(ang)

    def nat(c):
        return jnp.concatenate([c, c, jnp.zeros((c.shape[0], LANES - HEAD_DIM), F32)], axis=1)

    cos, sin = cs(jnp.arange(seq))
    qscale = HEAD_DIM ** -0.5 * LOG2E
    cc, sc = cs(jnp.arange(seq // CMP_STRIDE) * CMP_STRIDE + CMP_BLOCK - 1)
    return {"cn": nat(cos), "sn": nat(sin), "ct": (cos * qscale).T, "st": (sin * qscale).T,
            "cc": nat(cc), "sc": nat(sc)}


def _ffn_weights(w_in, conv_w, conv_b, w_out):
    d, two_ff = w_in.shape
    dff = two_ff // 2
    nchunk = dff // FF_CHUNK

    def chunks(w):
        return w.reshape(d, nchunk, FF_CHUNK).transpose(1, 0, 2).astype(BF16)

    wa, wg = chunks(w_in[:, :dff]), chunks(w_in[:, dff:])
    taps = jnp.concatenate([conv_w, conv_b[None, :]], axis=0)
    cw = jnp.concatenate([taps[:, :dff].reshape(4, nchunk, FF_CHUNK), taps[:, dff:].reshape(4, nchunk, FF_CHUNK)],
                         axis=0).transpose(1, 0, 2)
    return wa, wg, cw, w_out.reshape(nchunk, FF_CHUNK, d).astype(BF16)


def _nsa_attention(h, gain, w_in, cmp_pos, cmp_w1, cmp_w2, tabs, emat, batch, seq):
    g, r = A_KV_HEADS, N_HEADS // A_KV_HEADS
    d = h.shape[1]
    kvw = g * HEAD_DIM
    nq = N_HEADS * HEAD_DIM
    wq, wkc, wvc, wks, wvs, wkw, wvw, wgl = jnp.split(
        w_in, [nq, nq + kvw, nq + 2 * kvw, nq + 3 * kvw, nq + 4 * kvw, nq + 5 * kvw, nq + 6 * kvw], axis=1)
    wn = jnp.concatenate([_rope_k_weights(wks, g), _rope_k_weights(wkw, g), wkc, wvc], axis=1).astype(BF16)
    wgl = wgl.reshape(d, N_HEADS, 3).transpose(0, 2, 1).reshape(d, 3 * N_HEADS)
    wt = jnp.concatenate([wq, wgl, _pad_heads(wvs, g, VROWS), _pad_heads(wvw, g, VROWS)], axis=1).T.astype(BF16)
    kpad = g * LANES
    nat_plan = [("rope", 0, kpad), ("rope", 2 * kpad, kpad), ("heads", 4 * kpad, kvw), ("heads", 4 * kpad + kvw, kvw)]
    ng = 3 * N_HEADS
    tr_plan = [("ropeq", 0, nq), ("sigmoid", nq, ng), ("vaug", nq + ng, g * VROWS),
               ("vaug", nq + ng + g * VROWS, g * VROWS)]
    ks, kw, kc, vc, qt, gates, vst, vwt = _proj_call(
        h, gain, tabs, wn, wt, nat_plan, tr_plan, [BF16, BF16, F32, F32], batch, seq)
    ks = ks.reshape(batch, seq, kpad)
    kw = kw.reshape(batch, seq, kpad)
    gates = gates.reshape(batch, 3, g, r, seq)

    nseg = seq // CMP_STRIDE
    nb = seq // SEL_BLOCK
    seg_f = CMP_STRIDE * HEAD_DIM

    def segs(x):
        return x.reshape(batch, g, nseg, seg_f)

    pos = cmp_pos.reshape(2, 2, 1, seg_f)
    w1 = cmp_w1.reshape(2, 2, seg_f, cmp_w1.shape[-1]).astype(BF16)
    w2k = jnp.stack([_pad_heads(cmp_w2[0], 1, LANES), _pad_heads(_rot_heads(cmp_w2[0], 1), 1, LANES)]).astype(BF16)
    w2v = _pad_heads(cmp_w2[1], 1, VROWS).T.astype(BF16)
    kcc, vcc = _compress_call(segs(kc), segs(vc), pos, w1, w2k, w2v, tabs["cc"], tabs["sc"])
    kcc = kcc.reshape(batch, g, nb, 4, LANES).transpose(0, 1, 3, 2, 4).reshape(batch, g, nseg, LANES)
    vcc = vcc.reshape(batch, g, VROWS, nb, 4).transpose(0, 1, 2, 4, 3).reshape(batch, g, VROWS, nseg)

    o_c, mb = _cmp_call(qt, kcc, vcc, gates, batch, seq)
    o_sw = _attn_call("sel", qt, ks, vst, batch, seq, g, gates=gates, branch=1, mb=mb, emat=emat,
                      kw=kw, vwt=vwt, window=WINDOW_A)
    t_tokens = batch * seq
    return [o.reshape(t_tokens, nq) for o in (o_c, o_sw)]


def kernel(x, norm_attn, norm_ffn, a_w_in, a_cmp_pos, a_cmp_w1, a_cmp_w2, a_w_out, kv_norm, b_w_kv, b_w_q, b_sinks,
           b_w_out, ffn_w_in, ffn_conv_w, ffn_conv_b, ffn_w_out, final_norm):
    batch, seq, d = x.shape
    depth = norm_attn.shape[0]
    n_a = a_w_in.shape[0]
    tabs = _rope_tables(seq)
    mr = min(MASK_ROWS, seq // SEL_BLOCK)
    emat = (jnp.arange(mr * SEL_BLOCK)[:, None] // SEL_BLOCK == jnp.arange(mr)[None, :]).astype(BF16)
    h = x.reshape(batch * seq, d)
    k_sh = v_sh = None
    gb, rb = B_KV_HEADS, N_HEADS // B_KV_HEADS
    for layer in range(depth):
        if layer < n_a:
            o_list = _nsa_attention(h, norm_attn[layer], a_w_in[layer], a_cmp_pos[layer], a_cmp_w1[layer],
                                    a_cmp_w2[layer], tabs, emat, batch, seq)
            wo = a_w_out[layer]
        else:
            j = layer - n_a
            if k_sh is None:
                wk, wv = jnp.split(b_w_kv, 2, axis=1)
                kpad = gb * LANES
                k_sh, v_sh = _proj_call(
                    h, kv_norm, tabs, _rope_k_weights(wk, gb).astype(BF16), _pad_heads(wv, gb, VROWS).T.astype(BF16),
                    [("rope", 0, kpad)], [("vaug", 0, gb * VROWS)], [BF16], batch, seq)
                k_sh = k_sh.reshape(batch, seq, kpad)
            (qt,) = _proj_call(h, norm_attn[layer], tabs, None, b_w_q[j].T.astype(BF16), [],
                               [("ropeq", 0, N_HEADS * HEAD_DIM)], [], batch, seq)
            sinks = jnp.broadcast_to((b_sinks[j] * LOG2E).reshape(gb, 1, rb, 1),
                                     (gb, 1, rb, TQ_WINDOW)).reshape(gb, 1, rb * TQ_WINDOW)
            o = _attn_call("swa", qt, k_sh, v_sh, batch, seq, gb, sinks=sinks, window=WINDOW_B)
            o_list = [o.reshape(batch * seq, N_HEADS * HEAD_DIM)]
            wo = b_w_out[j]
        wa, wg, cw, wout = _ffn_weights(ffn_w_in[layer], ffn_conv_w[layer], ffn_conv_b[layer], ffn_w_out[layer])
        h = _ffn_call(h, o_list, wo.astype(BF16), norm_ffn[layer], wa, wg, cw, wout, seq,
                      final_gain=final_norm if layer == depth - 1 else None)
    return h.reshape(batch, seq, d)
```

```python
import functools
import math

import jax
import jax.numpy as jnp
from jax import lax
from jax.experimental import pallas as pl
from jax.experimental.pallas import tpu as pltpu

F32 = jnp.float32
BF16 = jnp.bfloat16

HEAD_DIM = 64
HALF = HEAD_DIM // 2
N_HEADS = 16
A_KV_HEADS = 4
B_KV_HEADS = 2
CMP_BLOCK = 32
CMP_STRIDE = 16
SEL_BLOCK = 64
N_SELECT = 16
WINDOW_A = 512
WINDOW_B = 128
CONV_WIDTH = 3
ROPE_THETA = 10000.0
EPS = 1e-6
FORCE = 1e6

LANES = 128
BF16_SUBLANES = 16
VMEM_LIMIT = 56 * 1024 * 1024
TQ = 256
TQ_WINDOW = 256
TK = 256
SEL_GROUP = 4
CMP_BUCKET_ALIGN = LANES
TM = 512
FF_CHUNK = 256
VROWS = HEAD_DIM + BF16_SUBLANES
MASK_ROWS = 128

LOG2E = 1.4426950408889634
NEG = -1e30
M_INIT = -1e29
REMOVED = -3e38

_NT = (((1,), (1,)), ((), ()))


def _cparams(sem):
    return pltpu.CompilerParams(dimension_semantics=sem, vmem_limit_bytes=VMEM_LIMIT)


def _const_spec(shape):
    n = len(shape)
    return pl.BlockSpec(shape, lambda *_: (0,) * n, pipeline_mode=pl.Buffered(1))


def _rms(x, g):
    ms = jnp.mean(x * x, axis=-1, keepdims=True)
    return x * lax.rsqrt(ms + EPS) * g


def _proj_body(*refs, nat_plan, tr_plan, tm):
    it = iter(refs)
    h_ref, g_ref = next(it), next(it)
    if nat_plan:
        cn_ref, sn_ref, wn_ref = next(it), next(it), next(it)
    ct_ref, st_ref, wt_ref = next(it), next(it), next(it)
    outs = list(it)

    hn = _rms(h_ref[...], g_ref[...]).astype(BF16)
    oi = 0
    for kind, c0, n in nat_plan:
        o_ref = outs[oi]
        oi += 1
        y = jnp.dot(hn, wn_ref[:, c0:c0 + n], preferred_element_type=F32)
        if kind == "rope":
            yr = jnp.dot(hn, wn_ref[:, c0 + n:c0 + 2 * n], preferred_element_type=F32)
            c, s = cn_ref[...], sn_ref[...]
            for g in range(n // LANES):
                sl = slice(LANES * g, LANES * (g + 1))
                o_ref[:, sl] = (y[:, sl] * c + yr[:, sl] * s).astype(o_ref.dtype)
        elif kind == "heads":
            for g in range(n // HEAD_DIM):
                o_ref[g] = y[:, HEAD_DIM * g:HEAD_DIM * (g + 1)].astype(o_ref.dtype)
        else:
            o_ref[...] = y.astype(o_ref.dtype)
    for kind, r0, n in tr_plan:
        o_ref = outs[oi]
        oi += 1
        y = lax.dot_general(wt_ref[r0:r0 + n, :], hn, _NT, preferred_element_type=F32)
        if kind == "ropeq":
            c, s = ct_ref[...], st_ref[...]
            for hd in range(n // HEAD_DIM):
                a = HEAD_DIM * hd
                y1, y2 = y[a:a + HALF], y[a + HALF:a + HEAD_DIM]
                o_ref[a:a + HALF, :] = (y1 * c - y2 * s).astype(o_ref.dtype)
                o_ref[a + HALF:a + HEAD_DIM, :] = (y2 * c + y1 * s).astype(o_ref.dtype)
        elif kind == "sigmoid":
            o_ref[...] = jax.nn.sigmoid(y)
        else:
            rows = lax.broadcasted_iota(jnp.int32, (VROWS, tm), 0)
            for g in range(n // VROWS):
                yg = jnp.where(rows == HEAD_DIM, 1.0, y[VROWS * g:VROWS * (g + 1)]).astype(o_ref.dtype)
                for t in range(tm // TK):
                    o_ref[t, VROWS * g:VROWS * (g + 1), :] = yg[:, t * TK:(t + 1) * TK]


def _proj_call(h, gain, tabs, wn, wt, nat_plan, tr_plan, nat_dtypes, batch, seq):
    t_tokens, d = h.shape
    tm = min(TM, seq)
    ns = seq // tm
    in_specs = [pl.BlockSpec((tm, d), lambda i: (i, 0)), _const_spec((1, d))]
    args = [h, gain.reshape(1, d)]
    if nat_plan:
        in_specs += [pl.BlockSpec((tm, LANES), lambda i: (i % ns, 0)),
                     pl.BlockSpec((tm, LANES), lambda i: (i % ns, 0)),
                     _const_spec(wn.shape)]
        args += [tabs["cn"], tabs["sn"], wn]
    in_specs += [pl.BlockSpec((HALF, tm), lambda i: (0, i % ns)),
                 pl.BlockSpec((HALF, tm), lambda i: (0, i % ns)),
                 _const_spec(wt.shape)]
    args += [tabs["ct"], tabs["st"], wt]
    out_shape, out_specs = [], []
    for (kind, _, n), dt in zip(nat_plan, nat_dtypes):
        if kind == "heads":
            nh = n // HEAD_DIM
            out_shape.append(jax.ShapeDtypeStruct((batch, nh, seq, HEAD_DIM), dt))
            out_specs.append(pl.BlockSpec((None, nh, tm, HEAD_DIM), lambda i: (i // ns, 0, i % ns, 0)))
        else:
            out_shape.append(jax.ShapeDtypeStruct((t_tokens, n), dt))
            out_specs.append(pl.BlockSpec((tm, n), lambda i: (i, 0)))
    for kind, _, n in tr_plan:
        if kind == "vaug":
            out_shape.append(jax.ShapeDtypeStruct((batch, seq // TK, n, TK), BF16))
            out_specs.append(pl.BlockSpec((None, tm // TK, n, TK), lambda i: (i // ns, i % ns, 0, 0)))
        else:
            dt = F32 if kind == "sigmoid" else BF16
            out_shape.append(jax.ShapeDtypeStruct((batch, n, seq), dt))
            out_specs.append(pl.BlockSpec((None, n, tm), lambda i: (i // ns, 0, i % ns)))
    return pl.pallas_call(
        functools.partial(_proj_body, nat_plan=tuple(nat_plan), tr_plan=tuple(tr_plan), tm=tm),
        grid=(t_tokens // tm,),
        in_specs=in_specs, out_specs=out_specs, out_shape=out_shape,
        compiler_params=_cparams(("parallel",)),
        name="norm_proj",
    )(*args)


def _gelu_tanh(x):
    c = math.sqrt(2.0 / math.pi)
    return x * (0.5 * (1.0 + jnp.tanh(c * (x + 0.044715 * (x * x * x)))))


def _compress_body(sk_ref, sv_ref, pos_ref, w1_ref, w2k_ref, w2v_ref, cc_ref, sc_ref, ok_ref, ov_ref, *, ncp):
    def hidden(seg, kv):
        xa = (seg + pos_ref[kv, 0]).astype(BF16)
        xb = (seg + pos_ref[kv, 1]).astype(BF16)
        a = jnp.dot(xa, w1_ref[kv, 0], preferred_element_type=F32)
        b = jnp.dot(xb, w1_ref[kv, 1], preferred_element_type=F32)
        return _gelu_tanh(a + pltpu.roll(b, ncp - 1, axis=0)).astype(BF16)

    gk = hidden(sk_ref[...], 0)
    k = jnp.dot(gk, w2k_ref[0], preferred_element_type=F32)
    kr = jnp.dot(gk, w2k_ref[1], preferred_element_type=F32)
    ok_ref[...] = (k * cc_ref[...] + kr * sc_ref[...]).astype(ok_ref.dtype)
    gv = hidden(sv_ref[...], 1)
    vt = lax.dot_general(w2v_ref[...], gv, _NT, preferred_element_type=F32)
    rows = lax.broadcasted_iota(jnp.int32, (VROWS, ncp), 0)
    ov_ref[...] = jnp.where(rows == HEAD_DIM, 1.0, vt).astype(ov_ref.dtype)


def _compress_call(segk, segv, pos, w1, w2k, w2v, cc, sc):
    b, g, ncp, f = segk.shape
    seg_spec = pl.BlockSpec((None, None, ncp, f), lambda i, j: (i, j, 0, 0))
    return pl.pallas_call(
        functools.partial(_compress_body, ncp=ncp),
        grid=(b, g),
        in_specs=[seg_spec, seg_spec, _const_spec(pos.shape), _const_spec(w1.shape), _const_spec(w2k.shape),
                  _const_spec(w2v.shape), _const_spec(cc.shape), _const_spec(sc.shape)],
        out_specs=[pl.BlockSpec((None, None, ncp, LANES), lambda i, j: (i, j, 0, 0)),
                   pl.BlockSpec((None, None, VROWS, ncp), lambda i, j: (i, j, 0, 0))],
        out_shape=[jax.ShapeDtypeStruct((b, g, ncp, LANES), BF16),
                   jax.ShapeDtypeStruct((b, g, VROWS, ncp), BF16)],
        compiler_params=_cparams(("parallel", "parallel")),
        name="compress",
    )(segk, segv, pos, w1, w2k, w2v, cc, sc)


def _load_queries(q_ref, qa_ref, r, tq):
    for rr in range(r):
        qa_ref[0:HEAD_DIM, rr * tq:(rr + 1) * tq] = q_ref[HEAD_DIM * rr:HEAD_DIM * (rr + 1), :]
    qa_ref[HEAD_DIM:LANES, :] = jnp.zeros((LANES - HEAD_DIM, r * tq), BF16)


def _gated_heads(acc, gate_ref, r, tq, colscale=None):
    inv = 1.0 / acc[HEAD_DIM:HEAD_DIM + 1, :]
    if colscale is not None:
        inv = inv * colscale
    o = acc[0:HEAD_DIM, :] * inv
    parts = []
    for rr in range(r):
        z = o[:, rr * tq:(rr + 1) * tq]
        if gate_ref is not None:
            z = z * gate_ref[rr:rr + 1, :]
        parts.append(z)
    return jnp.concatenate(parts, axis=0)


def _store_heads(o_ref, acc, gate_ref, r, tq, colscale=None):
    o_ref[...] = _gated_heads(acc, gate_ref, r, tq, colscale).T


def _window_tiles(window, tq):
    nt = max((s0 + tq - 1) // TK - (s0 - window + 1) // TK + 1 for s0 in range(8 * window, 8 * window + TK, tq))
    return nt, tuple((i == nt - 1, (nt - i) * TK - 1 >= window) for i in range(nt))


def _window_attend(qa_of, k_ref, v_ref, j0, u, checks, *, window, start, tq, r, m0_of=None, acc0=None,
                   between=None):
    row0 = pl.multiple_of(j0 * TK, TK)
    kt = k_ref[pl.ds(row0, u * TK), :]
    vt = jnp.concatenate([v_ref[j0 + i] for i in range(u)], axis=1)
    d0 = (start - row0) + (lax.broadcasted_iota(jnp.int32, (TK, tq), 1)
                           - lax.broadcasted_iota(jnp.int32, (TK, tq), 0))
    oks = []
    for i in range(u):
        lower, upper = checks[i]
        if lower and upper:
            oks.append((d0 >= i * TK) & (d0 < window + i * TK))
        elif lower:
            oks.append(d0 >= i * TK)
        elif upper:
            oks.append(d0 < window + i * TK)
        else:
            oks.append(None)
    cw = max(tq, 2 * LANES)
    chains = [slice(c0, c0 + cw) for c0 in range(0, r * tq, cw)]
    scores = [jnp.dot(kt, qa_of(ch), preferred_element_type=F32) for ch in chains]
    if between is not None:
        between()

    def masked(si, ok):
        if ok is None:
            return si
        return jnp.concatenate([jnp.where(ok, si[:, c0:c0 + tq], NEG) for c0 in range(0, cw, tq)], axis=1)

    outs = []
    for ch, s in zip(chains, scores):
        s = jnp.concatenate([masked(s[i * TK:(i + 1) * TK], oks[i]) for i in range(u)], axis=0)
        m = jnp.max(s, axis=0, keepdims=True)
        if m0_of is not None:
            m = jnp.maximum(m, m0_of(ch))
        acc = jnp.dot(vt, jnp.exp2(s - m).astype(BF16), preferred_element_type=F32)
        if acc0 is not None:
            acc = acc + acc0 * jnp.exp2(m0_of(ch) - m)
        outs.append(acc)
    return jnp.concatenate(outs, axis=1)


def _cmp_body(q_ref, k_ref, v_ref, gate_ref, o_ref, mb_ref, qa_ref, *, tq, nb, r):
    qi = pl.program_id(2)
    _load_queries(q_ref, qa_ref, r, tq)
    buckets = [b for b in (nb // 4, nb // 2, 3 * nb // 4) if b >= N_SELECT and b % CMP_BUCKET_ALIGN == 0] + [nb]
    args = (q_ref, k_ref, v_ref, gate_ref, o_ref, mb_ref, qa_ref, qi)

    @pl.when(qi == 0)
    def _():
        _cmp_rows(*args, tq=tq, nb=nb, r=r, rows=buckets[0], forced_distinct=False)

    lo = 1
    for rows in buckets:
        hi = rows * SEL_BLOCK // tq if rows < nb else pl.num_programs(2)

        @pl.when((qi >= lo) & (qi < hi))
        def _(rows=rows):
            _cmp_rows(*args, tq=tq, nb=nb, r=r, rows=rows, forced_distinct=True)
        lo = hi


def _cmp_rows(q_ref, k_ref, v_ref, gate_ref, o_ref, mb_ref, qa_ref, qi, *, tq, nb, r, rows, forced_distinct):
    lw = r * tq
    kt = jnp.concatenate([k_ref[m * nb:m * nb + rows, :] for m in range(4)], axis=0)
    vt = jnp.concatenate([v_ref[:, m * nb:m * nb + rows] for m in range(4)], axis=1)
    s = jnp.dot(kt, qa_ref[...], preferred_element_type=F32)
    if rows < nb:
        mb_ref[rows:nb, :] = jnp.full((nb - rows, tq), NEG, mb_ref.dtype)
    nb = rows
    j_io = lax.broadcasted_iota(jnp.int32, (nb, tq), 0)
    tt = qi * tq + lax.broadcasted_iota(jnp.int32, (nb, tq), 1)
    lim = tt - (CMP_BLOCK - 1)
    sm = []
    for m in range(4):
        valid = SEL_BLOCK * j_io + CMP_STRIDE * m <= lim
        sm.append(jnp.concatenate(
            [jnp.where(valid, s[m * nb:(m + 1) * nb, rr * tq:(rr + 1) * tq], NEG) for rr in range(r)], axis=1))
    mx = jnp.max(jnp.maximum(jnp.maximum(sm[0], sm[1]), jnp.maximum(sm[2], sm[3])), axis=0, keepdims=True)
    pm = [jnp.exp2(sm[m] - mx) for m in range(4)]
    den = jnp.sum(pm[0] + pm[1] + pm[2] + pm[3], axis=0, keepdims=True)
    t_lane = qi * tq + (lax.broadcasted_iota(jnp.int32, (1, lw), 1) & (tq - 1))
    has_key = jnp.where(t_lane >= CMP_BLOCK - 1, 1.0, 0.0)
    acc = jnp.dot(vt, jnp.concatenate(pm, axis=0).astype(BF16), preferred_element_type=F32)
    _store_heads(o_ref, acc, gate_ref, r, tq, has_key)

    inv = has_key / den
    ps = []
    for m in range(4):
        pn = pm[m] * inv
        acc_h = pn[:, 0:tq]
        for rr in range(1, r):
            acc_h = acc_h + pn[:, rr * tq:(rr + 1) * tq]
        ps.append(acc_h)
    prev3 = jnp.where(j_io == 0, 0.0, pltpu.roll(ps[3], 1, axis=0))
    imp = prev3 + 2.0 * (ps[0] + ps[1] + ps[2]) + ps[3]
    cur = tt >> (SEL_BLOCK.bit_length() - 1)
    forced = (j_io == 0) | (j_io == cur) | (j_io == cur - 1)
    causal = j_io * SEL_BLOCK <= tt
    v = jnp.where(forced, REMOVED if forced_distinct else FORCE, jnp.where(causal, imp, -FORCE))
    jf = j_io.astype(F32)
    for _ in range(min(N_SELECT, nb) - (3 if forced_distinct else 0)):
        top = jnp.max(v, axis=0, keepdims=True)
        idx = jnp.min(jnp.where(v == top, jf, float(nb)), axis=0, keepdims=True)
        v = jnp.where(jf == idx, REMOVED, v)
    mb_ref[0:nb, :] = jnp.where(v == REMOVED, 0.0, NEG).astype(mb_ref.dtype)


def _cmp_call(qt, kc, vct, gates, batch, seq):
    g, r = A_KV_HEADS, N_HEADS // A_KV_HEADS
    nb = seq // SEL_BLOCK
    ncp = 4 * nb
    tq = TQ_WINDOW
    return pl.pallas_call(
        functools.partial(_cmp_body, tq=tq, nb=nb, r=r),
        grid=(batch, g, seq // tq),
        in_specs=[pl.BlockSpec((None, r * HEAD_DIM, tq), lambda b, gg, q: (b, gg, q)),
                  pl.BlockSpec((None, None, ncp, LANES), lambda b, gg, q: (b, gg, 0, 0)),
                  pl.BlockSpec((None, None, VROWS, ncp), lambda b, gg, q: (b, gg, 0, 0)),
                  pl.BlockSpec((None, None, None, r, tq), lambda b, gg, q: (b, 0, gg, 0, q))],
        out_specs=[pl.BlockSpec((None, tq, r * HEAD_DIM), lambda b, gg, q: (b, q, gg)),
                   pl.BlockSpec((None, None, nb, tq), lambda b, gg, q: (b, gg, 0, q))],
        out_shape=[jax.ShapeDtypeStruct((batch, seq, N_HEADS * HEAD_DIM), F32),
                   jax.ShapeDtypeStruct((batch, g, nb, seq), BF16)],
        scratch_shapes=[pltpu.VMEM((LANES, r * tq), BF16)],
        compiler_params=_cparams(("parallel", "parallel", "parallel")),
        name="cmp_attn_topk",
    )(qt, kc, vct, gates)


def _sel_body(q_ref, k_ref, v_ref, gate_ref, mb_ref, e_ref, kw_ref, vw_ref, gatew_ref, o_ref,
              qa_ref, acc_ref, accw_ref, m_ref, macc_ref,
              s0_ref, s1_ref, p0_ref, p1_ref, mt0_ref, mt1_ref, cm0_ref, cm1_ref, *, tq, r, mr, nb, window):
    s_refs, p_refs, mt_refs, cm_refs = (s0_ref, s1_ref), (p0_ref, p1_ref), (mt0_ref, mt1_ref), (cm0_ref, cm1_ref)
    qi = pl.program_id(2)
    lw = r * tq
    start = qi * tq
    u = p0_ref.shape[0]
    gph = mr * SEL_BLOCK // (u * TK)
    assert tq == TK
    jd = start // TK
    gd = jd // u
    nh = nb // mr

    _load_queries(q_ref, qa_ref.at[0], r, tq)
    for hf in range(1, nh):
        qa_ref[hf, 0:LANES, :] = qa_ref[0, 0:LANES, :]
    for hf in range(nh):
        for rr in range(r):
            qa_ref[hf, LANES:LANES + mr, rr * tq:(rr + 1) * tq] = mb_ref[hf * mr:(hf + 1) * mr, :]
    acc_ref[...] = jnp.zeros((VROWS, lw), F32)

    def key_tile(gi, i):
        row0 = pl.multiple_of(gi * (u * TK), u * TK)
        eoff = pl.multiple_of(row0 % (mr * SEL_BLOCK), u * TK)
        return jnp.concatenate([k_ref[pl.ds(row0 + i * TK, TK), :], e_ref[pl.ds(eoff + i * TK, TK), :]], axis=1)

    def causal(gi, i):
        t_io = start + (lax.broadcasted_iota(jnp.int32, (TK, lw), 1) & (tq - 1))
        return (gi * u + i) * TK + lax.broadcasted_iota(jnp.int32, (TK, lw), 0) <= t_io

    def pipeline(stage_a, stage_b, stage_c):
        def trip(gi, slot):
            stage_a(gi + 1, 1 - slot)
            stage_b(gi, slot, None)
            stage_c(gi - 1, 1 - slot, u)

        def last(slot, has_prev):
            for kd in range(u):
                @pl.when(jd % u == kd)
                def _(kd=kd):
                    stage_b(gd, slot, kd)
                    if has_prev:
                        stage_c(gd - 1, 1 - slot, u)
                    stage_c(gd, slot, kd + 1)

        nt, _ = _window_tiles(window, tq)
        accw_ref[...] = _window_attend(lambda ch: qa_ref[0, 0:LANES, ch], kw_ref, vw_ref,
                                       jnp.maximum(start // TK - (nt - 1), 0), nt, ((True, True),) * nt,
                                       window=window, start=start, tq=tq, r=r, between=lambda: stage_a(0, 0))

        @pl.when(gd == 0)
        def _():
            last(0, False)

        @pl.when(gd > 0)
        def _():
            stage_a(1, 1)
            stage_b(0, 0, None)
            n = gd - 1

            def body(pi, c):
                trip(1 + 2 * pi, 1)
                trip(2 + 2 * pi, 0)
                return c
            lax.fori_loop(0, n // 2, body, 0)

            @pl.when(n % 2 == 1)
            def _():
                trip(gd - 1, 1)
                last(0, True)

            @pl.when(n % 2 == 0)
            def _():
                last(1, True)

    m_ref[...] = jnp.full((1, lw), M_INIT, F32)
    macc_ref[...] = jnp.full((1, lw), M_INIT, F32)

    def scores(gi, slot):
        qa = qa_ref[gi // gph]
        for i in range(u):
            s = jnp.dot(key_tile(gi, i), qa, preferred_element_type=F32)
            s_refs[slot][i] = s
            cm_refs[slot][i] = jnp.max(s, axis=0, keepdims=True)

    def softmax(gi, slot, diag):
        m = m_ref[...]
        for i in range(u if diag is None else diag + 1):
            if i == diag:
                s = jnp.where(causal(gi, i), s_refs[slot][i], NEG)
                m = jnp.maximum(m, jnp.max(s, axis=0, keepdims=True))
            else:
                m = jnp.maximum(m, cm_refs[slot][i])
                s = s_refs[slot][i]
            p_refs[slot][i] = jnp.exp2(s - m).astype(BF16)
            mt_refs[slot][i] = m
        m_ref[...] = m

    def values(gi, slot, ntiles):
        acc, ma = acc_ref[...], macc_ref[...]
        for i in range(ntiles):
            mt = mt_refs[slot][i]
            acc = acc * jnp.exp2(ma - mt) + jnp.dot(v_ref[gi * u + i], p_refs[slot][i],
                                                    preferred_element_type=F32)
            ma = mt
        acc_ref[...], macc_ref[...] = acc, ma

    pipeline(scores, softmax, values)
    o_ref[...] = (_gated_heads(acc_ref[...], gate_ref, r, tq) + _gated_heads(accw_ref[...], gatew_ref, r, tq)).T


def _attn_body(*refs, mode, tq, r, window):
    if mode == "win":
        q_ref, k_ref, v_ref, gate_ref, o_ref, qa_ref, acc_ref = refs
        m0_of = acc0 = None
    else:
        q_ref, k_ref, v_ref, sink_ref, o_ref, qa_ref, acc_ref = refs
        gate_ref = None
        m0_of = lambda ch: sink_ref[:, ch]
        acc0 = jnp.where(lax.broadcasted_iota(jnp.int32, (VROWS, 1), 0) == HEAD_DIM, 1.0, 0.0)
    qi = pl.program_id(2)
    start = qi * tq
    jd = start // TK
    _load_queries(q_ref, qa_ref, r, tq)

    def step(j0, checks):
        acc_ref[...] = _window_attend(lambda ch: qa_ref[:, ch], k_ref, v_ref, j0, nt, checks, window=window,
                                      start=start, tq=tq, r=r, m0_of=m0_of, acc0=acc0)

    nt, roles = _window_tiles(window, tq)

    @pl.when(jd >= nt - 1)
    def _():
        step(jd - (nt - 1), roles)

    @pl.when(jd < nt - 1)
    def _():
        step(0, ((True, True),) * nt)

    _store_heads(o_ref, acc_ref[...], gate_ref, r, tq)


def _attn_call(mode, qt, k, vt, batch, seq, g, gates=None, branch=None, mb=None, emat=None, sinks=None,
               window=None, kw=None, vwt=None):
    r = N_HEADS // g
    nb = seq // SEL_BLOCK
    mr = emat.shape[1] if mode == "sel" else 0
    nkt = seq // TK
    kc = LANES + mr
    tq = TQ if mode == "sel" else TQ_WINDOW
    k_spec = pl.BlockSpec((None, seq, LANES), lambda b, gg, q: (b, 0, gg))
    v_spec = pl.BlockSpec((None, nkt, VROWS, TK), lambda b, gg, q: (b, 0, gg, 0))
    in_specs = [pl.BlockSpec((None, r * HEAD_DIM, tq), lambda b, gg, q: (b, gg, q)), k_spec, v_spec]
    args = [qt, k, vt]
    if mode in ("sel", "win"):
        in_specs.append(pl.BlockSpec((None, None, None, r, tq), lambda b, gg, q: (b, branch, gg, 0, q)))
        args.append(gates)
    if mode == "sel":
        in_specs += [pl.BlockSpec((None, None, nb, tq), lambda b, gg, q: (b, gg, 0, q)), _const_spec(emat.shape),
                     k_spec, v_spec,
                     pl.BlockSpec((None, None, None, r, tq), lambda b, gg, q: (b, branch + 1, gg, 0, q))]
        args += [mb, emat, kw, vwt, gates]
    if mode == "swa":
        in_specs.append(pl.BlockSpec((None, 1, r * tq), lambda b, gg, q: (gg, 0, 0)))
        args.append(sinks)
    lw = r * tq
    scratch = [pltpu.VMEM((kc, lw), BF16), pltpu.VMEM((VROWS, lw), F32)]
    if mode == "sel":
        u = min(SEL_GROUP, mr * SEL_BLOCK // TK)
        scratch[0] = pltpu.VMEM((nb // mr, kc, lw), BF16)
        scratch += ([pltpu.VMEM((VROWS, lw), F32)] + [pltpu.VMEM((1, lw), F32)] * 2
                    + [pltpu.VMEM((u, TK, lw), F32)] * 2
                    + [pltpu.VMEM((u, TK, lw), BF16)] * 2 + [pltpu.VMEM((u, 1, lw), F32)] * 4)
        body = functools.partial(_sel_body, tq=tq, r=r, mr=mr, nb=nb, window=window)
    else:
        body = functools.partial(_attn_body, mode=mode, tq=tq, r=r, window=window)
    return pl.pallas_call(
        body,
        grid=(batch, g, seq // tq),
        in_specs=in_specs,
        out_specs=pl.BlockSpec((None, tq, r * HEAD_DIM), lambda b, gg, q: (b, q, gg)),
        out_shape=jax.ShapeDtypeStruct((batch, seq, N_HEADS * HEAD_DIM), F32),
        scratch_shapes=scratch,
        compiler_params=_cparams(("parallel", "parallel", "arbitrary")),
        name=mode + "_attn",
    )(*args)


def _ffn_body(*refs, n_o, final, tm, ns, nchunk):
    it = iter(refs)
    h_ref = next(it)
    o_refs = [next(it) for _ in range(n_o)]
    wo_ref, g_ref, wa_ref, wg_ref, cw_ref, wout_ref = (next(it) for _ in range(6))
    gf_ref = next(it) if final else None
    out_ref, hn_ref, y_ref, prev_ref = (next(it) for _ in range(4))
    u_refs = [next(it) for _ in range(3)]
    act_refs = [next(it) for _ in range(3)]

    osum = o_refs[0][...]
    for o_ref in o_refs[1:]:
        osum = osum + o_ref[...]
    h = h_ref[...] + jnp.dot(osum.astype(BF16), wo_ref[...], preferred_element_type=F32)
    hn_ref[...] = _rms(h, g_ref[...]).astype(BF16)
    y_ref[...] = h

    @pl.when(pl.program_id(0) % ns == 0)
    def _():
        prev_ref[...] = jnp.zeros_like(prev_ref)

    rid = lax.broadcasted_iota(jnp.int32, (8, FF_CHUNK), 0)

    def conv(u, p8, w):
        u1 = pltpu.roll(u, 1, axis=0)
        u2 = pltpu.roll(u, 2, axis=0)
        f1 = jnp.where(rid < 1, pltpu.roll(p8, 1, axis=0), u1[0:8])
        f2 = jnp.where(rid < 2, pltpu.roll(p8, 2, axis=0), u2[0:8])
        u1 = jnp.concatenate([f1, u1[8:]], axis=0)
        u2 = jnp.concatenate([f2, u2[8:]], axis=0)
        return w[3:4] + w[0:1] * u2 + w[1:2] * u1 + w[2:3] * u

    def proj_in(c, slot):
        hn = hn_ref[...]
        u_refs[slot][0] = jnp.dot(hn, wa_ref[c], preferred_element_type=F32)
        u_refs[slot][1] = jnp.dot(hn, wg_ref[c], preferred_element_type=F32)

    def gate(c, slot):
        ua, ug = u_refs[slot][0], u_refs[slot][1]
        cw = cw_ref[c]
        pa, pg = prev_ref[c, 0:8], prev_ref[c, 8:16]
        prev_ref[c, 0:8] = ua[tm - 8:tm]
        prev_ref[c, 8:16] = ug[tm - 8:tm]
        ca = conv(ua, pa, cw[0:4])
        cg = conv(ug, pg, cw[4:8])
        act_refs[slot][...] = (ca * jax.nn.sigmoid(ca) * cg).astype(BF16)

    def proj_out(c, slot):
        y_ref[...] += jnp.dot(act_refs[slot][...], wout_ref[c], preferred_element_type=F32)

    proj_in(0, 0)
    for c in range(nchunk):
        if c + 1 < nchunk:
            proj_in(c + 1, (c + 1) % 3)
        gate(c, c % 3)
        if c >= 1:
            proj_out(c - 1, (c - 1) % 3)
    proj_out(nchunk - 1, (nchunk - 1) % 3)
    out = y_ref[...]
    if final:
        out = _rms(out, gf_ref[...])
    out_ref[...] = out


def _ffn_call(h, o_list, wo, gain, wa, wg, cw, wout, seq, final_gain=None):
    t_tokens, d = h.shape
    tm = min(TM, seq)
    ns = seq // tm
    nchunk = wa.shape[0]
    tile = pl.BlockSpec((tm, d), lambda i: (i, 0))
    in_specs = [tile] + [tile] * len(o_list) + [
        _const_spec(wo.shape), _const_spec((1, d)), _const_spec(wa.shape), _const_spec(wg.shape),
        _const_spec(cw.shape), _const_spec(wout.shape)]
    args = [h] + list(o_list) + [wo, gain.reshape(1, d), wa, wg, cw, wout]
    if final_gain is not None:
        in_specs.append(_const_spec((1, d)))
        args.append(final_gain.reshape(1, d))
    return pl.pallas_call(
        functools.partial(_ffn_body, n_o=len(o_list), final=final_gain is not None, tm=tm, ns=ns, nchunk=nchunk),
        grid=(t_tokens // tm,),
        in_specs=in_specs, out_specs=tile,
        out_shape=jax.ShapeDtypeStruct((t_tokens, d), F32),
        scratch_shapes=[pltpu.VMEM((tm, d), BF16), pltpu.VMEM((tm, d), F32),
                        pltpu.VMEM((nchunk, 16, FF_CHUNK), F32)]
        + [pltpu.VMEM((2, tm, FF_CHUNK), F32)] * 3 + [pltpu.VMEM((tm, FF_CHUNK), BF16)] * 3,
        compiler_params=_cparams(("arbitrary",)),
        name="attn_out_ffn",
    )(*args)


def _pad_heads(w, g, width):
    d = w.shape[0]
    w3 = w.reshape(d, g, HEAD_DIM)
    return jnp.pad(w3, ((0, 0), (0, 0), (0, width - HEAD_DIM))).reshape(d, g * width)


def _rot_heads(w, g):
    d = w.shape[0]
    w3 = w.reshape(d, g, HEAD_DIM)
    return jnp.concatenate([-w3[..., HALF:], w3[..., :HALF]], axis=-1).reshape(d, g * HEAD_DIM)


def _rope_k_weights(w, g):
    return jnp.concatenate([_pad_heads(w, g, LANES), _pad_heads(_rot_heads(w, g), g, LANES)], axis=1)


def _rope_tables(seq):
    inv = jnp.float32(ROPE_THETA) ** (-jnp.arange(HALF, dtype=F32) / HALF)

    def cs(pos):
        ang = pos.astype(F32)[:, None] * inv[None, :]
        return jnp.cos(ang), jnp.sin(ang)

    def nat(c):
        return jnp.concatenate([c, c, jnp.zeros((c.shape[0], LANES - HEAD_DIM), F32)], axis=1)

    cos, sin = cs(jnp.arange(seq))
    qscale = HEAD_DIM ** -0.5 * LOG2E
    cc, sc = cs(jnp.arange(seq // CMP_STRIDE) * CMP_STRIDE + CMP_BLOCK - 1)
    return {"cn": nat(cos), "sn": nat(sin), "ct": (cos * qscale).T, "st": (sin * qscale).T,
            "cc": nat(cc), "sc": nat(sc)}


def _ffn_weights(w_in, conv_w, conv_b, w_out):
    d, two_ff = w_in.shape
    dff = two_ff // 2
    nchunk = dff // FF_CHUNK

    def chunks(w):
        return w.reshape(d, nchunk, FF_CHUNK).transpose(1, 0, 2).astype(BF16)

    wa, wg = chunks(w_in[:, :dff]), chunks(w_in[:, dff:])
    taps = jnp.concatenate([conv_w, conv_b[None, :]], axis=0)
    cw = jnp.concatenate([taps[:, :dff].reshape(4, nchunk, FF_CHUNK), taps[:, dff:].reshape(4, nchunk, FF_CHUNK)],
                         axis=0).transpose(1, 0, 2)
    return wa, wg, cw, w_out.reshape(nchunk, FF_CHUNK, d).astype(BF16)


def _nsa_attention(h, gain, w_in, cmp_pos, cmp_w1, cmp_w2, tabs, emat, batch, seq):
    g, r = A_KV_HEADS, N_HEADS // A_KV_HEADS
    d = h.shape[1]
    kvw = g * HEAD_DIM
    nq = N_HEADS * HEAD_DIM
    wq, wkc, wvc, wks, wvs, wkw, wvw, wgl = jnp.split(
        w_in, [nq, nq + kvw, nq + 2 * kvw, nq + 3 * kvw, nq + 4 * kvw, nq + 5 * kvw, nq + 6 * kvw], axis=1)
    wn = jnp.concatenate([_rope_k_weights(wks, g), _rope_k_weights(wkw, g), wkc, wvc], axis=1).astype(BF16)
    wgl = wgl.reshape(d, N_HEADS, 3).transpose(0, 2, 1).reshape(d, 3 * N_HEADS)
    wt = jnp.concatenate([wq, wgl, _pad_heads(wvs, g, VROWS), _pad_heads(wvw, g, VROWS)], axis=1).T.astype(BF16)
    kpad = g * LANES
    nat_plan = [("rope", 0, kpad), ("rope", 2 * kpad, kpad), ("heads", 4 * kpad, kvw), ("heads", 4 * kpad + kvw, kvw)]
    ng = 3 * N_HEADS
    tr_plan = [("ropeq", 0, nq), ("sigmoid", nq, ng), ("vaug", nq + ng, g * VROWS),
               ("vaug", nq + ng + g * VROWS, g * VROWS)]
    ks, kw, kc, vc, qt, gates, vst, vwt = _proj_call(
        h, gain, tabs, wn, wt, nat_plan, tr_plan, [BF16, BF16, F32, F32], batch, seq)
    ks = ks.reshape(batch, seq, kpad)
    kw = kw.reshape(batch, seq, kpad)
    gates = gates.reshape(batch, 3, g, r, seq)

    nseg = seq // CMP_STRIDE
    nb = seq // SEL_BLOCK
    seg_f = CMP_STRIDE * HEAD_DIM

    def segs(x):
        return x.reshape(batch, g, nseg, seg_f)

    pos = cmp_pos.reshape(2, 2, 1, seg_f)
    w1 = cmp_w1.reshape(2, 2, seg_f, cmp_w1.shape[-1]).astype(BF16)
    w2k = jnp.stack([_pad_heads(cmp_w2[0], 1, LANES), _pad_heads(_rot_heads(cmp_w2[0], 1), 1, LANES)]).astype(BF16)
    w2v = _pad_heads(cmp_w2[1], 1, VROWS).T.astype(BF16)
    kcc, vcc = _compress_call(segs(kc), segs(vc), pos, w1, w2k, w2v, tabs["cc"], tabs["sc"])
    kcc = kcc.reshape(batch, g, nb, 4, LANES).transpose(0, 1, 3, 2, 4).reshape(batch, g, nseg, LANES)
    vcc = vcc.reshape(batch, g, VROWS, nb, 4).transpose(0, 1, 2, 4, 3).reshape(batch, g, VROWS, nseg)

    o_c, mb = _cmp_call(qt, kcc, vcc, gates, batch, seq)
    o_sw = _attn_call("sel", qt, ks, vst, batch, seq, g, gates=gates, branch=1, mb=mb, emat=emat,
                      kw=kw, vwt=vwt, window=WINDOW_A)
    t_tokens = batch * seq
    return [o.reshape(t_tokens, nq) for o in (o_c, o_sw)]


def kernel(x, norm_attn, norm_ffn, a_w_in, a_cmp_pos, a_cmp_w1, a_cmp_w2, a_w_out, kv_norm, b_w_kv, b_w_q, b_sinks,
           b_w_out, ffn_w_in, ffn_conv_w, ffn_conv_b, ffn_w_out, final_norm):
    batch, seq, d = x.shape
    depth = norm_attn.shape[0]
    n_a = a_w_in.shape[0]
    tabs = _rope_tables(seq)
    mr = min(MASK_ROWS, seq // SEL_BLOCK)
    emat = (jnp.arange(mr * SEL_BLOCK)[:, None] // SEL_BLOCK == jnp.arange(mr)[None, :]).astype(BF16)
    h = x.reshape(batch * seq, d)
    k_sh = v_sh = None
    gb, rb = B_KV_HEADS, N_HEADS // B_KV_HEADS
    for layer in range(depth):
        if layer < n_a:
            o_list = _nsa_attention(h, norm_attn[layer], a_w_in[layer], a_cmp_pos[layer], a_cmp_w1[layer],
                                    a_cmp_w2[layer], tabs, emat, batch, seq)
            wo = a_w_out[layer]
        else:
            j = layer - n_a
            if k_sh is None:
                wk, wv = jnp.split(b_w_kv, 2, axis=1)
                kpad = gb * LANES
                k_sh, v_sh = _proj_call(
                    h, kv_norm, tabs, _rope_k_weights(wk, gb).astype(BF16), _pad_heads(wv, gb, VROWS).T.astype(BF16),
                    [("rope", 0, kpad)], [("vaug", 0, gb * VROWS)], [BF16], batch, seq)
                k_sh = k_sh.reshape(batch, seq, kpad)
            (qt,) = _proj_call(h, norm_attn[layer], tabs, None, b_w_q[j].T.astype(BF16), [],
                               [("ropeq", 0, N_HEADS * HEAD_DIM)], [], batch, seq)
            sinks = jnp.broadcast_to((b_sinks[j] * LOG2E).reshape(gb, 1, rb, 1),
                                     (gb, 1, rb, TQ_WINDOW)).reshape(gb, 1, rb * TQ_WINDOW)
            o = _attn_call("swa", qt, k_sh, v_sh, batch, seq, gb, sinks=sinks, window=WINDOW_B)
            o_list = [o.reshape(batch * seq, N_HEADS * HEAD_DIM)]
            wo = b_w_out[j]
        wa, wg, cw, wout = _ffn_weights(ffn_w_in[layer], ffn_conv_w[layer], ffn_conv_b[layer], ffn_w_out[layer])
        h = _ffn_call(h, o_list, wo.astype(BF16), norm_ffn[layer], wa, wg, cw, wout, seq,
                      final_gain=final_norm if layer == depth - 1 else None)
    return h.reshape(batch, seq, d)
```

```python
import functools
import math

import jax
import jax.numpy as jnp
from jax import lax
from jax.experimental import pallas as pl
from jax.experimental.pallas import tpu as pltpu

F32 = jnp.float32
BF16 = jnp.bfloat16

HEAD_DIM = 64
HALF = HEAD_DIM // 2
N_HEADS = 16
A_KV_HEADS = 4
B_KV_HEADS = 2
CMP_BLOCK = 32
CMP_STRIDE = 16
SEL_BLOCK = 64
N_SELECT = 16
WINDOW_A = 512
WINDOW_B = 128
CONV_WIDTH = 3
ROPE_THETA = 10000.0
EPS = 1e-6
FORCE = 1e6

LANES = 128
BF16_SUBLANES = 16
VMEM_LIMIT = 56 * 1024 * 1024
TQ = 256
TQ_WINDOW = 256
TK = 256
SEL_GROUP = 4
CMP_BUCKET_ALIGN = LANES
TM = 512
FF_CHUNK = 256
VROWS = HEAD_DIM + BF16_SUBLANES
MASK_ROWS = 128

LOG2E = 1.4426950408889634
NEG = -1e30
M_INIT = -1e29
REMOVED = -3e38

_NT = (((1,), (1,)), ((), ()))


def _cparams(sem):
    return pltpu.CompilerParams(dimension_semantics=sem, vmem_limit_bytes=VMEM_LIMIT)


def _const_spec(shape):
    n = len(shape)
    return pl.BlockSpec(shape, lambda *_: (0,) * n, pipeline_mode=pl.Buffered(1))


def _rms(x, g):
    ms = jnp.mean(x * x, axis=-1, keepdims=True)
    return x * lax.rsqrt(ms + EPS) * g


def _proj_body(*refs, nat_plan, tr_plan, tm):
    it = iter(refs)
    h_ref, g_ref = next(it), next(it)
    if nat_plan:
        cn_ref, sn_ref, wn_ref = next(it), next(it), next(it)
    ct_ref, st_ref, wt_ref = next(it), next(it), next(it)
    outs = list(it)

    hn = _rms(h_ref[...], g_ref[...]).astype(BF16)
    oi = 0
    for kind, c0, n in nat_plan:
        o_ref = outs[oi]
        oi += 1
        y = jnp.dot(hn, wn_ref[:, c0:c0 + n], preferred_element_type=F32)
        if kind == "rope":
            yr = jnp.dot(hn, wn_ref[:, c0 + n:c0 + 2 * n], preferred_element_type=F32)
            c, s = cn_ref[...], sn_ref[...]
            for g in range(n // LANES):
                sl = slice(LANES * g, LANES * (g + 1))
                o_ref[:, sl] = (y[:, sl] * c + yr[:, sl] * s).astype(o_ref.dtype)
        elif kind == "heads":
            for g in range(n // HEAD_DIM):
                o_ref[g] = y[:, HEAD_DIM * g:HEAD_DIM * (g + 1)].astype(o_ref.dtype)
        else:
            o_ref[...] = y.astype(o_ref.dtype)
    for kind, r0, n in tr_plan:
        o_ref = outs[oi]
        oi += 1
        y = lax.dot_general(wt_ref[r0:r0 + n, :], hn, _NT, preferred_element_type=F32)
        if kind == "ropeq":
            c, s = ct_ref[...], st_ref[...]
            for hd in range(n // HEAD_DIM):
                a = HEAD_DIM * hd
                y1, y2 = y[a:a + HALF], y[a + HALF:a + HEAD_DIM]
                o_ref[a:a + HALF, :] = (y1 * c - y2 * s).astype(o_ref.dtype)
                o_ref[a + HALF:a + HEAD_DIM, :] = (y2 * c + y1 * s).astype(o_ref.dtype)
        elif kind == "sigmoid":
            o_ref[...] = jax.nn.sigmoid(y)
        else:
            rows = lax.broadcasted_iota(jnp.int32, (VROWS, tm), 0)
            for g in range(n // VROWS):
                yg = jnp.where(rows == HEAD_DIM, 1.0, y[VROWS * g:VROWS * (g + 1)]).astype(o_ref.dtype)
                for t in range(tm // TK):
                    o_ref[t, VROWS * g:VROWS * (g + 1), :] = yg[:, t * TK:(t + 1) * TK]


def _proj_call(h, gain, tabs, wn, wt, nat_plan, tr_plan, nat_dtypes, batch, seq):
    t_tokens, d = h.shape
    tm = min(TM, seq)
    ns = seq // tm
    in_specs = [pl.BlockSpec((tm, d), lambda i: (i, 0)), _const_spec((1, d))]
    args = [h, gain.reshape(1, d)]
    if nat_plan:
        in_specs += [pl.BlockSpec((tm, LANES), lambda i: (i % ns, 0)),
                     pl.BlockSpec((tm, LANES), lambda i: (i % ns, 0)),
                     _const_spec(wn.shape)]
        args += [tabs["cn"], tabs["sn"], wn]
    in_specs += [pl.BlockSpec((HALF, tm), lambda i: (0, i % ns)),
                 pl.BlockSpec((HALF, tm), lambda i: (0, i % ns)),
                 _const_spec(wt.shape)]
    args += [tabs["ct"], tabs["st"], wt]
    out_shape, out_specs = [], []
    for (kind, _, n), dt in zip(nat_plan, nat_dtypes):
        if kind == "heads":
            nh = n // HEAD_DIM
            out_shape.append(jax.ShapeDtypeStruct((batch, nh, seq, HEAD_DIM), dt))
            out_specs.append(pl.BlockSpec((None, nh, tm, HEAD_DIM), lambda i: (i // ns, 0, i % ns, 0)))
        else:
            out_shape.append(jax.ShapeDtypeStruct((t_tokens, n), dt))
            out_specs.append(pl.BlockSpec((tm, n), lambda i: (i, 0)))
    for kind, _, n in tr_plan:
        if kind == "vaug":
            out_shape.append(jax.ShapeDtypeStruct((batch, seq // TK, n, TK), BF16))
            out_specs.append(pl.BlockSpec((None, tm // TK, n, TK), lambda i: (i // ns, i % ns, 0, 0)))
        else:
            dt = F32 if kind == "sigmoid" else BF16
            out_shape.append(jax.ShapeDtypeStruct((batch, n, seq), dt))
            out_specs.append(pl.BlockSpec((None, n, tm), lambda i: (i // ns, 0, i % ns)))
    return pl.pallas_call(
        functools.partial(_proj_body, nat_plan=tuple(nat_plan), tr_plan=tuple(tr_plan), tm=tm),
        grid=(t_tokens // tm,),
        in_specs=in_specs, out_specs=out_specs, out_shape=out_shape,
        compiler_params=_cparams(("parallel",)),
        name="norm_proj",
    )(*args)


def _gelu_tanh(x):
    c = math.sqrt(2.0 / math.pi)
    return x * (0.5 * (1.0 + jnp.tanh(c * (x + 0.044715 * (x * x * x)))))


def _compress_body(sk_ref, sv_ref, pos_ref, w1_ref, w2k_ref, w2v_ref, cc_ref, sc_ref, ok_ref, ov_ref, *, ncp):
    def hidden(seg, kv):
        xa = (seg + pos_ref[kv, 0]).astype(BF16)
        xb = (seg + pos_ref[kv, 1]).astype(BF16)
        a = jnp.dot(xa, w1_ref[kv, 0], preferred_element_type=F32)
        b = jnp.dot(xb, w1_ref[kv, 1], preferred_element_type=F32)
        return _gelu_tanh(a + pltpu.roll(b, ncp - 1, axis=0)).astype(BF16)

    gk = hidden(sk_ref[...], 0)
    k = jnp.dot(gk, w2k_ref[0], preferred_element_type=F32)
    kr = jnp.dot(gk, w2k_ref[1], preferred_element_type=F32)
    ok_ref[...] = (k * cc_ref[...] + kr * sc_ref[...]).astype(ok_ref.dtype)
    gv = hidden(sv_ref[...], 1)
    vt = lax.dot_general(w2v_ref[...], gv, _NT, preferred_element_type=F32)
    rows = lax.broadcasted_iota(jnp.int32, (VROWS, ncp), 0)
    ov_ref[...] = jnp.where(rows == HEAD_DIM, 1.0, vt).astype(ov_ref.dtype)


def _compress_call(segk, segv, pos, w1, w2k, w2v, cc, sc):
    b, g, ncp, f = segk.shape
    seg_spec = pl.BlockSpec((None, None, ncp, f), lambda i, j: (i, j, 0, 0))
    return pl.pallas_call(
        functools.partial(_compress_body, ncp=ncp),
        grid=(b, g),
        in_specs=[seg_spec, seg_spec, _const_spec(pos.shape), _const_spec(w1.shape), _const_spec(w2k.shape),
                  _const_spec(w2v.shape), _const_spec(cc.shape), _const_spec(sc.shape)],
        out_specs=[pl.BlockSpec((None, None, ncp, LANES), lambda i, j: (i, j, 0, 0)),
                   pl.BlockSpec((None, None, VROWS, ncp), lambda i, j: (i, j, 0, 0))],
        out_shape=[jax.ShapeDtypeStruct((b, g, ncp, LANES), BF16),
                   jax.ShapeDtypeStruct((b, g, VROWS, ncp), BF16)],
        compiler_params=_cparams(("parallel", "parallel")),
        name="compress",
    )(segk, segv, pos, w1, w2k, w2v, cc, sc)


def _load_queries(q_ref, qa_ref, r, tq):
    for rr in range(r):
        qa_ref[0:HEAD_DIM, rr * tq:(rr + 1) * tq] = q_ref[HEAD_DIM * rr:HEAD_DIM * (rr + 1), :]
    qa_ref[HEAD_DIM:LANES, :] = jnp.zeros((LANES - HEAD_DIM, r * tq), BF16)


def _gated_heads(acc, gate_ref, r, tq, colscale=None):
    inv = 1.0 / acc[HEAD_DIM:HEAD_DIM + 1, :]
    if colscale is not None:
        inv = inv * colscale
    o = acc[0:HEAD_DIM, :] * inv
    parts = []
    for rr in range(r):
        z = o[:, rr * tq:(rr + 1) * tq]
        if gate_ref is not None:
            z = z * gate_ref[rr:rr + 1, :]
        parts.append(z)
    return jnp.concatenate(parts, axis=0)


def _store_heads(o_ref, acc, gate_ref, r, tq, colscale=None):
    o_ref[...] = _gated_heads(acc, gate_ref, r, tq, colscale).T


def _window_tiles(window, tq):
    nt = max((s0 + tq - 1) // TK - (s0 - window + 1) // TK + 1 for s0 in range(8 * window, 8 * window + TK, tq))
    return nt, tuple((i == nt - 1, (nt - i) * TK - 1 >= window) for i in range(nt))


def _window_attend(qa_of, k_ref, v_ref, j0, u, checks, *, window, start, tq, r, m0_of=None, acc0=None,
                   between=None):
    row0 = pl.multiple_of(j0 * TK, TK)
    kt = k_ref[pl.ds(row0, u * TK), :]
    vt = jnp.concatenate([v_ref[j0 + i] for i in range(u)], axis=1)
    d0 = (start - row0) + (lax.broadcasted_iota(jnp.int32, (TK, tq), 1)
                           - lax.broadcasted_iota(jnp.int32, (TK, tq), 0))
    oks = []
    for i in range(u):
        lower, upper = checks[i]
        if lower and upper:
            oks.append((d0 >= i * TK) & (d0 < window + i * TK))
        elif lower:
            oks.append(d0 >= i * TK)
        elif upper:
            oks.append(d0 < window + i * TK)
        else:
            oks.append(None)
    cw = max(tq, 2 * LANES)
    chains = [slice(c0, c0 + cw) for c0 in range(0, r * tq, cw)]
    scores = [jnp.dot(kt, qa_of(ch), preferred_element_type=F32) for ch in chains]
    if between is not None:
        between()

    def masked(si, ok):
        if ok is None:
            return si
        return jnp.concatenate([jnp.where(ok, si[:, c0:c0 + tq], NEG) for c0 in range(0, cw, tq)], axis=1)

    outs = []
    for ch, s in zip(chains, scores):
        s = jnp.concatenate([masked(s[i * TK:(i + 1) * TK], oks[i]) for i in range(u)], axis=0)
        m = jnp.max(s, axis=0, keepdims=True)
        if m0_of is not None:
            m = jnp.maximum(m, m0_of(ch))
        acc = jnp.dot(vt, jnp.exp2(s - m).astype(BF16), preferred_element_type=F32)
        if acc0 is not None:
            acc = acc + acc0 * jnp.exp2(m0_of(ch) - m)
        outs.append(acc)
    return jnp.concatenate(outs, axis=1)


def _cmp_body(q_ref, k_ref, v_ref, gate_ref, o_ref, mb_ref, qa_ref, *, tq, nb, r):
    qi = pl.program_id(2)
    _load_queries(q_ref, qa_ref, r, tq)
    buckets = [b for b in (nb // 4, nb // 2, 3 * nb // 4) if b >= N_SELECT and b % CMP_BUCKET_ALIGN == 0] + [nb]
    args = (q_ref, k_ref, v_ref, gate_ref, o_ref, mb_ref, qa_ref, qi)

    @pl.when(qi == 0)
    def _():
        _cmp_rows(*args, tq=tq, nb=nb, r=r, rows=buckets[0], forced_distinct=False)

    lo = 1
    for rows in buckets:
        hi = rows * SEL_BLOCK // tq if rows < nb else pl.num_programs(2)

        @pl.when((qi >= lo) & (qi < hi))
        def _(rows=rows):
            _cmp_rows(*args, tq=tq, nb=nb, r=r, rows=rows, forced_distinct=True)
        lo = hi


def _cmp_rows(q_ref, k_ref, v_ref, gate_ref, o_ref, mb_ref, qa_ref, qi, *, tq, nb, r, rows, forced_distinct):
    lw = r * tq
    kt = jnp.concatenate([k_ref[m * nb:m * nb + rows, :] for m in range(4)], axis=0)
    vt = jnp.concatenate([v_ref[:, m * nb:m * nb + rows] for m in range(4)], axis=1)
    s = jnp.dot(kt, qa_ref[...], preferred_element_type=F32)
    if rows < nb:
        mb_ref[rows:nb, :] = jnp.full((nb - rows, tq), NEG, mb_ref.dtype)
    nb = rows
    j_io = lax.broadcasted_iota(jnp.int32, (nb, tq), 0)
    tt = qi * tq + lax.broadcasted_iota(jnp.int32, (nb, tq), 1)
    lim = tt - (CMP_BLOCK - 1)
    sm = []
    for m in range(4):
        valid = SEL_BLOCK * j_io + CMP_STRIDE * m <= lim
        sm.append(jnp.concatenate(
            [jnp.where(valid, s[m * nb:(m + 1) * nb, rr * tq:(rr + 1) * tq], NEG) for rr in range(r)], axis=1))
    mx = jnp.max(jnp.maximum(jnp.maximum(sm[0], sm[1]), jnp.maximum(sm[2], sm[3])), axis=0, keepdims=True)
    pm = [jnp.exp2(sm[m] - mx) for m in range(4)]
    den = jnp.sum(pm[0] + pm[1] + pm[2] + pm[3], axis=0, keepdims=True)
    t_lane = qi * tq + (lax.broadcasted_iota(jnp.int32, (1, lw), 1) & (tq - 1))
    has_key = jnp.where(t_lane >= CMP_BLOCK - 1, 1.0, 0.0)
    acc = jnp.dot(vt, jnp.concatenate(pm, axis=0).astype(BF16), preferred_element_type=F32)
    _store_heads(o_ref, acc, gate_ref, r, tq, has_key)

    inv = has_key / den
    ps = []
    for m in range(4):
        pn = pm[m] * inv
        acc_h = pn[:, 0:tq]
        for rr in range(1, r):
            acc_h = acc_h + pn[:, rr * tq:(rr + 1) * tq]
        ps.append(acc_h)
    prev3 = jnp.where(j_io == 0, 0.0, pltpu.roll(ps[3], 1, axis=0))
    imp = prev3 + 2.0 * (ps[0] + ps[1] + ps[2]) + ps[3]
    cur = tt >> (SEL_BLOCK.bit_length() - 1)
    forced = (j_io == 0) | (j_io == cur) | (j_io == cur - 1)
    causal = j_io * SEL_BLOCK <= tt
    v = jnp.where(forced, REMOVED if forced_distinct else FORCE, jnp.where(causal, imp, -FORCE))
    jf = j_io.astype(F32)
    for _ in range(min(N_SELECT, nb) - (3 if forced_distinct else 0)):
        top = jnp.max(v, axis=0, keepdims=True)
        idx = jnp.min(jnp.where(v == top, jf, float(nb)), axis=0, keepdims=True)
        v = jnp.where(jf == idx, REMOVED, v)
    mb_ref[0:nb, :] = jnp.where(v == REMOVED, 0.0, NEG).astype(mb_ref.dtype)


def _cmp_call(qt, kc, vct, gates, batch, seq):
    g, r = A_KV_HEADS, N_HEADS // A_KV_HEADS
    nb = seq // SEL_BLOCK
    ncp = 4 * nb
    tq = TQ_WINDOW
    return pl.pallas_call(
        functools.partial(_cmp_body, tq=tq, nb=nb, r=r),
        grid=(batch, g, seq // tq),
        in_specs=[pl.BlockSpec((None, r * HEAD_DIM, tq), lambda b, gg, q: (b, gg, q)),
                  pl.BlockSpec((None, None, ncp, LANES), lambda b, gg, q: (b, gg, 0, 0)),
                  pl.BlockSpec((None, None, VROWS, ncp), lambda b, gg, q: (b, gg, 0, 0)),
                  pl.BlockSpec((None, None, None, r, tq), lambda b, gg, q: (b, 0, gg, 0, q))],
        out_specs=[pl.BlockSpec((None, tq, r * HEAD_DIM), lambda b, gg, q: (b, q, gg)),
                   pl.BlockSpec((None, None, nb, tq), lambda b, gg, q: (b, gg, 0, q))],
        out_shape=[jax.ShapeDtypeStruct((batch, seq, N_HEADS * HEAD_DIM), F32),
                   jax.ShapeDtypeStruct((batch, g, nb, seq), BF16)],
        scratch_shapes=[pltpu.VMEM((LANES, r * tq), BF16)],
        compiler_params=_cparams(("parallel", "parallel", "parallel")),
        name="cmp_attn_topk",
    )(qt, kc, vct, gates)


def _sel_body(q_ref, k_ref, v_ref, gate_ref, mb_ref, e_ref, kw_ref, vw_ref, gatew_ref, o_ref,
              qa_ref, acc_ref, accw_ref, m_ref, macc_ref,
              s0_ref, s1_ref, p0_ref, p1_ref, mt0_ref, mt1_ref, cm0_ref, cm1_ref, *, tq, r, mr, nb, window):
    s_refs, p_refs, mt_refs, cm_refs = (s0_ref, s1_ref), (p0_ref, p1_ref), (mt0_ref, mt1_ref), (cm0_ref, cm1_ref)
    qi = pl.program_id(2)
    lw = r * tq
    start = qi * tq
    u = p0_ref.shape[0]
    gph = mr * SEL_BLOCK // (u * TK)
    assert tq == TK
    jd = start // TK
    gd = jd // u
    nh = nb // mr

    _load_queries(q_ref, qa_ref.at[0], r, tq)
    for hf in range(1, nh):
        qa_ref[hf, 0:LANES, :] = qa_ref[0, 0:LANES, :]
    for hf in range(nh):
        for rr in range(r):
            qa_ref[hf, LANES:LANES + mr, rr * tq:(rr + 1) * tq] = mb_ref[hf * mr:(hf + 1) * mr, :]
    acc_ref[...] = jnp.zeros((VROWS, lw), F32)

    def key_tile(gi, i):
        row0 = pl.multiple_of(gi * (u * TK), u * TK)
        eoff = pl.multiple_of(row0 % (mr * SEL_BLOCK), u * TK)
        return jnp.concatenate([k_ref[pl.ds(row0 + i * TK, TK), :], e_ref[pl.ds(eoff + i * TK, TK), :]], axis=1)

    def causal(gi, i):
        t_io = start + (lax.broadcasted_iota(jnp.int32, (TK, lw), 1) & (tq - 1))
        return (gi * u + i) * TK + lax.broadcasted_iota(jnp.int32, (TK, lw), 0) <= t_io

    def pipeline(stage_a, stage_b, stage_c):
        def trip(gi, slot):
            stage_a(gi + 1, 1 - slot)
            stage_b(gi, slot, None)
            stage_c(gi - 1, 1 - slot, u)

        def last(slot, has_prev):
            for kd in range(u):
                @pl.when(jd % u == kd)
                def _(kd=kd):
                    stage_b(gd, slot, kd)
                    if has_prev:
                        stage_c(gd - 1, 1 - slot, u)
                    stage_c(gd, slot, kd + 1)

        nt, _ = _window_tiles(window, tq)
        accw_ref[...] = _window_attend(lambda ch: qa_ref[0, 0:LANES, ch], kw_ref, vw_ref,
                                       jnp.maximum(start // TK - (nt - 1), 0), nt, ((True, True),) * nt,
                                       window=window, start=start, tq=tq, r=r, between=lambda: stage_a(0, 0))

        @pl.when(gd == 0)
        def _():
            last(0, False)

        @pl.when(gd > 0)
        def _():
            stage_a(1, 1)
            stage_b(0, 0, None)
            n = gd - 1

            def body(pi, c):
                trip(1 + 2 * pi, 1)
                trip(2 + 2 * pi, 0)
                return c
            lax.fori_loop(0, n // 2, body, 0)

            @pl.when(n % 2 == 1)
            def _():
                trip(gd - 1, 1)
                last(0, True)

            @pl.when(n % 2 == 0)
            def _():
                last(1, True)

    m_ref[...] = jnp.full((1, lw), M_INIT, F32)
    macc_ref[...] = jnp.full((1, lw), M_INIT, F32)

    def scores(gi, slot):
        qa = qa_ref[gi // gph]
        for i in range(u):
            s = jnp.dot(key_tile(gi, i), qa, preferred_element_type=F32)
            s_refs[slot][i] = s
            cm_refs[slot][i] = jnp.max(s, axis=0, keepdims=True)

    def softmax(gi, slot, diag):
        m = m_ref[...]
        for i in range(u if diag is None else diag + 1):
            if i == diag:
                s = jnp.where(causal(gi, i), s_refs[slot][i], NEG)
                m = jnp.maximum(m, jnp.max(s, axis=0, keepdims=True))
            else:
                m = jnp.maximum(m, cm_refs[slot][i])
                s = s_refs[slot][i]
            p_refs[slot][i] = jnp.exp2(s - m).astype(BF16)
            mt_refs[slot][i] = m
        m_ref[...] = m

    def values(gi, slot, ntiles):
        acc, ma = acc_ref[...], macc_ref[...]
        for i in range(ntiles):
            mt = mt_refs[slot][i]
            acc = acc * jnp.exp2(ma - mt) + jnp.dot(v_ref[gi * u + i], p_refs[slot][i],
                                                    preferred_element_type=F32)
            ma = mt
        acc_ref[...], macc_ref[...] = acc, ma

    pipeline(scores, softmax, values)
    o_ref[...] = (_gated_heads(acc_ref[...], gate_ref, r, tq) + _gated_heads(accw_ref[...], gatew_ref, r, tq)).T


def _attn_body(*refs, mode, tq, r, window):
    if mode == "win":
        q_ref, k_ref, v_ref, gate_ref, o_ref, qa_ref, acc_ref = refs
        m0_of = acc0 = None
    else:
        q_ref, k_ref, v_ref, sink_ref, o_ref, qa_ref, acc_ref = refs
        gate_ref = None
        m0_of = lambda ch: sink_ref[:, ch]
        acc0 = jnp.where(lax.broadcasted_iota(jnp.int32, (VROWS, 1), 0) == HEAD_DIM, 1.0, 0.0)
    qi = pl.program_id(2)
    start = qi * tq
    jd = start // TK
    _load_queries(q_ref, qa_ref, r, tq)

    def step(j0, checks):
        acc_ref[...] = _window_attend(lambda ch: qa_ref[:, ch], k_ref, v_ref, j0, nt, checks, window=window,
                                      start=start, tq=tq, r=r, m0_of=m0_of, acc0=acc0)

    nt, roles = _window_tiles(window, tq)

    @pl.when(jd >= nt - 1)
    def _():
        step(jd - (nt - 1), roles)

    @pl.when(jd < nt - 1)
    def _():
        step(0, ((True, True),) * nt)

    _store_heads(o_ref, acc_ref[...], gate_ref, r, tq)


def _attn_call(mode, qt, k, vt, batch, seq, g, gates=None, branch=None, mb=None, emat=None, sinks=None,
               window=None, kw=None, vwt=None):
    r = N_HEADS // g
    nb = seq // SEL_BLOCK
    mr = emat.shape[1] if mode == "sel" else 0
    nkt = seq // TK
    kc = LANES + mr
    tq = TQ if mode == "sel" else TQ_WINDOW
    k_spec = pl.BlockSpec((None, seq, LANES), lambda b, gg, q: (b, 0, gg))
    v_spec = pl.BlockSpec((None, nkt, VROWS, TK), lambda b, gg, q: (b, 0, gg, 0))
    in_specs = [pl.BlockSpec((None, r * HEAD_DIM, tq), lambda b, gg, q: (b, gg, q)), k_spec, v_spec]
    args = [qt, k, vt]
    if mode in ("sel", "win"):
        in_specs.append(pl.BlockSpec((None, None, None, r, tq), lambda b, gg, q: (b, branch, gg, 0, q)))
        args.append(gates)
    if mode == "sel":
        in_specs += [pl.BlockSpec((None, None, nb, tq), lambda b, gg, q: (b, gg, 0, q)), _const_spec(emat.shape),
                     k_spec, v_spec,
                     pl.BlockSpec((None, None, None, r, tq), lambda b, gg, q: (b, branch + 1, gg, 0, q))]
        args += [mb, emat, kw, vwt, gates]
    if mode == "swa":
        in_specs.append(pl.BlockSpec((None, 1, r * tq), lambda b, gg, q: (gg, 0, 0)))
        args.append(sinks)
    lw = r * tq
    scratch = [pltpu.VMEM((kc, lw), BF16), pltpu.VMEM((VROWS, lw), F32)]
    if mode == "sel":
        u = min(SEL_GROUP, mr * SEL_BLOCK // TK)
        scratch[0] = pltpu.VMEM((nb // mr, kc, lw), BF16)
        scratch += ([pltpu.VMEM((VROWS, lw), F32)] + [pltpu.VMEM((1, lw), F32)] * 2
                    + [pltpu.VMEM((u, TK, lw), F32)] * 2
                    + [pltpu.VMEM((u, TK, lw), BF16)] * 2 + [pltpu.VMEM((u, 1, lw), F32)] * 4)
        body = functools.partial(_sel_body, tq=tq, r=r, mr=mr, nb=nb, window=window)
    else:
        body = functools.partial(_attn_body, mode=mode, tq=tq, r=r, window=window)
    return pl.pallas_call(
        body,
        grid=(batch, g, seq // tq),
        in_specs=in_specs,
        out_specs=pl.BlockSpec((None, tq, r * HEAD_DIM), lambda b, gg, q: (b, q, gg)),
        out_shape=jax.ShapeDtypeStruct((batch, seq, N_HEADS * HEAD_DIM), F32),
        scratch_shapes=scratch,
        compiler_params=_cparams(("parallel", "parallel", "arbitrary")),
        name=mode + "_attn",
    )(*args)


def _ffn_body(*refs, n_o, final, tm, ns, nchunk):
    it = iter(refs)
    h_ref = next(it)
    o_refs = [next(it) for _ in range(n_o)]
    wo_ref, g_ref, wa_ref, wg_ref, cw_ref, wout_ref = (next(it) for _ in range(6))
    gf_ref = next(it) if final else None
    out_ref, hn_ref, y_ref, prev_ref = (next(it) for _ in range(4))
    u_refs = [next(it) for _ in range(3)]
    act_refs = [next(it) for _ in range(3)]

    osum = o_refs[0][...]
    for o_ref in o_refs[1:]:
        osum = osum + o_ref[...]
    h = h_ref[...] + jnp.dot(osum.astype(BF16), wo_ref[...], preferred_element_type=F32)
    hn_ref[...] = _rms(h, g_ref[...]).astype(BF16)
    y_ref[...] = h

    @pl.when(pl.program_id(0) % ns == 0)
    def _():
        prev_ref[...] = jnp.zeros_like(prev_ref)

    rid = lax.broadcasted_iota(jnp.int32, (8, FF_CHUNK), 0)

    def conv(u, p8, w):
        u1 = pltpu.roll(u, 1, axis=0)
        u2 = pltpu.roll(u, 2, axis=0)
        f1 = jnp.where(rid < 1, pltpu.roll(p8, 1, axis=0), u1[0:8])
        f2 = jnp.where(rid < 2, pltpu.roll(p8, 2, axis=0), u2[0:8])
        u1 = jnp.concatenate([f1, u1[8:]], axis=0)
        u2 = jnp.concatenate([f2, u2[8:]], axis=0)
        return w[3:4] + w[0:1] * u2 + w[1:2] * u1 + w[2:3] * u

    def proj_in(c, slot):
        hn = hn_ref[...]
        u_refs[slot][0] = jnp.dot(hn, wa_ref[c], preferred_element_type=F32)
        u_refs[slot][1] = jnp.dot(hn, wg_ref[c], preferred_element_type=F32)

    def gate(c, slot):
        ua, ug = u_refs[slot][0], u_refs[slot][1]
        cw = cw_ref[c]
        pa, pg = prev_ref[c, 0:8], prev_ref[c, 8:16]
        prev_ref[c, 0:8] = ua[tm - 8:tm]
        prev_ref[c, 8:16] = ug[tm - 8:tm]
        ca = conv(ua, pa, cw[0:4])
        cg = conv(ug, pg, cw[4:8])
        act_refs[slot][...] = (ca * jax.nn.sigmoid(ca) * cg).astype(BF16)

    def proj_out(c, slot):
        y_ref[...] += jnp.dot(act_refs[slot][...], wout_ref[c], preferred_element_type=F32)

    proj_in(0, 0)
    for c in range(nchunk):
        if c + 1 < nchunk:
            proj_in(c + 1, (c + 1) % 3)
        gate(c, c % 3)
        if c >= 1:
            proj_out(c - 1, (c - 1) % 3)
    proj_out(nchunk - 1, (nchunk - 1) % 3)
    out = y_ref[...]
    if final:
        out = _rms(out, gf_ref[...])
    out_ref[...] = out


def _ffn_call(h, o_list, wo, gain, wa, wg, cw, wout, seq, final_gain=None):
    t_tokens, d = h.shape
    tm = min(TM, seq)
    ns = seq // tm
    nchunk = wa.shape[0]
    tile = pl.BlockSpec((tm, d), lambda i: (i, 0))
    in_specs = [tile] + [tile] * len(o_list) + [
        _const_spec(wo.shape), _const_spec((1, d)), _const_spec(wa.shape), _const_spec(wg.shape),
        _const_spec(cw.shape), _const_spec(wout.shape)]
    args = [h] + list(o_list) + [wo, gain.reshape(1, d), wa, wg, cw, wout]
    if final_gain is not None:
        in_specs.append(_const_spec((1, d)))
        args.append(final_gain.reshape(1, d))
    return pl.pallas_call(
        functools.partial(_ffn_body, n_o=len(o_list), final=final_gain is not None, tm=tm, ns=ns, nchunk=nchunk),
        grid=(t_tokens // tm,),
        in_specs=in_specs, out_specs=tile,
        out_shape=jax.ShapeDtypeStruct((t_tokens, d), F32),
        scratch_shapes=[pltpu.VMEM((tm, d), BF16), pltpu.VMEM((tm, d), F32),
                        pltpu.VMEM((nchunk, 16, FF_CHUNK), F32)]
        + [pltpu.VMEM((2, tm, FF_CHUNK), F32)] * 3 + [pltpu.VMEM((tm, FF_CHUNK), BF16)] * 3,
        compiler_params=_cparams(("arbitrary",)),
        name="attn_out_ffn",
    )(*args)


def _pad_heads(w, g, width):
    d = w.shape[0]
    w3 = w.reshape(d, g, HEAD_DIM)
    return jnp.pad(w3, ((0, 0), (0, 0), (0, width - HEAD_DIM))).reshape(d, g * width)


def _rot_heads(w, g):
    d = w.shape[0]
    w3 = w.reshape(d, g, HEAD_DIM)
    return jnp.concatenate([-w3[..., HALF:], w3[..., :HALF]], axis=-1).reshape(d, g * HEAD_DIM)


def _rope_k_weights(w, g):
    return jnp.concatenate([_pad_heads(w, g, LANES), _pad_heads(_rot_heads(w, g), g, LANES)], axis=1)


def _rope_tables(seq):
    inv = jnp.float32(ROPE_THETA) ** (-jnp.arange(HALF, dtype=F32) / HALF)

    def cs(pos):
        ang = pos.astype(F32)[:, None] * inv[None, :]
        return jnp.cos(ang), jnp.sin(ang)

    def nat(c):
        return jnp.concatenate([c, c, jnp.zeros((c.shape[0], LANES - HEAD_DIM), F32)], axis=1)

    cos, sin = cs(jnp.arange(seq))
    qscale = HEAD_DIM ** -0.5 * LOG2E
    cc, sc = cs(jnp.arange(seq // CMP_STRIDE) * CMP_STRIDE + CMP_BLOCK - 1)
    return {"cn": nat(cos), "sn": nat(sin), "ct": (cos * qscale).T, "st": (sin * qscale).T,
            "cc": nat(cc), "sc": nat(sc)}


def _ffn_weights(w_in, conv_w, conv_b, w_out):
    d, two_ff = w_in.shape
    dff = two_ff // 2
    assert conv_w.shape[0] == CONV_WIDTH and dff % FF_CHUNK == 0
    nchunk = dff // FF_CHUNK

    def chunks(w):
        return w.reshape(d, nchunk, FF_CHUNK).transpose(1, 0, 2).astype(BF16)

    wa, wg = chunks(w_in[:, :dff]), chunks(w_in[:, dff:])
    taps = jnp.concatenate([conv_w, conv_b[None, :]], axis=0)
    cw = jnp.concatenate([taps[:, :dff].reshape(4, nchunk, FF_CHUNK), taps[:, dff:].reshape(4, nchunk, FF_CHUNK)],
                         axis=0).transpose(1, 0, 2)
    return wa, wg, cw, w_out.reshape(nchunk, FF_CHUNK, d).astype(BF16)


def _nsa_attention(h, gain, w_in, cmp_pos, cmp_w1, cmp_w2, tabs, emat, batch, seq):
    g, r = A_KV_HEADS, N_HEADS // A_KV_HEADS
    d = h.shape[1]
    kvw = g * HEAD_DIM
    nq = N_HEADS * HEAD_DIM
    wq, wkc, wvc, wks, wvs, wkw, wvw, wgl = jnp.split(
        w_in, [nq, nq + kvw, nq + 2 * kvw, nq + 3 * kvw, nq + 4 * kvw, nq + 5 * kvw, nq + 6 * kvw], axis=1)
    wn = jnp.concatenate([_rope_k_weights(wks, g), _rope_k_weights(wkw, g), wkc, wvc], axis=1).astype(BF16)
    wgl = wgl.reshape(d, N_HEADS, 3).transpose(0, 2, 1).reshape(d, 3 * N_HEADS)
    wt = jnp.concatenate([wq, wgl, _pad_heads(wvs, g, VROWS), _pad_heads(wvw, g, VROWS)], axis=1).T.astype(BF16)
    kpad = g * LANES
    nat_plan = [("rope", 0, kpad), ("rope", 2 * kpad, kpad), ("heads", 4 * kpad, kvw), ("heads", 4 * kpad + kvw, kvw)]
    ng = 3 * N_HEADS
    tr_plan = [("ropeq", 0, nq), ("sigmoid", nq, ng), ("vaug", nq + ng, g * VROWS),
               ("vaug", nq + ng + g * VROWS, g * VROWS)]
    ks, kw, kc, vc, qt, gates, vst, vwt = _proj_call(
        h, gain, tabs, wn, wt, nat_plan, tr_plan, [BF16, BF16, F32, F32], batch, seq)
    ks = ks.reshape(batch, seq, kpad)
    kw = kw.reshape(batch, seq, kpad)
    gates = gates.reshape(batch, 3, g, r, seq)

    nseg = seq // CMP_STRIDE
    nb = seq // SEL_BLOCK
    seg_f = CMP_STRIDE * HEAD_DIM

    def segs(x):
        return x.reshape(batch, g, nseg, seg_f)

    pos = cmp_pos.reshape(2, 2, 1, seg_f)
    w1 = cmp_w1.reshape(2, 2, seg_f, cmp_w1.shape[-1]).astype(BF16)
    w2k = jnp.stack([_pad_heads(cmp_w2[0], 1, LANES), _pad_heads(_rot_heads(cmp_w2[0], 1), 1, LANES)]).astype(BF16)
    w2v = _pad_heads(cmp_w2[1], 1, VROWS).T.astype(BF16)
    kcc, vcc = _compress_call(segs(kc), segs(vc), pos, w1, w2k, w2v, tabs["cc"], tabs["sc"])
    kcc = kcc.reshape(batch, g, nb, 4, LANES).transpose(0, 1, 3, 2, 4).reshape(batch, g, nseg, LANES)
    vcc = vcc.reshape(batch, g, VROWS, nb, 4).transpose(0, 1, 2, 4, 3).reshape(batch, g, VROWS, nseg)

    o_c, mb = _cmp_call(qt, kcc, vcc, gates, batch, seq)
    o_sw = _attn_call("sel", qt, ks, vst, batch, seq, g, gates=gates, branch=1, mb=mb, emat=emat,
                      kw=kw, vwt=vwt, window=WINDOW_A)
    t_tokens = batch * seq
    return [o.reshape(t_tokens, nq) for o in (o_c, o_sw)]


def kernel(x, norm_attn, norm_ffn, a_w_in, a_cmp_pos, a_cmp_w1, a_cmp_w2, a_w_out, kv_norm, b_w_kv, b_w_q, b_sinks,
           b_w_out, ffn_w_in, ffn_conv_w, ffn_conv_b, ffn_w_out, final_norm):
    batch, seq, d = x.shape
    depth = norm_attn.shape[0]
    n_a = a_w_in.shape[0]
    tabs = _rope_tables(seq)
    mr = min(MASK_ROWS, seq // SEL_BLOCK)
    emat = (jnp.arange(mr * SEL_BLOCK)[:, None] // SEL_BLOCK == jnp.arange(mr)[None, :]).astype(BF16)
    h = x.reshape(batch * seq, d)
    k_sh = v_sh = None
    gb, rb = B_KV_HEADS, N_HEADS // B_KV_HEADS
    for layer in range(depth):
        if layer < n_a:
            o_list = _nsa_attention(h, norm_attn[layer], a_w_in[layer], a_cmp_pos[layer], a_cmp_w1[layer],
                                    a_cmp_w2[layer], tabs, emat, batch, seq)
            wo = a_w_out[layer]
        else:
            j = layer - n_a
            if k_sh is None:
                wk, wv = jnp.split(b_w_kv, 2, axis=1)
                kpad = gb * LANES
                k_sh, v_sh = _proj_call(
                    h, kv_norm, tabs, _rope_k_weights(wk, gb).astype(BF16), _pad_heads(wv, gb, VROWS).T.astype(BF16),
                    [("rope", 0, kpad)], [("vaug", 0, gb * VROWS)], [BF16], batch, seq)
                k_sh = k_sh.reshape(batch, seq, kpad)
            (qt,) = _proj_call(h, norm_attn[layer], tabs, None, b_w_q[j].T.astype(BF16), [],
                               [("ropeq", 0, N_HEADS * HEAD_DIM)], [], batch, seq)
            sinks = jnp.broadcast_to((b_sinks[j] * LOG2E).reshape(gb, 1, rb, 1),
                                     (gb, 1, rb, TQ_WINDOW)).reshape(gb, 1, rb * TQ_WINDOW)
            o = _attn_call("swa", qt, k_sh, v_sh, batch, seq, gb, sinks=sinks, window=WINDOW_B)
            o_list = [o.reshape(batch * seq, N_HEADS * HEAD_DIM)]
            wo = b_w_out[j]
        wa, wg, cw, wout = _ffn_weights(ffn_w_in[layer], ffn_conv_w[layer], ffn_conv_b[layer], ffn_w_out[layer])
        h = _ffn_call(h, o_list, wo.astype(BF16), norm_ffn[layer], wa, wg, cw, wout, seq,
                      final_gain=final_norm if layer == depth - 1 else None)
    return h.reshape(batch, seq, d)
```

```python
import functools
import math

import jax
import jax.numpy as jnp
from jax import lax
from jax.experimental import pallas as pl
from jax.experimental.pallas import tpu as pltpu

F32 = jnp.float32
BF16 = jnp.bfloat16

HEAD_DIM = 64
HALF = HEAD_DIM // 2
N_HEADS = 16
A_KV_HEADS = 4
B_KV_HEADS = 2
CMP_BLOCK = 32
CMP_STRIDE = 16
SEL_BLOCK = 64
N_SELECT = 16
WINDOW_A = 512
WINDOW_B = 128
CONV_WIDTH = 3
ROPE_THETA = 10000.0
EPS = 1e-6
FORCE = 1e6

LANES = 128
BF16_SUBLANES = 16
VMEM_LIMIT = 56 * 1024 * 1024
TQ = 256
TQ_WINDOW = 256
TK = 256
SEL_GROUP = 4
CMP_CHUNK = 64
TM = 512
FF_CHUNK = 256
VROWS = HEAD_DIM + BF16_SUBLANES
MASK_ROWS = 128

LOG2E = 1.4426950408889634
NEG = -1e30
M_INIT = -1e29
REMOVED = -3e38

_NT = (((1,), (1,)), ((), ()))


def _cparams(sem):
    return pltpu.CompilerParams(dimension_semantics=sem, vmem_limit_bytes=VMEM_LIMIT)


def _const_spec(shape):
    n = len(shape)
    return pl.BlockSpec(shape, lambda *_: (0,) * n, pipeline_mode=pl.Buffered(1))


def _rms(x, g):
    ms = jnp.mean(x * x, axis=-1, keepdims=True)
    return x * lax.rsqrt(ms + EPS) * g


def _proj_body(*refs, nat_plan, tr_plan, tm):
    it = iter(refs)
    h_ref, g_ref = next(it), next(it)
    if nat_plan:
        cn_ref, sn_ref, wn_ref = next(it), next(it), next(it)
    ct_ref, st_ref, wt_ref = next(it), next(it), next(it)
    outs = list(it)

    hn = _rms(h_ref[...], g_ref[...]).astype(BF16)
    oi = 0
    for kind, c0, n in nat_plan:
        o_ref = outs[oi]
        oi += 1
        y = jnp.dot(hn, wn_ref[:, c0:c0 + n], preferred_element_type=F32)
        if kind == "rope":
            yr = jnp.dot(hn, wn_ref[:, c0 + n:c0 + 2 * n], preferred_element_type=F32)
            c, s = cn_ref[...], sn_ref[...]
            for g in range(n // LANES):
                sl = slice(LANES * g, LANES * (g + 1))
                o_ref[:, sl] = (y[:, sl] * c + yr[:, sl] * s).astype(o_ref.dtype)
        elif kind == "heads":
            for g in range(n // HEAD_DIM):
                o_ref[g] = y[:, HEAD_DIM * g:HEAD_DIM * (g + 1)].astype(o_ref.dtype)
        else:
            o_ref[...] = y.astype(o_ref.dtype)
    for kind, r0, n in tr_plan:
        o_ref = outs[oi]
        oi += 1
        y = lax.dot_general(wt_ref[r0:r0 + n, :], hn, _NT, preferred_element_type=F32)
        if kind == "ropeq":
            c, s = ct_ref[...], st_ref[...]
            for hd in range(n // HEAD_DIM):
                a = HEAD_DIM * hd
                y1, y2 = y[a:a + HALF], y[a + HALF:a + HEAD_DIM]
                o_ref[a:a + HALF, :] = (y1 * c - y2 * s).astype(o_ref.dtype)
                o_ref[a + HALF:a + HEAD_DIM, :] = (y2 * c + y1 * s).astype(o_ref.dtype)
        elif kind == "sigmoid":
            o_ref[...] = jax.nn.sigmoid(y)
        else:
            rows = lax.broadcasted_iota(jnp.int32, (VROWS, tm), 0)
            for g in range(n // VROWS):
                yg = jnp.where(rows == HEAD_DIM, 1.0, y[VROWS * g:VROWS * (g + 1)]).astype(o_ref.dtype)
                for t in range(tm // TK):
                    o_ref[t, VROWS * g:VROWS * (g + 1), :] = yg[:, t * TK:(t + 1) * TK]


def _proj_call(h, gain, tabs, wn, wt, nat_plan, tr_plan, nat_dtypes, batch, seq):
    t_tokens, d = h.shape
    tm = min(TM, seq)
    ns = seq // tm
    in_specs = [pl.BlockSpec((tm, d), lambda i: (i, 0)), _const_spec((1, d))]
    args = [h, gain.reshape(1, d)]
    if nat_plan:
        in_specs += [pl.BlockSpec((tm, LANES), lambda i: (i % ns, 0)),
                     pl.BlockSpec((tm, LANES), lambda i: (i % ns, 0)),
                     _const_spec(wn.shape)]
        args += [tabs["cn"], tabs["sn"], wn]
    in_specs += [pl.BlockSpec((HALF, tm), lambda i: (0, i % ns)),
                 pl.BlockSpec((HALF, tm), lambda i: (0, i % ns)),
                 _const_spec(wt.shape)]
    args += [tabs["ct"], tabs["st"], wt]
    out_shape, out_specs = [], []
    for (kind, _, n), dt in zip(nat_plan, nat_dtypes):
        if kind == "heads":
            nh = n // HEAD_DIM
            out_shape.append(jax.ShapeDtypeStruct((batch, nh, seq, HEAD_DIM), dt))
            out_specs.append(pl.BlockSpec((None, nh, tm, HEAD_DIM), lambda i: (i // ns, 0, i % ns, 0)))
        else:
            out_shape.append(jax.ShapeDtypeStruct((t_tokens, n), dt))
            out_specs.append(pl.BlockSpec((tm, n), lambda i: (i, 0)))
    for kind, _, n in tr_plan:
        if kind == "vaug":
            out_shape.append(jax.ShapeDtypeStruct((batch, seq // TK, n, TK), BF16))
            out_specs.append(pl.BlockSpec((None, tm // TK, n, TK), lambda i: (i // ns, i % ns, 0, 0)))
        else:
            dt = F32 if kind == "sigmoid" else BF16
            out_shape.append(jax.ShapeDtypeStruct((batch, n, seq), dt))
            out_specs.append(pl.BlockSpec((None, n, tm), lambda i: (i // ns, 0, i % ns)))
    return pl.pallas_call(
        functools.partial(_proj_body, nat_plan=tuple(nat_plan), tr_plan=tuple(tr_plan), tm=tm),
        grid=(t_tokens // tm,),
        in_specs=in_specs, out_specs=out_specs, out_shape=out_shape,
        compiler_params=_cparams(("parallel",)),
        name="norm_proj",
    )(*args)


def _gelu_tanh(x):
    c = math.sqrt(2.0 / math.pi)
    return x * (0.5 * (1.0 + jnp.tanh(c * (x + 0.044715 * (x * x * x)))))


def _compress_body(sk_ref, sv_ref, pos_ref, w1_ref, w2k_ref, w2v_ref, cc_ref, sc_ref, ok_ref, ov_ref, *, ncp):
    def hidden(seg, kv):
        xa = (seg + pos_ref[kv, 0]).astype(BF16)
        xb = (seg + pos_ref[kv, 1]).astype(BF16)
        a = jnp.dot(xa, w1_ref[kv, 0], preferred_element_type=F32)
        b = jnp.dot(xb, w1_ref[kv, 1], preferred_element_type=F32)
        return _gelu_tanh(a + pltpu.roll(b, ncp - 1, axis=0)).astype(BF16)

    gk = hidden(sk_ref[...], 0)
    k = jnp.dot(gk, w2k_ref[0], preferred_element_type=F32)
    kr = jnp.dot(gk, w2k_ref[1], preferred_element_type=F32)
    ok_ref[...] = (k * cc_ref[...] + kr * sc_ref[...]).astype(ok_ref.dtype)
    gv = hidden(sv_ref[...], 1)
    vt = lax.dot_general(w2v_ref[...], gv, _NT, preferred_element_type=F32)
    rows = lax.broadcasted_iota(jnp.int32, (VROWS, ncp), 0)
    ov_ref[...] = jnp.where(rows == HEAD_DIM, 1.0, vt).astype(ov_ref.dtype)


def _compress_call(segk, segv, pos, w1, w2k, w2v, cc, sc):
    b, g, ncp, f = segk.shape
    seg_spec = pl.BlockSpec((None, None, ncp, f), lambda i, j: (i, j, 0, 0))
    return pl.pallas_call(
        functools.partial(_compress_body, ncp=ncp),
        grid=(b, g),
        in_specs=[seg_spec, seg_spec, _const_spec(pos.shape), _const_spec(w1.shape), _const_spec(w2k.shape),
                  _const_spec(w2v.shape), _const_spec(cc.shape), _const_spec(sc.shape)],
        out_specs=[pl.BlockSpec((None, None, ncp, LANES), lambda i, j: (i, j, 0, 0)),
                   pl.BlockSpec((None, None, VROWS, ncp), lambda i, j: (i, j, 0, 0))],
        out_shape=[jax.ShapeDtypeStruct((b, g, ncp, LANES), BF16),
                   jax.ShapeDtypeStruct((b, g, VROWS, ncp), BF16)],
        compiler_params=_cparams(("parallel", "parallel")),
        name="compress",
    )(segk, segv, pos, w1, w2k, w2v, cc, sc)


def _load_queries(q_ref, qa_ref, r, tq):
    for rr in range(r):
        qa_ref[0:HEAD_DIM, rr * tq:(rr + 1) * tq] = q_ref[HEAD_DIM * rr:HEAD_DIM * (rr + 1), :]
    qa_ref[HEAD_DIM:LANES, :] = jnp.zeros((LANES - HEAD_DIM, r * tq), BF16)


def _gated_heads(acc, gate_ref, r, tq, colscale=None):
    inv = 1.0 / acc[HEAD_DIM:HEAD_DIM + 1, :]
    if colscale is not None:
        inv = inv * colscale
    o = acc[0:HEAD_DIM, :] * inv
    parts = []
    for rr in range(r):
        z = o[:, rr * tq:(rr + 1) * tq]
        if gate_ref is not None:
            z = z * gate_ref[rr:rr + 1, :]
        parts.append(z)
    return jnp.concatenate(parts, axis=0)


def _store_heads(o_ref, acc, gate_ref, r, tq, colscale=None):
    o_ref[...] = _gated_heads(acc, gate_ref, r, tq, colscale).T


def _window_tiles(window, tq):
    nt = max((s0 + tq - 1) // TK - (s0 - window + 1) // TK + 1 for s0 in range(8 * window, 8 * window + TK, tq))
    return nt, tuple((i == nt - 1, (nt - i) * TK - 1 >= window) for i in range(nt))


def _window_attend(qa_of, k_ref, v_ref, j0, u, checks, *, window, start, tq, r, m0_of=None, acc0=None,
                   between=None):
    row0 = pl.multiple_of(j0 * TK, TK)
    kt = k_ref[pl.ds(row0, u * TK), :]
    vt = jnp.concatenate([v_ref[j0 + i] for i in range(u)], axis=1)
    d0 = (start - row0) + (lax.broadcasted_iota(jnp.int32, (TK, tq), 1)
                           - lax.broadcasted_iota(jnp.int32, (TK, tq), 0))
    oks = []
    for i in range(u):
        lower, upper = checks[i]
        if lower and upper:
            oks.append((d0 >= i * TK) & (d0 < window + i * TK))
        elif lower:
            oks.append(d0 >= i * TK)
        elif upper:
            oks.append(d0 < window + i * TK)
        else:
            oks.append(None)
    cw = max(tq, 2 * LANES)
    chains = [slice(c0, c0 + cw) for c0 in range(0, r * tq, cw)]
    scores = [jnp.dot(kt, qa_of(ch), preferred_element_type=F32) for ch in chains]
    if between is not None:
        between()

    def masked(si, ok):
        if ok is None:
            return si
        return jnp.concatenate([jnp.where(ok, si[:, c0:c0 + tq], NEG) for c0 in range(0, cw, tq)], axis=1)

    outs = []
    for ch, s in zip(chains, scores):
        s = jnp.concatenate([masked(s[i * TK:(i + 1) * TK], oks[i]) for i in range(u)], axis=0)
        m = jnp.max(s, axis=0, keepdims=True)
        if m0_of is not None:
            m = jnp.maximum(m, m0_of(ch))
        acc = jnp.dot(vt, jnp.exp2(s - m).astype(BF16), preferred_element_type=F32)
        if acc0 is not None:
            acc = acc + acc0 * jnp.exp2(m0_of(ch) - m)
        outs.append(acc)
    return jnp.concatenate(outs, axis=1)


def _cmp_body(q_ref, k_ref, v_ref, gate_ref, o_ref, mb_ref, qa_ref, *, tq, nb, r):
    qi = pl.program_id(2)
    _load_queries(q_ref, qa_ref, r, tq)
    ch = min(CMP_CHUNK, nb)
    buckets = [b for b in range(ch, nb, ch) if b >= N_SELECT] + [nb]
    args = (q_ref, k_ref, v_ref, gate_ref, o_ref, mb_ref, qa_ref, qi)

    @pl.when(qi == 0)
    def _():
        _cmp_rows(*args, tq=tq, nb=nb, r=r, rows=buckets[0], forced_distinct=False)

    lo = 1
    for rows in buckets:
        hi = rows * SEL_BLOCK // tq if rows < nb else pl.num_programs(2)

        @pl.when((qi >= lo) & (qi < hi))
        def _(rows=rows):
            _cmp_rows(*args, tq=tq, nb=nb, r=r, rows=rows, forced_distinct=True)
        lo = hi


def _cmp_rows(q_ref, k_ref, v_ref, gate_ref, o_ref, mb_ref, qa_ref, qi, *, tq, nb, r, rows, forced_distinct):
    lw = r * tq
    ch = min(CMP_CHUNK, nb)
    s = jnp.dot(k_ref[0:4 * rows, :], qa_ref[...], preferred_element_type=F32)
    if rows < nb:
        mb_ref[rows:nb, :] = jnp.full((nb - rows, tq), NEG, mb_ref.dtype)
    nb = rows
    i_io = lax.broadcasted_iota(jnp.int32, (ch, tq), 0)
    lim = qi * tq + lax.broadcasted_iota(jnp.int32, (ch, tq), 1) - (CMP_BLOCK - 1)
    pieces = []
    for c in range(nb // ch):
        for m in range(4):
            valid = SEL_BLOCK * i_io <= lim - (SEL_BLOCK * c * ch + CMP_STRIDE * m)
            r0 = (4 * c + m) * ch
            pieces.append(jnp.concatenate(
                [jnp.where(valid, s[r0:r0 + ch, rr * tq:(rr + 1) * tq], NEG) for rr in range(r)], axis=1))
    mx = pieces[0]
    for x in pieces[1:]:
        mx = jnp.maximum(mx, x)
    mx = jnp.max(mx, axis=0, keepdims=True)
    pm = [jnp.exp2(x - mx) for x in pieces]
    tot = pm[0]
    for x in pm[1:]:
        tot = tot + x
    den = jnp.sum(tot, axis=0, keepdims=True)
    t_lane = qi * tq + (lax.broadcasted_iota(jnp.int32, (1, lw), 1) & (tq - 1))
    has_key = jnp.where(t_lane >= CMP_BLOCK - 1, 1.0, 0.0)
    acc = jnp.dot(v_ref[:, 0:4 * rows], jnp.concatenate(pm, axis=0).astype(BF16), preferred_element_type=F32)
    _store_heads(o_ref, acc, gate_ref, r, tq, has_key)

    inv = has_key / den
    ps = []
    for m in range(4):
        slabs = []
        for c in range(nb // ch):
            pn = pm[4 * c + m] * inv
            acc_h = pn[:, 0:tq]
            for rr in range(1, r):
                acc_h = acc_h + pn[:, rr * tq:(rr + 1) * tq]
            slabs.append(acc_h)
        ps.append(jnp.concatenate(slabs, axis=0))
    j_io = lax.broadcasted_iota(jnp.int32, (nb, tq), 0)
    tt = qi * tq + lax.broadcasted_iota(jnp.int32, (nb, tq), 1)
    prev3 = jnp.where(j_io == 0, 0.0, pltpu.roll(ps[3], 1, axis=0))
    imp = prev3 + 2.0 * (ps[0] + ps[1] + ps[2]) + ps[3]
    cur = tt >> (SEL_BLOCK.bit_length() - 1)
    forced = (j_io == 0) | (j_io == cur) | (j_io == cur - 1)
    causal = j_io * SEL_BLOCK <= tt
    v = jnp.where(forced, REMOVED if forced_distinct else FORCE, jnp.where(causal, imp, -FORCE))
    jf = j_io.astype(F32)
    for _ in range(min(N_SELECT, nb) - (3 if forced_distinct else 0)):
        top = jnp.max(v, axis=0, keepdims=True)
        idx = jnp.min(jnp.where(v == top, jf, float(nb)), axis=0, keepdims=True)
        v = jnp.where(jf == idx, REMOVED, v)
    mb_ref[0:nb, :] = jnp.where(v == REMOVED, 0.0, NEG).astype(mb_ref.dtype)


def _cmp_call(qt, kc, vct, gates, batch, seq):
    g, r = A_KV_HEADS, N_HEADS // A_KV_HEADS
    nb = seq // SEL_BLOCK
    ncp = 4 * nb
    tq = TQ_WINDOW
    return pl.pallas_call(
        functools.partial(_cmp_body, tq=tq, nb=nb, r=r),
        grid=(batch, g, seq // tq),
        in_specs=[pl.BlockSpec((None, r * HEAD_DIM, tq), lambda b, gg, q: (b, gg, q)),
                  pl.BlockSpec((None, None, ncp, LANES), lambda b, gg, q: (b, gg, 0, 0)),
                  pl.BlockSpec((None, None, VROWS, ncp), lambda b, gg, q: (b, gg, 0, 0)),
                  pl.BlockSpec((None, None, None, r, tq), lambda b, gg, q: (b, 0, gg, 0, q))],
        out_specs=[pl.BlockSpec((None, tq, r * HEAD_DIM), lambda b, gg, q: (b, q, gg)),
                   pl.BlockSpec((None, None, nb, tq), lambda b, gg, q: (b, gg, 0, q))],
        out_shape=[jax.ShapeDtypeStruct((batch, seq, N_HEADS * HEAD_DIM), F32),
                   jax.ShapeDtypeStruct((batch, g, nb, seq), BF16)],
        scratch_shapes=[pltpu.VMEM((LANES, r * tq), BF16)],
        compiler_params=_cparams(("parallel", "parallel", "parallel")),
        name="cmp_attn_topk",
    )(qt, kc, vct, gates)


def _sel_body(q_ref, k_ref, v_ref, gate_ref, mb_ref, e_ref, kw_ref, vw_ref, gatew_ref, o_ref,
              qa_ref, acc_ref, accw_ref, m_ref, macc_ref,
              s0_ref, s1_ref, p0_ref, p1_ref, mt0_ref, mt1_ref, cm0_ref, cm1_ref, *, tq, r, mr, nb, window):
    s_refs, p_refs, mt_refs, cm_refs = (s0_ref, s1_ref), (p0_ref, p1_ref), (mt0_ref, mt1_ref), (cm0_ref, cm1_ref)
    qi = pl.program_id(2)
    lw = r * tq
    start = qi * tq
    u = p0_ref.shape[0]
    gph = mr * SEL_BLOCK // (u * TK)
    assert tq == TK
    jd = start // TK
    gd = jd // u
    nh = nb // mr

    _load_queries(q_ref, qa_ref.at[0], r, tq)
    for hf in range(1, nh):
        qa_ref[hf, 0:LANES, :] = qa_ref[0, 0:LANES, :]
    for hf in range(nh):
        for rr in range(r):
            qa_ref[hf, LANES:LANES + mr, rr * tq:(rr + 1) * tq] = mb_ref[hf * mr:(hf + 1) * mr, :]
    acc_ref[...] = jnp.zeros((VROWS, lw), F32)

    def key_tile(gi, i):
        row0 = pl.multiple_of(gi * (u * TK), u * TK)
        eoff = pl.multiple_of(row0 % (mr * SEL_BLOCK), u * TK)
        return jnp.concatenate([k_ref[pl.ds(row0 + i * TK, TK), :], e_ref[pl.ds(eoff + i * TK, TK), :]], axis=1)

    def causal(gi, i):
        t_io = start + (lax.broadcasted_iota(jnp.int32, (TK, lw), 1) & (tq - 1))
        return (gi * u + i) * TK + lax.broadcasted_iota(jnp.int32, (TK, lw), 0) <= t_io

    def pipeline(stage_a, stage_b, stage_c):
        def trip(gi, slot):
            stage_a(gi + 1, 1 - slot)
            stage_b(gi, slot, None)
            stage_c(gi - 1, 1 - slot, u)

        def last(slot, has_prev):
            for kd in range(u):
                @pl.when(jd % u == kd)
                def _(kd=kd):
                    stage_b(gd, slot, kd)
                    if has_prev:
                        stage_c(gd - 1, 1 - slot, u)
                    stage_c(gd, slot, kd + 1)

        nt, _ = _window_tiles(window, tq)
        accw_ref[...] = _window_attend(lambda ch: qa_ref[0, 0:LANES, ch], kw_ref, vw_ref,
                                       jnp.maximum(start // TK - (nt - 1), 0), nt, ((True, True),) * nt,
                                       window=window, start=start, tq=tq, r=r, between=lambda: stage_a(0, 0))

        @pl.when(gd == 0)
        def _():
            last(0, False)

        @pl.when(gd > 0)
        def _():
            stage_a(1, 1)
            stage_b(0, 0, None)
            n = gd - 1

            def body(pi, c):
                trip(1 + 2 * pi, 1)
                trip(2 + 2 * pi, 0)
                return c
            lax.fori_loop(0, n // 2, body, 0)

            @pl.when(n % 2 == 1)
            def _():
                trip(gd - 1, 1)
                last(0, True)

            @pl.when(n % 2 == 0)
            def _():
                last(1, True)

    m_ref[...] = jnp.full((1, lw), M_INIT, F32)
    macc_ref[...] = jnp.full((1, lw), M_INIT, F32)

    def scores(gi, slot):
        qa = qa_ref[gi // gph]
        for i in range(u):
            s = jnp.dot(key_tile(gi, i), qa, preferred_element_type=F32)
            s_refs[slot][i] = s
            cm_refs[slot][i] = jnp.max(s, axis=0, keepdims=True)

    def softmax(gi, slot, diag):
        m = m_ref[...]
        for i in range(u if diag is None else diag + 1):
            if i == diag:
                s = jnp.where(causal(gi, i), s_refs[slot][i], NEG)
                m = jnp.maximum(m, jnp.max(s, axis=0, keepdims=True))
            else:
                m = jnp.maximum(m, cm_refs[slot][i])
                s = s_refs[slot][i]
            p_refs[slot][i] = jnp.exp2(s - m).astype(BF16)
            mt_refs[slot][i] = m
        m_ref[...] = m

    def values(gi, slot, ntiles):
        acc, ma = acc_ref[...], macc_ref[...]
        for i in range(ntiles):
            mt = mt_refs[slot][i]
            acc = acc * jnp.exp2(ma - mt) + jnp.dot(v_ref[gi * u + i], p_refs[slot][i],
                                                    preferred_element_type=F32)
            ma = mt
        acc_ref[...], macc_ref[...] = acc, ma

    pipeline(scores, softmax, values)
    o_ref[...] = (_gated_heads(acc_ref[...], gate_ref, r, tq) + _gated_heads(accw_ref[...], gatew_ref, r, tq)).T


def _attn_body(*refs, mode, tq, r, window):
    if mode == "win":
        q_ref, k_ref, v_ref, gate_ref, o_ref, qa_ref, acc_ref = refs
        m0_of = acc0 = None
    else:
        q_ref, k_ref, v_ref, sink_ref, o_ref, qa_ref, acc_ref = refs
        gate_ref = None
        m0_of = lambda ch: sink_ref[:, ch]
        acc0 = jnp.where(lax.broadcasted_iota(jnp.int32, (VROWS, 1), 0) == HEAD_DIM, 1.0, 0.0)
    qi = pl.program_id(2)
    start = qi * tq
    jd = start // TK
    _load_queries(q_ref, qa_ref, r, tq)

    def step(j0, checks):
        acc_ref[...] = _window_attend(lambda ch: qa_ref[:, ch], k_ref, v_ref, j0, nt, checks, window=window,
                                      start=start, tq=tq, r=r, m0_of=m0_of, acc0=acc0)

    nt, roles = _window_tiles(window, tq)

    @pl.when(jd >= nt - 1)
    def _():
        step(jd - (nt - 1), roles)

    @pl.when(jd < nt - 1)
    def _():
        step(0, ((True, True),) * nt)

    _store_heads(o_ref, acc_ref[...], gate_ref, r, tq)


def _attn_call(mode, qt, k, vt, batch, seq, g, gates=None, branch=None, mb=None, emat=None, sinks=None,
               window=None, kw=None, vwt=None):
    r = N_HEADS // g
    nb = seq // SEL_BLOCK
    mr = emat.shape[1] if mode == "sel" else 0
    nkt = seq // TK
    kc = LANES + mr
    tq = TQ if mode == "sel" else TQ_WINDOW
    k_spec = pl.BlockSpec((None, seq, LANES), lambda b, gg, q: (b, 0, gg))
    v_spec = pl.BlockSpec((None, nkt, VROWS, TK), lambda b, gg, q: (b, 0, gg, 0))
    in_specs = [pl.BlockSpec((None, r * HEAD_DIM, tq), lambda b, gg, q: (b, gg, q)), k_spec, v_spec]
    args = [qt, k, vt]
    if mode in ("sel", "win"):
        in_specs.append(pl.BlockSpec((None, None, None, r, tq), lambda b, gg, q: (b, branch, gg, 0, q)))
        args.append(gates)
    if mode == "sel":
        in_specs += [pl.BlockSpec((None, None, nb, tq), lambda b, gg, q: (b, gg, 0, q)), _const_spec(emat.shape),
                     k_spec, v_spec,
                     pl.BlockSpec((None, None, None, r, tq), lambda b, gg, q: (b, branch + 1, gg, 0, q))]
        args += [mb, emat, kw, vwt, gates]
    if mode == "swa":
        in_specs.append(pl.BlockSpec((None, 1, r * tq), lambda b, gg, q: (gg, 0, 0)))
        args.append(sinks)
    lw = r * tq
    scratch = [pltpu.VMEM((kc, lw), BF16), pltpu.VMEM((VROWS, lw), F32)]
    if mode == "sel":
        u = min(SEL_GROUP, mr * SEL_BLOCK // TK)
        scratch[0] = pltpu.VMEM((nb // mr, kc, lw), BF16)
        scratch += ([pltpu.VMEM((VROWS, lw), F32)] + [pltpu.VMEM((1, lw), F32)] * 2
                    + [pltpu.VMEM((u, TK, lw), F32)] * 2
                    + [pltpu.VMEM((u, TK, lw), BF16)] * 2 + [pltpu.VMEM((u, 1, lw), F32)] * 4)
        body = functools.partial(_sel_body, tq=tq, r=r, mr=mr, nb=nb, window=window)
    else:
        body = functools.partial(_attn_body, mode=mode, tq=tq, r=r, window=window)
    return pl.pallas_call(
        body,
        grid=(batch, g, seq // tq),
        in_specs=in_specs,
        out_specs=pl.BlockSpec((None, tq, r * HEAD_DIM), lambda b, gg, q: (b, q, gg)),
        out_shape=jax.ShapeDtypeStruct((batch, seq, N_HEADS * HEAD_DIM), F32),
        scratch_shapes=scratch,
        compiler_params=_cparams(("parallel", "parallel", "arbitrary")),
        name=mode + "_attn",
    )(*args)


def _ffn_body(*refs, n_o, final, tm, ns, nchunk):
    it = iter(refs)
    h_ref = next(it)
    o_refs = [next(it) for _ in range(n_o)]
    wo_ref, g_ref, wa_ref, wg_ref, cw_ref, wout_ref = (next(it) for _ in range(6))
    gf_ref = next(it) if final else None
    out_ref, hn_ref, y_ref, prev_ref = (next(it) for _ in range(4))
    u_refs = [next(it) for _ in range(3)]
    act_refs = [next(it) for _ in range(3)]

    osum = o_refs[0][...]
    for o_ref in o_refs[1:]:
        osum = osum + o_ref[...]
    h = h_ref[...] + jnp.dot(osum.astype(BF16), wo_ref[...], preferred_element_type=F32)
    hn_ref[...] = _rms(h, g_ref[...]).astype(BF16)
    y_ref[...] = h

    @pl.when(pl.program_id(0) % ns == 0)
    def _():
        prev_ref[...] = jnp.zeros_like(prev_ref)

    rid = lax.broadcasted_iota(jnp.int32, (8, FF_CHUNK), 0)

    def conv(u, p8, w):
        u1 = pltpu.roll(u, 1, axis=0)
        u2 = pltpu.roll(u, 2, axis=0)
        f1 = jnp.where(rid < 1, pltpu.roll(p8, 1, axis=0), u1[0:8])
        f2 = jnp.where(rid < 2, pltpu.roll(p8, 2, axis=0), u2[0:8])
        u1 = jnp.concatenate([f1, u1[8:]], axis=0)
        u2 = jnp.concatenate([f2, u2[8:]], axis=0)
        return w[3:4] + w[0:1] * u2 + w[1:2] * u1 + w[2:3] * u

    def proj_in(c, slot):
        hn = hn_ref[...]
        u_refs[slot][0] = jnp.dot(hn, wa_ref[c], preferred_element_type=F32)
        u_refs[slot][1] = jnp.dot(hn, wg_ref[c], preferred_element_type=F32)

    def gate(c, slot):
        ua, ug = u_refs[slot][0], u_refs[slot][1]
        cw = cw_ref[c]
        pa, pg = prev_ref[c, 0:8], prev_ref[c, 8:16]
        prev_ref[c, 0:8] = ua[tm - 8:tm]
        prev_ref[c, 8:16] = ug[tm - 8:tm]
        ca = conv(ua, pa, cw[0:4])
        cg = conv(ug, pg, cw[4:8])
        act_refs[slot][...] = (ca * jax.nn.sigmoid(ca) * cg).astype(BF16)

    def proj_out(c, slot):
        y_ref[...] += jnp.dot(act_refs[slot][...], wout_ref[c], preferred_element_type=F32)

    proj_in(0, 0)
    for c in range(nchunk):
        if c + 1 < nchunk:
            proj_in(c + 1, (c + 1) % 3)
        gate(c, c % 3)
        if c >= 1:
            proj_out(c - 1, (c - 1) % 3)
    proj_out(nchunk - 1, (nchunk - 1) % 3)
    out = y_ref[...]
    if final:
        out = _rms(out, gf_ref[...])
    out_ref[...] = out


def _ffn_call(h, o_list, wo, gain, wa, wg, cw, wout, seq, final_gain=None):
    t_tokens, d = h.shape
    tm = min(TM, seq)
    ns = seq // tm
    nchunk = wa.shape[0]
    tile = pl.BlockSpec((tm, d), lambda i: (i, 0))
    in_specs = [tile] + [tile] * len(o_list) + [
        _const_spec(wo.shape), _const_spec((1, d)), _const_spec(wa.shape), _const_spec(wg.shape),
        _const_spec(cw.shape), _const_spec(wout.shape)]
    args = [h] + list(o_list) + [wo, gain.reshape(1, d), wa, wg, cw, wout]
    if final_gain is not None:
        in_specs.append(_const_spec((1, d)))
        args.append(final_gain.reshape(1, d))
    return pl.pallas_call(
        functools.partial(_ffn_body, n_o=len(o_list), final=final_gain is not None, tm=tm, ns=ns, nchunk=nchunk),
        grid=(t_tokens // tm,),
        in_specs=in_specs, out_specs=tile,
        out_shape=jax.ShapeDtypeStruct((t_tokens, d), F32),
        scratch_shapes=[pltpu.VMEM((tm, d), BF16), pltpu.VMEM((tm, d), F32),
                        pltpu.VMEM((nchunk, 16, FF_CHUNK), F32)]
        + [pltpu.VMEM((2, tm, FF_CHUNK), F32)] * 3 + [pltpu.VMEM((tm, FF_CHUNK), BF16)] * 3,
        compiler_params=_cparams(("arbitrary",)),
        name="attn_out_ffn",
    )(*args)


def _pad_heads(w, g, width):
    d = w.shape[0]
    w3 = w.reshape(d, g, HEAD_DIM)
    return jnp.pad(w3, ((0, 0), (0, 0), (0, width - HEAD_DIM))).reshape(d, g * width)


def _rot_heads(w, g):
    d = w.shape[0]
    w3 = w.reshape(d, g, HEAD_DIM)
    return jnp.concatenate([-w3[..., HALF:], w3[..., :HALF]], axis=-1).reshape(d, g * HEAD_DIM)


def _rope_k_weights(w, g):
    return jnp.concatenate([_pad_heads(w, g, LANES), _pad_heads(_rot_heads(w, g), g, LANES)], axis=1)


def _rope_tables(seq):
    inv = jnp.float32(ROPE_THETA) ** (-jnp.arange(HALF, dtype=F32) / HALF)

    def cs(pos):
        ang = pos.astype(F32)[:, None] * inv[None, :]
        return jnp.cos(ang), jnp.sin(ang)

    def nat(c):
        return jnp.concatenate([c, c, jnp.zeros((c.shape[0], LANES - HEAD_DIM), F32)], axis=1)

    cos, sin = cs(jnp.arange(seq))
    qscale = HEAD_DIM ** -0.5 * LOG2E
    cc, sc = cs(jnp.arange(seq // CMP_STRIDE) * CMP_STRIDE + CMP_BLOCK - 1)
    return {"cn": nat(cos), "sn": nat(sin), "ct": (cos * qscale).T, "st": (sin * qscale).T,
            "cc": nat(cc), "sc": nat(sc)}


def _ffn_weights(w_in, conv_w, conv_b, w_out):
    d, two_ff = w_in.shape
    dff = two_ff // 2
    assert conv_w.shape[0] == CONV_WIDTH and dff % FF_CHUNK == 0
    nchunk = dff // FF_CHUNK

    def chunks(w):
        return w.reshape(d, nchunk, FF_CHUNK).transpose(1, 0, 2).astype(BF16)

    wa, wg = chunks(w_in[:, :dff]), chunks(w_in[:, dff:])
    taps = jnp.concatenate([conv_w, conv_b[None, :]], axis=0)
    cw = jnp.concatenate([taps[:, :dff].reshape(4, nchunk, FF_CHUNK), taps[:, dff:].reshape(4, nchunk, FF_CHUNK)],
                         axis=0).transpose(1, 0, 2)
    return wa, wg, cw, w_out.reshape(nchunk, FF_CHUNK, d).astype(BF16)


def _nsa_attention(h, gain, w_in, cmp_pos, cmp_w1, cmp_w2, tabs, emat, batch, seq):
    g, r = A_KV_HEADS, N_HEADS // A_KV_HEADS
    d = h.shape[1]
    kvw = g * HEAD_DIM
    nq = N_HEADS * HEAD_DIM
    wq, wkc, wvc, wks, wvs, wkw, wvw, wgl = jnp.split(
        w_in, [nq, nq + kvw, nq + 2 * kvw, nq + 3 * kvw, nq + 4 * kvw, nq + 5 * kvw, nq + 6 * kvw], axis=1)
    wn = jnp.concatenate([_rope_k_weights(wks, g), _rope_k_weights(wkw, g), wkc, wvc], axis=1).astype(BF16)
    wgl = wgl.reshape(d, N_HEADS, 3).transpose(0, 2, 1).reshape(d, 3 * N_HEADS)
    wt = jnp.concatenate([wq, wgl, _pad_heads(wvs, g, VROWS), _pad_heads(wvw, g, VROWS)], axis=1).T.astype(BF16)
    kpad = g * LANES
    nat_plan = [("rope", 0, kpad), ("rope", 2 * kpad, kpad), ("heads", 4 * kpad, kvw), ("heads", 4 * kpad + kvw, kvw)]
    ng = 3 * N_HEADS
    tr_plan = [("ropeq", 0, nq), ("sigmoid", nq, ng), ("vaug", nq + ng, g * VROWS),
               ("vaug", nq + ng + g * VROWS, g * VROWS)]
    ks, kw, kc, vc, qt, gates, vst, vwt = _proj_call(
        h, gain, tabs, wn, wt, nat_plan, tr_plan, [BF16, BF16, F32, F32], batch, seq)
    ks = ks.reshape(batch, seq, kpad)
    kw = kw.reshape(batch, seq, kpad)
    gates = gates.reshape(batch, 3, g, r, seq)

    nseg = seq // CMP_STRIDE
    nb = seq // SEL_BLOCK
    seg_f = CMP_STRIDE * HEAD_DIM

    def segs(x):
        return x.reshape(batch, g, nseg, seg_f)

    pos = cmp_pos.reshape(2, 2, 1, seg_f)
    w1 = cmp_w1.reshape(2, 2, seg_f, cmp_w1.shape[-1]).astype(BF16)
    w2k = jnp.stack([_pad_heads(cmp_w2[0], 1, LANES), _pad_heads(_rot_heads(cmp_w2[0], 1), 1, LANES)]).astype(BF16)
    w2v = _pad_heads(cmp_w2[1], 1, VROWS).T.astype(BF16)
    kcc, vcc = _compress_call(segs(kc), segs(vc), pos, w1, w2k, w2v, tabs["cc"], tabs["sc"])
    ch = min(CMP_CHUNK, nb)
    kcc = kcc.reshape(batch, g, nb // ch, ch, 4, LANES).transpose(0, 1, 2, 4, 3, 5).reshape(batch, g, nseg, LANES)
    vcc = vcc.reshape(batch, g, VROWS, nb // ch, ch, 4).transpose(0, 1, 2, 3, 5, 4).reshape(batch, g, VROWS, nseg)

    o_c, mb = _cmp_call(qt, kcc, vcc, gates, batch, seq)
    o_sw = _attn_call("sel", qt, ks, vst, batch, seq, g, gates=gates, branch=1, mb=mb, emat=emat,
                      kw=kw, vwt=vwt, window=WINDOW_A)
    t_tokens = batch * seq
    return [o.reshape(t_tokens, nq) for o in (o_c, o_sw)]


def kernel(x, norm_attn, norm_ffn, a_w_in, a_cmp_pos, a_cmp_w1, a_cmp_w2, a_w_out, kv_norm, b_w_kv, b_w_q, b_sinks,
           b_w_out, ffn_w_in, ffn_conv_w, ffn_conv_b, ffn_w_out, final_norm):
    batch, seq, d = x.shape
    depth = norm_attn.shape[0]
    n_a = a_w_in.shape[0]
    tabs = _rope_tables(seq)
    mr = min(MASK_ROWS, seq // SEL_BLOCK)
    emat = (jnp.arange(mr * SEL_BLOCK)[:, None] // SEL_BLOCK == jnp.arange(mr)[None, :]).astype(BF16)
    h = x.reshape(batch * seq, d)
    k_sh = v_sh = None
    gb, rb = B_KV_HEADS, N_HEADS // B_KV_HEADS
    for layer in range(depth):
        if layer < n_a:
            o_list = _nsa_attention(h, norm_attn[layer], a_w_in[layer], a_cmp_pos[layer], a_cmp_w1[layer],
                                    a_cmp_w2[layer], tabs, emat, batch, seq)
            wo = a_w_out[layer]
        else:
            j = layer - n_a
            if k_sh is None:
                wk, wv = jnp.split(b_w_kv, 2, axis=1)
                kpad = gb * LANES
                k_sh, v_sh = _proj_call(
                    h, kv_norm, tabs, _rope_k_weights(wk, gb).astype(BF16), _pad_heads(wv, gb, VROWS).T.astype(BF16),
                    [("rope", 0, kpad)], [("vaug", 0, gb * VROWS)], [BF16], batch, seq)
                k_sh = k_sh.reshape(batch, seq, kpad)
            (qt,) = _proj_call(h, norm_attn[layer], tabs, None, b_w_q[j].T.astype(BF16), [],
                               [("ropeq", 0, N_HEADS * HEAD_DIM)], [], batch, seq)
            sinks = jnp.broadcast_to((b_sinks[j] * LOG2E).reshape(gb, 1, rb, 1),
                                     (gb, 1, rb, TQ_WINDOW)).reshape(gb, 1, rb * TQ_WINDOW)
            o = _attn_call("swa", qt, k_sh, v_sh, batch, seq, gb, sinks=sinks, window=WINDOW_B)
            o_list = [o.reshape(batch * seq, N_HEADS * HEAD_DIM)]
            wo = b_w_out[j]
        wa, wg, cw, wout = _ffn_weights(ffn_w_in[layer], ffn_conv_w[layer], ffn_conv_b[layer], ffn_w_out[layer])
        h = _ffn_call(h, o_list, wo.astype(BF16), norm_ffn[layer], wa, wg, cw, wout, seq,
                      final_gain=final_norm if layer == depth - 1 else None)
    return h.reshape(batch, seq, d)
```

```python
import functools
import math

import jax
import jax.numpy as jnp
from jax import lax
from jax.experimental import pallas as pl
from jax.experimental.pallas import tpu as pltpu

F32 = jnp.float32
BF16 = jnp.bfloat16

HEAD_DIM = 64
HALF = HEAD_DIM // 2
N_HEADS = 16
A_KV_HEADS = 4
B_KV_HEADS = 2
CMP_BLOCK = 32
CMP_STRIDE = 16
SEL_BLOCK = 64
N_SELECT = 16
WINDOW_A = 512
WINDOW_B = 128
CONV_WIDTH = 3
ROPE_THETA = 10000.0
EPS = 1e-6
FORCE = 1e6

LANES = 128
BF16_SUBLANES = 16
VMEM_LIMIT = 56 * 1024 * 1024
TQ = 256
TQ_WINDOW = 256
TK = 256
SEL_GROUP = 4
CMP_CHUNK = 64
TM = 512
FF_CHUNK = 256
VROWS = HEAD_DIM + BF16_SUBLANES
MASK_ROWS = 128

LOG2E = 1.4426950408889634
NEG = -1e30
M_INIT = -1e29
REMOVED = -3e38

_NT = (((1,), (1,)), ((), ()))


def _cparams(sem):
    return pltpu.CompilerParams(dimension_semantics=sem, vmem_limit_bytes=VMEM_LIMIT)


def _const_spec(shape):
    n = len(shape)
    return pl.BlockSpec(shape, lambda *_: (0,) * n, pipeline_mode=pl.Buffered(1))


def _rms(x, g):
    ms = jnp.mean(x * x, axis=-1, keepdims=True)
    return x * lax.rsqrt(ms + EPS) * g


def _proj_body(*refs, nat_plan, tr_plan, tm):
    it = iter(refs)
    h_ref, g_ref = next(it), next(it)
    if nat_plan:
        cn_ref, sn_ref, wn_ref = next(it), next(it), next(it)
    ct_ref, st_ref, wt_ref = next(it), next(it), next(it)
    outs = list(it)

    hn = _rms(h_ref[...], g_ref[...]).astype(BF16)
    oi = 0
    for kind, c0, n in nat_plan:
        o_ref = outs[oi]
        oi += 1
        y = jnp.dot(hn, wn_ref[:, c0:c0 + n], preferred_element_type=F32)
        if kind == "rope":
            yr = jnp.dot(hn, wn_ref[:, c0 + n:c0 + 2 * n], preferred_element_type=F32)
            c, s = cn_ref[...], sn_ref[...]
            for g in range(n // LANES):
                sl = slice(LANES * g, LANES * (g + 1))
                o_ref[:, sl] = (y[:, sl] * c + yr[:, sl] * s).astype(o_ref.dtype)
        elif kind == "heads":
            for g in range(n // HEAD_DIM):
                o_ref[g] = y[:, HEAD_DIM * g:HEAD_DIM * (g + 1)].astype(o_ref.dtype)
        else:
            o_ref[...] = y.astype(o_ref.dtype)
    for kind, r0, n in tr_plan:
        o_ref = outs[oi]
        oi += 1
        y = lax.dot_general(wt_ref[r0:r0 + n, :], hn, _NT, preferred_element_type=F32)
        if kind == "ropeq":
            c, s = ct_ref[...], st_ref[...]
            for hd in range(n // HEAD_DIM):
                a = HEAD_DIM * hd
                y1, y2 = y[a:a + HALF], y[a + HALF:a + HEAD_DIM]
                o_ref[a:a + HALF, :] = (y1 * c - y2 * s).astype(o_ref.dtype)
                o_ref[a + HALF:a + HEAD_DIM, :] = (y2 * c + y1 * s).astype(o_ref.dtype)
        elif kind == "sigmoid":
            o_ref[...] = jax.nn.sigmoid(y)
        else:
            rows = lax.broadcasted_iota(jnp.int32, (VROWS, tm), 0)
            for g in range(n // VROWS):
                yg = jnp.where(rows == HEAD_DIM, 1.0, y[VROWS * g:VROWS * (g + 1)]).astype(o_ref.dtype)
                for t in range(tm // TK):
                    o_ref[t, VROWS * g:VROWS * (g + 1), :] = yg[:, t * TK:(t + 1) * TK]


def _proj_call(h, gain, tabs, wn, wt, nat_plan, tr_plan, nat_dtypes, batch, seq):
    t_tokens, d = h.shape
    tm = min(TM, seq)
    ns = seq // tm
    in_specs = [pl.BlockSpec((tm, d), lambda i: (i, 0)), _const_spec((1, d))]
    args = [h, gain.reshape(1, d)]
    if nat_plan:
        in_specs += [pl.BlockSpec((tm, LANES), lambda i: (i % ns, 0)),
                     pl.BlockSpec((tm, LANES), lambda i: (i % ns, 0)),
                     _const_spec(wn.shape)]
        args += [tabs["cn"], tabs["sn"], wn]
    in_specs += [pl.BlockSpec((HALF, tm), lambda i: (0, i % ns)),
                 pl.BlockSpec((HALF, tm), lambda i: (0, i % ns)),
                 _const_spec(wt.shape)]
    args += [tabs["ct"], tabs["st"], wt]
    out_shape, out_specs = [], []
    for (kind, _, n), dt in zip(nat_plan, nat_dtypes):
        if kind == "heads":
            nh = n // HEAD_DIM
            out_shape.append(jax.ShapeDtypeStruct((batch, nh, seq, HEAD_DIM), dt))
            out_specs.append(pl.BlockSpec((None, nh, tm, HEAD_DIM), lambda i: (i // ns, 0, i % ns, 0)))
        else:
            out_shape.append(jax.ShapeDtypeStruct((t_tokens, n), dt))
            out_specs.append(pl.BlockSpec((tm, n), lambda i: (i, 0)))
    for kind, _, n in tr_plan:
        if kind == "vaug":
            out_shape.append(jax.ShapeDtypeStruct((batch, seq // TK, n, TK), BF16))
            out_specs.append(pl.BlockSpec((None, tm // TK, n, TK), lambda i: (i // ns, i % ns, 0, 0)))
        else:
            dt = F32 if kind == "sigmoid" else BF16
            out_shape.append(jax.ShapeDtypeStruct((batch, n, seq), dt))
            out_specs.append(pl.BlockSpec((None, n, tm), lambda i: (i // ns, 0, i % ns)))
    return pl.pallas_call(
        functools.partial(_proj_body, nat_plan=tuple(nat_plan), tr_plan=tuple(tr_plan), tm=tm),
        grid=(t_tokens // tm,),
        in_specs=in_specs, out_specs=out_specs, out_shape=out_shape,
        compiler_params=_cparams(("parallel",)),
        name="norm_proj",
    )(*args)


def _gelu_tanh(x):
    c = math.sqrt(2.0 / math.pi)
    return x * (0.5 * (1.0 + jnp.tanh(c * (x + 0.044715 * (x * x * x)))))


def _compress_body(sk_ref, sv_ref, pos_ref, w1_ref, w2k_ref, w2v_ref, cc_ref, sc_ref, ok_ref, ov_ref, *, ncp):
    def hidden(seg, kv):
        xa = (seg + pos_ref[kv, 0]).astype(BF16)
        xb = (seg + pos_ref[kv, 1]).astype(BF16)
        a = jnp.dot(xa, w1_ref[kv, 0], preferred_element_type=F32)
        b = jnp.dot(xb, w1_ref[kv, 1], preferred_element_type=F32)
        return _gelu_tanh(a + pltpu.roll(b, ncp - 1, axis=0)).astype(BF16)

    gk = hidden(sk_ref[...], 0)
    k = jnp.dot(gk, w2k_ref[0], preferred_element_type=F32)
    kr = jnp.dot(gk, w2k_ref[1], preferred_element_type=F32)
    ok_ref[...] = (k * cc_ref[...] + kr * sc_ref[...]).astype(ok_ref.dtype)
    gv = hidden(sv_ref[...], 1)
    vt = lax.dot_general(w2v_ref[...], gv, _NT, preferred_element_type=F32)
    rows = lax.broadcasted_iota(jnp.int32, (VROWS, ncp), 0)
    ov_ref[...] = jnp.where(rows == HEAD_DIM, 1.0, vt).astype(ov_ref.dtype)


def _compress_call(segk, segv, pos, w1, w2k, w2v, cc, sc):
    b, g, ncp, f = segk.shape
    seg_spec = pl.BlockSpec((None, None, ncp, f), lambda i, j: (i, j, 0, 0))
    return pl.pallas_call(
        functools.partial(_compress_body, ncp=ncp),
        grid=(b, g),
        in_specs=[seg_spec, seg_spec, _const_spec(pos.shape), _const_spec(w1.shape), _const_spec(w2k.shape),
                  _const_spec(w2v.shape), _const_spec(cc.shape), _const_spec(sc.shape)],
        out_specs=[pl.BlockSpec((None, None, ncp, LANES), lambda i, j: (i, j, 0, 0)),
                   pl.BlockSpec((None, None, VROWS, ncp), lambda i, j: (i, j, 0, 0))],
        out_shape=[jax.ShapeDtypeStruct((b, g, ncp, LANES), BF16),
                   jax.ShapeDtypeStruct((b, g, VROWS, ncp), BF16)],
        compiler_params=_cparams(("parallel", "parallel")),
        name="compress",
    )(segk, segv, pos, w1, w2k, w2v, cc, sc)


def _load_queries(q_ref, qa_ref, r, tq):
    for rr in range(r):
        qa_ref[0:HEAD_DIM, rr * tq:(rr + 1) * tq] = q_ref[HEAD_DIM * rr:HEAD_DIM * (rr + 1), :]
    qa_ref[HEAD_DIM:LANES, :] = jnp.zeros((LANES - HEAD_DIM, r * tq), BF16)


def _gated_heads(acc, gate_ref, r, tq, colscale=None):
    inv = 1.0 / acc[HEAD_DIM:HEAD_DIM + 1, :]
    if colscale is not None:
        inv = inv * colscale
    o = acc[0:HEAD_DIM, :] * inv
    parts = []
    for rr in range(r):
        z = o[:, rr * tq:(rr + 1) * tq]
        if gate_ref is not None:
            z = z * gate_ref[rr:rr + 1, :]
        parts.append(z)
    return jnp.concatenate(parts, axis=0)


def _store_heads(o_ref, acc, gate_ref, r, tq, colscale=None):
    o_ref[...] = _gated_heads(acc, gate_ref, r, tq, colscale).T


def _window_tiles(window, tq):
    nt = max((s0 + tq - 1) // TK - (s0 - window + 1) // TK + 1 for s0 in range(8 * window, 8 * window + TK, tq))
    return nt, tuple((i == nt - 1, (nt - i) * TK - 1 >= window) for i in range(nt))


def _window_attend(qa_of, k_ref, v_ref, j0, u, checks, *, window, start, tq, r, m0_of=None, acc0=None,
                   between=None):
    row0 = pl.multiple_of(j0 * TK, TK)
    kt = k_ref[pl.ds(row0, u * TK), :]
    vt = jnp.concatenate([v_ref[j0 + i] for i in range(u)], axis=1)
    d0 = (start - row0) + (lax.broadcasted_iota(jnp.int32, (TK, tq), 1)
                           - lax.broadcasted_iota(jnp.int32, (TK, tq), 0))
    oks = []
    for i in range(u):
        lower, upper = checks[i]
        if lower and upper:
            oks.append((d0 >= i * TK) & (d0 < window + i * TK))
        elif lower:
            oks.append(d0 >= i * TK)
        elif upper:
            oks.append(d0 < window + i * TK)
        else:
            oks.append(None)
    cw = max(tq, 2 * LANES)
    chains = [slice(c0, c0 + cw) for c0 in range(0, r * tq, cw)]
    scores = [jnp.dot(kt, qa_of(ch), preferred_element_type=F32) for ch in chains]
    if between is not None:
        between()

    def masked(si, ok):
        if ok is None:
            return si
        return jnp.concatenate([jnp.where(ok, si[:, c0:c0 + tq], NEG) for c0 in range(0, cw, tq)], axis=1)

    outs = []
    for ch, s in zip(chains, scores):
        s = jnp.concatenate([masked(s[i * TK:(i + 1) * TK], oks[i]) for i in range(u)], axis=0)
        m = jnp.max(s, axis=0, keepdims=True)
        if m0_of is not None:
            m = jnp.maximum(m, m0_of(ch))
        acc = jnp.dot(vt, jnp.exp2(s - m).astype(BF16), preferred_element_type=F32)
        if acc0 is not None:
            acc = acc + acc0 * jnp.exp2(m0_of(ch) - m)
        outs.append(acc)
    return jnp.concatenate(outs, axis=1)


def _cmp_body(q_ref, k_ref, v_ref, gate_ref, o_ref, mb_ref, qa_ref, *, tq, nb, r):
    qi = pl.program_id(2)
    _load_queries(q_ref, qa_ref, r, tq)
    ch = min(CMP_CHUNK, nb)
    buckets = [b for b in range(ch, nb, ch) if b >= N_SELECT] + [nb]
    args = (q_ref, k_ref, v_ref, gate_ref, o_ref, mb_ref, qa_ref, qi)

    @pl.when(qi == 0)
    def _():
        _cmp_rows(*args, tq=tq, nb=nb, r=r, rows=buckets[0], forced_distinct=False)

    lo = 1
    for rows in buckets:
        hi = rows * SEL_BLOCK // tq if rows < nb else pl.num_programs(2)

        @pl.when((qi >= lo) & (qi < hi))
        def _(rows=rows):
            _cmp_rows(*args, tq=tq, nb=nb, r=r, rows=rows, forced_distinct=True)
        lo = hi


def _cmp_rows(q_ref, k_ref, v_ref, gate_ref, o_ref, mb_ref, qa_ref, qi, *, tq, nb, r, rows, forced_distinct):
    lw = r * tq
    ch = min(CMP_CHUNK, nb)
    s = jnp.dot(k_ref[0:4 * rows, :], qa_ref[...], preferred_element_type=F32)
    if rows < nb:
        mb_ref[rows:nb, :] = jnp.full((nb - rows, tq), NEG, mb_ref.dtype)
    nb = rows
    i_io = lax.broadcasted_iota(jnp.int32, (ch, tq), 0)
    lim = qi * tq + lax.broadcasted_iota(jnp.int32, (ch, tq), 1) - (CMP_BLOCK - 1)
    pieces = []
    for c in range(nb // ch):
        for m in range(4):
            valid = SEL_BLOCK * i_io <= lim - (SEL_BLOCK * c * ch + CMP_STRIDE * m)
            r0 = (4 * c + m) * ch
            pieces.append(jnp.concatenate(
                [jnp.where(valid, s[r0:r0 + ch, rr * tq:(rr + 1) * tq], NEG) for rr in range(r)], axis=1))
    mx = pieces[0]
    for x in pieces[1:]:
        mx = jnp.maximum(mx, x)
    mx = jnp.max(mx, axis=0, keepdims=True)
    pm = [jnp.exp2(x - mx) for x in pieces]
    tot = pm[0]
    for x in pm[1:]:
        tot = tot + x
    den = jnp.sum(tot, axis=0, keepdims=True)
    t_lane = qi * tq + (lax.broadcasted_iota(jnp.int32, (1, lw), 1) & (tq - 1))
    has_key = jnp.where(t_lane >= CMP_BLOCK - 1, 1.0, 0.0)
    acc = jnp.dot(v_ref[:, 0:4 * rows], jnp.concatenate(pm, axis=0).astype(BF16), preferred_element_type=F32)
    _store_heads(o_ref, acc, gate_ref, r, tq, has_key)

    inv = has_key / den
    ps = []
    for m in range(4):
        slabs = []
        for c in range(nb // ch):
            pn = pm[4 * c + m] * inv
            acc_h = pn[:, 0:tq]
            for rr in range(1, r):
                acc_h = acc_h + pn[:, rr * tq:(rr + 1) * tq]
            slabs.append(acc_h)
        ps.append(jnp.concatenate(slabs, axis=0))
    j_io = lax.broadcasted_iota(jnp.int32, (nb, tq), 0)
    tt = qi * tq + lax.broadcasted_iota(jnp.int32, (nb, tq), 1)
    prev3 = jnp.where(j_io == 0, 0.0, pltpu.roll(ps[3], 1, axis=0))
    imp = prev3 + 2.0 * (ps[0] + ps[1] + ps[2]) + ps[3]
    cur = tt >> (SEL_BLOCK.bit_length() - 1)
    forced = (j_io == 0) | (j_io == cur) | (j_io == cur - 1)
    causal = j_io * SEL_BLOCK <= tt
    v = jnp.where(forced, REMOVED if forced_distinct else FORCE, jnp.where(causal, imp, -FORCE))
    jf = j_io.astype(F32)
    for _ in range(min(N_SELECT, nb) - (3 if forced_distinct else 0)):
        top = jnp.max(v, axis=0, keepdims=True)
        idx = jnp.min(jnp.where(v == top, jf, float(nb)), axis=0, keepdims=True)
        v = jnp.where(jf == idx, REMOVED, v)
    mb_ref[0:nb, :] = jnp.where(v == REMOVED, 0.0, NEG).astype(mb_ref.dtype)


def _cmp_call(qt, kc, vct, gates, batch, seq):
    g, r = A_KV_HEADS, N_HEADS // A_KV_HEADS
    nb = seq // SEL_BLOCK
    ncp = 4 * nb
    tq = TQ_WINDOW
    return pl.pallas_call(
        functools.partial(_cmp_body, tq=tq, nb=nb, r=r),
        grid=(batch, g, seq // tq),
        in_specs=[pl.BlockSpec((None, r * HEAD_DIM, tq), lambda b, gg, q: (b, gg, q)),
                  pl.BlockSpec((None, None, ncp, LANES), lambda b, gg, q: (b, gg, 0, 0)),
                  pl.BlockSpec((None, None, VROWS, ncp), lambda b, gg, q: (b, gg, 0, 0)),
                  pl.BlockSpec((None, None, None, r, tq), lambda b, gg, q: (b, 0, gg, 0, q))],
        out_specs=[pl.BlockSpec((None, tq, r * HEAD_DIM), lambda b, gg, q: (b, q, gg)),
                   pl.BlockSpec((None, None, nb, tq), lambda b, gg, q: (b, gg, 0, q))],
        out_shape=[jax.ShapeDtypeStruct((batch, seq, N_HEADS * HEAD_DIM), F32),
                   jax.ShapeDtypeStruct((batch, g, nb, seq), BF16)],
        scratch_shapes=[pltpu.VMEM((LANES, r * tq), BF16)],
        compiler_params=_cparams(("parallel", "parallel", "parallel")),
        name="cmp_attn_topk",
    )(qt, kc, vct, gates)


def _sel_body(q_ref, k_ref, v_ref, gate_ref, mb_ref, e_ref, kw_ref, vw_ref, gatew_ref, o_ref,
              qa_ref, acc_ref, accw_ref, m_ref, macc_ref,
              s0_ref, s1_ref, p0_ref, p1_ref, mt0_ref, mt1_ref, cm0_ref, cm1_ref, *, tq, r, mr, nb, window):
    s_refs, p_refs, mt_refs, cm_refs = (s0_ref, s1_ref), (p0_ref, p1_ref), (mt0_ref, mt1_ref), (cm0_ref, cm1_ref)
    qi = pl.program_id(2)
    lw = r * tq
    start = qi * tq
    u = p0_ref.shape[0]
    gph = mr * SEL_BLOCK // (u * TK)
    assert tq == TK
    jd = start // TK
    gd = jd // u
    nh = nb // mr

    _load_queries(q_ref, qa_ref.at[0], r, tq)
    for hf in range(1, nh):
        qa_ref[hf, 0:LANES, :] = qa_ref[0, 0:LANES, :]
    for hf in range(nh):
        for rr in range(r):
            qa_ref[hf, LANES:LANES + mr, rr * tq:(rr + 1) * tq] = mb_ref[hf * mr:(hf + 1) * mr, :]
    acc_ref[...] = jnp.zeros((VROWS, lw), F32)

    def key_tile(gi, i):
        row0 = pl.multiple_of(gi * (u * TK), u * TK)
        eoff = pl.multiple_of(row0 % (mr * SEL_BLOCK), u * TK)
        return jnp.concatenate([k_ref[pl.ds(row0 + i * TK, TK), :], e_ref[pl.ds(eoff + i * TK, TK), :]], axis=1)

    def causal(gi, i):
        t_io = start + (lax.broadcasted_iota(jnp.int32, (TK, lw), 1) & (tq - 1))
        return (gi * u + i) * TK + lax.broadcasted_iota(jnp.int32, (TK, lw), 0) <= t_io

    def pipeline(stage_a, stage_b, stage_c):
        def trip(gi, slot):
            stage_a(gi + 1, 1 - slot)
            stage_b(gi, slot, None)
            stage_c(gi - 1, 1 - slot, u)

        def last(slot, has_prev):
            for kd in range(u):
                @pl.when(jd % u == kd)
                def _(kd=kd):
                    stage_b(gd, slot, kd)
                    if has_prev:
                        stage_c(gd - 1, 1 - slot, u)
                    stage_c(gd, slot, kd + 1)

        nt, _ = _window_tiles(window, tq)
        accw_ref[...] = _window_attend(lambda ch: qa_ref[0, 0:LANES, ch], kw_ref, vw_ref,
                                       jnp.maximum(start // TK - (nt - 1), 0), nt, ((True, True),) * nt,
                                       window=window, start=start, tq=tq, r=r, between=lambda: stage_a(0, 0))

        @pl.when(gd == 0)
        def _():
            last(0, False)

        @pl.when(gd > 0)
        def _():
            stage_a(1, 1)
            stage_b(0, 0, None)
            n = gd - 1

            def body(pi, c):
                trip(1 + 2 * pi, 1)
                trip(2 + 2 * pi, 0)
                return c
            lax.fori_loop(0, n // 2, body, 0)

            @pl.when(n % 2 == 1)
            def _():
                trip(gd - 1, 1)
                last(0, True)

            @pl.when(n % 2 == 0)
            def _():
                last(1, True)

    m_ref[...] = jnp.full((1, lw), M_INIT, F32)
    macc_ref[...] = jnp.full((1, lw), M_INIT, F32)

    def scores(gi, slot):
        qa = qa_ref[gi // gph]
        for i in range(u):
            s = jnp.dot(key_tile(gi, i), qa, preferred_element_type=F32)
            s_refs[slot][i] = s
            cm_refs[slot][i] = jnp.max(s, axis=0, keepdims=True)

    def softmax(gi, slot, diag):
        m = m_ref[...]
        for i in range(u if diag is None else diag + 1):
            if i == diag:
                s = jnp.where(causal(gi, i), s_refs[slot][i], NEG)
                m = jnp.maximum(m, jnp.max(s, axis=0, keepdims=True))
            else:
                m = jnp.maximum(m, cm_refs[slot][i])
                s = s_refs[slot][i]
            p_refs[slot][i] = jnp.exp2(s - m).astype(BF16)
            mt_refs[slot][i] = m
        m_ref[...] = m

    def values(gi, slot, ntiles):
        acc, ma = acc_ref[...], macc_ref[...]
        for i in range(ntiles):
            mt = mt_refs[slot][i]
            acc = acc * jnp.exp2(ma - mt) + jnp.dot(v_ref[gi * u + i], p_refs[slot][i],
                                                    preferred_element_type=F32)
            ma = mt
        acc_ref[...], macc_ref[...] = acc, ma

    pipeline(scores, softmax, values)
    o_ref[...] = (_gated_heads(acc_ref[...], gate_ref, r, tq) + _gated_heads(accw_ref[...], gatew_ref, r, tq)).T


def _swa_body(q_ref, k_ref, v_ref, sink_ref, o_ref, qa_ref, acc_ref, *, tq, r, window):
    m0_of = lambda ch: sink_ref[:, ch]
    acc0 = jnp.where(lax.broadcasted_iota(jnp.int32, (VROWS, 1), 0) == HEAD_DIM, 1.0, 0.0)
    qi = pl.program_id(2)
    start = qi * tq
    jd = start // TK
    _load_queries(q_ref, qa_ref, r, tq)

    def step(j0, checks):
        acc_ref[...] = _window_attend(lambda ch: qa_ref[:, ch], k_ref, v_ref, j0, nt, checks, window=window,
                                      start=start, tq=tq, r=r, m0_of=m0_of, acc0=acc0)

    nt, roles = _window_tiles(window, tq)

    @pl.when(jd >= nt - 1)
    def _():
        step(jd - (nt - 1), roles)

    @pl.when(jd < nt - 1)
    def _():
        step(0, ((True, True),) * nt)

    _store_heads(o_ref, acc_ref[...], None, r, tq)


def _attn_call(mode, qt, k, vt, batch, seq, g, gates=None, branch=None, mb=None, emat=None, sinks=None,
               window=None, kw=None, vwt=None):
    r = N_HEADS // g
    nb = seq // SEL_BLOCK
    mr = emat.shape[1] if mode == "sel" else 0
    nkt = seq // TK
    kc = LANES + mr
    tq = TQ if mode == "sel" else TQ_WINDOW
    k_spec = pl.BlockSpec((None, seq, LANES), lambda b, gg, q: (b, 0, gg))
    v_spec = pl.BlockSpec((None, nkt, VROWS, TK), lambda b, gg, q: (b, 0, gg, 0))
    in_specs = [pl.BlockSpec((None, r * HEAD_DIM, tq), lambda b, gg, q: (b, gg, q)), k_spec, v_spec]
    args = [qt, k, vt]
    if mode == "sel":
        in_specs += [pl.BlockSpec((None, None, None, r, tq), lambda b, gg, q: (b, branch, gg, 0, q)),
                     pl.BlockSpec((None, None, nb, tq), lambda b, gg, q: (b, gg, 0, q)), _const_spec(emat.shape),
                     k_spec, v_spec,
                     pl.BlockSpec((None, None, None, r, tq), lambda b, gg, q: (b, branch + 1, gg, 0, q))]
        args += [gates, mb, emat, kw, vwt, gates]
    else:
        in_specs.append(pl.BlockSpec((None, 1, r * tq), lambda b, gg, q: (gg, 0, 0)))
        args.append(sinks)
    lw = r * tq
    scratch = [pltpu.VMEM((kc, lw), BF16), pltpu.VMEM((VROWS, lw), F32)]
    if mode == "sel":
        u = min(SEL_GROUP, mr * SEL_BLOCK // TK)
        scratch[0] = pltpu.VMEM((nb // mr, kc, lw), BF16)
        scratch += ([pltpu.VMEM((VROWS, lw), F32)] + [pltpu.VMEM((1, lw), F32)] * 2
                    + [pltpu.VMEM((u, TK, lw), F32)] * 2
                    + [pltpu.VMEM((u, TK, lw), BF16)] * 2 + [pltpu.VMEM((u, 1, lw), F32)] * 4)
        body = functools.partial(_sel_body, tq=tq, r=r, mr=mr, nb=nb, window=window)
    else:
        body = functools.partial(_swa_body, tq=tq, r=r, window=window)
    return pl.pallas_call(
        body,
        grid=(batch, g, seq // tq),
        in_specs=in_specs,
        out_specs=pl.BlockSpec((None, tq, r * HEAD_DIM), lambda b, gg, q: (b, q, gg)),
        out_shape=jax.ShapeDtypeStruct((batch, seq, N_HEADS * HEAD_DIM), F32),
        scratch_shapes=scratch,
        compiler_params=_cparams(("parallel", "parallel", "arbitrary")),
        name=mode + "_attn",
    )(*args)


def _ffn_body(*refs, n_o, final, tm, ns, nchunk):
    it = iter(refs)
    h_ref = next(it)
    o_refs = [next(it) for _ in range(n_o)]
    wo_ref, g_ref, wa_ref, wg_ref, cw_ref, wout_ref = (next(it) for _ in range(6))
    gf_ref = next(it) if final else None
    out_ref, hn_ref, y_ref, prev_ref = (next(it) for _ in range(4))
    u_refs = [next(it) for _ in range(3)]
    act_refs = [next(it) for _ in range(3)]

    osum = o_refs[0][...]
    for o_ref in o_refs[1:]:
        osum = osum + o_ref[...]
    h = h_ref[...] + jnp.dot(osum.astype(BF16), wo_ref[...], preferred_element_type=F32)
    hn_ref[...] = _rms(h, g_ref[...]).astype(BF16)
    y_ref[...] = h

    @pl.when(pl.program_id(0) % ns == 0)
    def _():
        prev_ref[...] = jnp.zeros_like(prev_ref)

    rid = lax.broadcasted_iota(jnp.int32, (8, FF_CHUNK), 0)

    def conv(u, p8, w):
        u1 = pltpu.roll(u, 1, axis=0)
        u2 = pltpu.roll(u, 2, axis=0)
        f1 = jnp.where(rid < 1, pltpu.roll(p8, 1, axis=0), u1[0:8])
        f2 = jnp.where(rid < 2, pltpu.roll(p8, 2, axis=0), u2[0:8])
        u1 = jnp.concatenate([f1, u1[8:]], axis=0)
        u2 = jnp.concatenate([f2, u2[8:]], axis=0)
        return w[3:4] + w[0:1] * u2 + w[1:2] * u1 + w[2:3] * u

    def proj_in(c, slot):
        hn = hn_ref[...]
        u_refs[slot][0] = jnp.dot(hn, wa_ref[c], preferred_element_type=F32)
        u_refs[slot][1] = jnp.dot(hn, wg_ref[c], preferred_element_type=F32)

    def gate(c, slot):
        ua, ug = u_refs[slot][0], u_refs[slot][1]
        cw = cw_ref[c]
        pa, pg = prev_ref[c, 0:8], prev_ref[c, 8:16]
        prev_ref[c, 0:8] = ua[tm - 8:tm]
        prev_ref[c, 8:16] = ug[tm - 8:tm]
        ca = conv(ua, pa, cw[0:4])
        cg = conv(ug, pg, cw[4:8])
        act_refs[slot][...] = (ca * jax.nn.sigmoid(ca) * cg).astype(BF16)

    def proj_out(c, slot):
        y_ref[...] += jnp.dot(act_refs[slot][...], wout_ref[c], preferred_element_type=F32)

    proj_in(0, 0)
    for c in range(nchunk):
        if c + 1 < nchunk:
            proj_in(c + 1, (c + 1) % 3)
        gate(c, c % 3)
        if c >= 1:
            proj_out(c - 1, (c - 1) % 3)
    proj_out(nchunk - 1, (nchunk - 1) % 3)
    out = y_ref[...]
    if final:
        out = _rms(out, gf_ref[...])
    out_ref[...] = out


def _ffn_call(h, o_list, wo, gain, wa, wg, cw, wout, seq, final_gain=None):
    t_tokens, d = h.shape
    tm = min(TM, seq)
    ns = seq // tm
    nchunk = wa.shape[0]
    tile = pl.BlockSpec((tm, d), lambda i: (i, 0))
    in_specs = [tile] + [tile] * len(o_list) + [
        _const_spec(wo.shape), _const_spec((1, d)), _const_spec(wa.shape), _const_spec(wg.shape),
        _const_spec(cw.shape), _const_spec(wout.shape)]
    args = [h] + list(o_list) + [wo, gain.reshape(1, d), wa, wg, cw, wout]
    if final_gain is not None:
        in_specs.append(_const_spec((1, d)))
        args.append(final_gain.reshape(1, d))
    return pl.pallas_call(
        functools.partial(_ffn_body, n_o=len(o_list), final=final_gain is not None, tm=tm, ns=ns, nchunk=nchunk),
        grid=(t_tokens // tm,),
        in_specs=in_specs, out_specs=tile,
        out_shape=jax.ShapeDtypeStruct((t_tokens, d), F32),
        scratch_shapes=[pltpu.VMEM((tm, d), BF16), pltpu.VMEM((tm, d), F32),
                        pltpu.VMEM((nchunk, 16, FF_CHUNK), F32)]
        + [pltpu.VMEM((2, tm, FF_CHUNK), F32)] * 3 + [pltpu.VMEM((tm, FF_CHUNK), BF16)] * 3,
        compiler_params=_cparams(("arbitrary",)),
        name="attn_out_ffn",
    )(*args)


def _pad_heads(w, g, width):
    d = w.shape[0]
    w3 = w.reshape(d, g, HEAD_DIM)
    return jnp.pad(w3, ((0, 0), (0, 0), (0, width - HEAD_DIM))).reshape(d, g * width)


def _rot_heads(w, g):
    d = w.shape[0]
    w3 = w.reshape(d, g, HEAD_DIM)
    return jnp.concatenate([-w3[..., HALF:], w3[..., :HALF]], axis=-1).reshape(d, g * HEAD_DIM)


def _rope_k_weights(w, g):
    return jnp.concatenate([_pad_heads(w, g, LANES), _pad_heads(_rot_heads(w, g), g, LANES)], axis=1)


def _rope_tables(seq):
    inv = jnp.float32(ROPE_THETA) ** (-jnp.arange(HALF, dtype=F32) / HALF)

    def cs(pos):
        ang = pos.astype(F32)[:, None] * inv[None, :]
        return jnp.cos(ang), jnp.sin(ang)

    def nat(c):
        return jnp.concatenate([c, c, jnp.zeros((c.shape[0], LANES - HEAD_DIM), F32)], axis=1)

    cos, sin = cs(jnp.arange(seq))
    qscale = HEAD_DIM ** -0.5 * LOG2E
    cc, sc = cs(jnp.arange(seq // CMP_STRIDE) * CMP_STRIDE + CMP_BLOCK - 1)
    return {"cn": nat(cos), "sn": nat(sin), "ct": (cos * qscale).T, "st": (sin * qscale).T,
            "cc": nat(cc), "sc": nat(sc)}


def _ffn_weights(w_in, conv_w, conv_b, w_out):
    d, two_ff = w_in.shape
    dff = two_ff // 2
    assert conv_w.shape[0] == CONV_WIDTH and dff % FF_CHUNK == 0
    nchunk = dff // FF_CHUNK

    def chunks(w):
        return w.reshape(d, nchunk, FF_CHUNK).transpose(1, 0, 2).astype(BF16)

    wa, wg = chunks(w_in[:, :dff]), chunks(w_in[:, dff:])
    taps = jnp.concatenate([conv_w, conv_b[None, :]], axis=0)
    cw = jnp.concatenate([taps[:, :dff].reshape(4, nchunk, FF_CHUNK), taps[:, dff:].reshape(4, nchunk, FF_CHUNK)],
                         axis=0).transpose(1, 0, 2)
    return wa, wg, cw, w_out.reshape(nchunk, FF_CHUNK, d).astype(BF16)


def _nsa_attention(h, gain, w_in, cmp_pos, cmp_w1, cmp_w2, tabs, emat, batch, seq):
    g, r = A_KV_HEADS, N_HEADS // A_KV_HEADS
    d = h.shape[1]
    kvw = g * HEAD_DIM
    nq = N_HEADS * HEAD_DIM
    wq, wkc, wvc, wks, wvs, wkw, wvw, wgl = jnp.split(
        w_in, [nq, nq + kvw, nq + 2 * kvw, nq + 3 * kvw, nq + 4 * kvw, nq + 5 * kvw, nq + 6 * kvw], axis=1)
    wn = jnp.concatenate([_rope_k_weights(wks, g), _rope_k_weights(wkw, g), wkc, wvc], axis=1).astype(BF16)
    wgl = wgl.reshape(d, N_HEADS, 3).transpose(0, 2, 1).reshape(d, 3 * N_HEADS)
    wt = jnp.concatenate([wq, wgl, _pad_heads(wvs, g, VROWS), _pad_heads(wvw, g, VROWS)], axis=1).T.astype(BF16)
    kpad = g * LANES
    nat_plan = [("rope", 0, kpad), ("rope", 2 * kpad, kpad), ("heads", 4 * kpad, kvw), ("heads", 4 * kpad + kvw, kvw)]
    ng = 3 * N_HEADS
    tr_plan = [("ropeq", 0, nq), ("sigmoid", nq, ng), ("vaug", nq + ng, g * VROWS),
               ("vaug", nq + ng + g * VROWS, g * VROWS)]
    ks, kw, kc, vc, qt, gates, vst, vwt = _proj_call(
        h, gain, tabs, wn, wt, nat_plan, tr_plan, [BF16, BF16, F32, F32], batch, seq)
    ks = ks.reshape(batch, seq, kpad)
    kw = kw.reshape(batch, seq, kpad)
    gates = gates.reshape(batch, 3, g, r, seq)

    nseg = seq // CMP_STRIDE
    nb = seq // SEL_BLOCK
    seg_f = CMP_STRIDE * HEAD_DIM

    def segs(x):
        return x.reshape(batch, g, nseg, seg_f)

    pos = cmp_pos.reshape(2, 2, 1, seg_f)
    w1 = cmp_w1.reshape(2, 2, seg_f, cmp_w1.shape[-1]).astype(BF16)
    w2k = jnp.stack([_pad_heads(cmp_w2[0], 1, LANES), _pad_heads(_rot_heads(cmp_w2[0], 1), 1, LANES)]).astype(BF16)
    w2v = _pad_heads(cmp_w2[1], 1, VROWS).T.astype(BF16)
    kcc, vcc = _compress_call(segs(kc), segs(vc), pos, w1, w2k, w2v, tabs["cc"], tabs["sc"])
    ch = min(CMP_CHUNK, nb)
    kcc = kcc.reshape(batch, g, nb // ch, ch, 4, LANES).transpose(0, 1, 2, 4, 3, 5).reshape(batch, g, nseg, LANES)
    vcc = vcc.reshape(batch, g, VROWS, nb // ch, ch, 4).transpose(0, 1, 2, 3, 5, 4).reshape(batch, g, VROWS, nseg)

    o_c, mb = _cmp_call(qt, kcc, vcc, gates, batch, seq)
    o_sw = _attn_call("sel", qt, ks, vst, batch, seq, g, gates=gates, branch=1, mb=mb, emat=emat,
                      kw=kw, vwt=vwt, window=WINDOW_A)
    t_tokens = batch * seq
    return [o.reshape(t_tokens, nq) for o in (o_c, o_sw)]


def kernel(x, norm_attn, norm_ffn, a_w_in, a_cmp_pos, a_cmp_w1, a_cmp_w2, a_w_out, kv_norm, b_w_kv, b_w_q, b_sinks,
           b_w_out, ffn_w_in, ffn_conv_w, ffn_conv_b, ffn_w_out, final_norm):
    batch, seq, d = x.shape
    depth = norm_attn.shape[0]
    n_a = a_w_in.shape[0]
    tabs = _rope_tables(seq)
    mr = min(MASK_ROWS, seq // SEL_BLOCK)
    emat = (jnp.arange(mr * SEL_BLOCK)[:, None] // SEL_BLOCK == jnp.arange(mr)[None, :]).astype(BF16)
    h = x.reshape(batch * seq, d)
    k_sh = v_sh = None
    gb, rb = B_KV_HEADS, N_HEADS // B_KV_HEADS
    for layer in range(depth):
        if layer < n_a:
            o_list = _nsa_attention(h, norm_attn[layer], a_w_in[layer], a_cmp_pos[layer], a_cmp_w1[layer],
                                    a_cmp_w2[layer], tabs, emat, batch, seq)
            wo = a_w_out[layer]
        else:
            j = layer - n_a
            if k_sh is None:
                wk, wv = jnp.split(b_w_kv, 2, axis=1)
                kpad = gb * LANES
                k_sh, v_sh = _proj_call(
                    h, kv_norm, tabs, _rope_k_weights(wk, gb).astype(BF16), _pad_heads(wv, gb, VROWS).T.astype(BF16),
                    [("rope", 0, kpad)], [("vaug", 0, gb * VROWS)], [BF16], batch, seq)
                k_sh = k_sh.reshape(batch, seq, kpad)
            (qt,) = _proj_call(h, norm_attn[layer], tabs, None, b_w_q[j].T.astype(BF16), [],
                               [("ropeq", 0, N_HEADS * HEAD_DIM)], [], batch, seq)
            sinks = jnp.broadcast_to((b_sinks[j] * LOG2E).reshape(gb, 1, rb, 1),
                                     (gb, 1, rb, TQ_WINDOW)).reshape(gb, 1, rb * TQ_WINDOW)
            o = _attn_call("swa", qt, k_sh, v_sh, batch, seq, gb, sinks=sinks, window=WINDOW_B)
            o_list = [o.reshape(batch * seq, N_HEADS * HEAD_DIM)]
            wo = b_w_out[j]
        wa, wg, cw, wout = _ffn_weights(ffn_w_in[layer], ffn_conv_w[layer], ffn_conv_b[layer], ffn_w_out[layer])
        h = _ffn_call(h, o_list, wo.astype(BF16), norm_ffn[layer], wa, wg, cw, wout, seq,
                      final_gain=final_norm if layer == depth - 1 else None)
    return h.reshape(batch, seq, d)
```

```python
import functools
import math

import jax
import jax.numpy as jnp
from jax import lax
from jax.experimental import pallas as pl
from jax.experimental.pallas import tpu as pltpu

F32 = jnp.float32
BF16 = jnp.bfloat16

HEAD_DIM = 64
HALF = HEAD_DIM // 2
N_HEADS = 16
A_KV_HEADS = 4
B_KV_HEADS = 2
CMP_BLOCK = 32
CMP_STRIDE = 16
SEL_BLOCK = 64
N_SELECT = 16
WINDOW_A = 512
WINDOW_B = 128
CONV_WIDTH = 3
ROPE_THETA = 10000.0
EPS = 1e-6
FORCE = 1e6

LANES = 128
BF16_SUBLANES = 16
VMEM_LIMIT = 56 * 1024 * 1024
TQ = 256
TQ_WINDOW = 256
TK = 256
SEL_GROUP = 4
CMP_CHUNK = 64
TM = 512
FF_CHUNK = 256
VROWS = HEAD_DIM + BF16_SUBLANES
MASK_ROWS = 128

LOG2E = 1.4426950408889634
NEG = -1e30
M_INIT = -1e29
REMOVED = -3e38

_NT = (((1,), (1,)), ((), ()))


def _cparams(sem):
    return pltpu.CompilerParams(dimension_semantics=sem, vmem_limit_bytes=VMEM_LIMIT)


def _const_spec(shape):
    n = len(shape)
    return pl.BlockSpec(shape, lambda *_: (0,) * n, pipeline_mode=pl.Buffered(1))


def _rms(x, g):
    ms = jnp.mean(x * x, axis=-1, keepdims=True)
    return x * lax.rsqrt(ms + EPS) * g


def _proj_body(*refs, nat_plan, tr_plan, tm):
    it = iter(refs)
    h_ref, g_ref = next(it), next(it)
    if nat_plan:
        cn_ref, sn_ref, wn_ref = next(it), next(it), next(it)
    ct_ref, st_ref, wt_ref = next(it), next(it), next(it)
    outs = list(it)

    hn = _rms(h_ref[...], g_ref[...]).astype(BF16)
    oi = 0
    for kind, c0, n in nat_plan:
        o_ref = outs[oi]
        oi += 1
        y = jnp.dot(hn, wn_ref[:, c0:c0 + n], preferred_element_type=F32)
        if kind == "rope":
            yr = jnp.dot(hn, wn_ref[:, c0 + n:c0 + 2 * n], preferred_element_type=F32)
            c, s = cn_ref[...], sn_ref[...]
            for g in range(n // LANES):
                sl = slice(LANES * g, LANES * (g + 1))
                o_ref[:, sl] = (y[:, sl] * c + yr[:, sl] * s).astype(o_ref.dtype)
        elif kind == "heads":
            for g in range(n // HEAD_DIM):
                o_ref[g] = y[:, HEAD_DIM * g:HEAD_DIM * (g + 1)].astype(o_ref.dtype)
        else:
            o_ref[...] = y.astype(o_ref.dtype)
    for kind, r0, n in tr_plan:
        o_ref = outs[oi]
        oi += 1
        y = lax.dot_general(wt_ref[r0:r0 + n, :], hn, _NT, preferred_element_type=F32)
        if kind == "ropeq":
            c, s = ct_ref[...], st_ref[...]
            for hd in range(n // HEAD_DIM):
                a = HEAD_DIM * hd
                y1, y2 = y[a:a + HALF], y[a + HALF:a + HEAD_DIM]
                o_ref[a:a + HALF, :] = (y1 * c - y2 * s).astype(o_ref.dtype)
                o_ref[a + HALF:a + HEAD_DIM, :] = (y2 * c + y1 * s).astype(o_ref.dtype)
        elif kind == "sigmoid":
            o_ref[...] = jax.nn.sigmoid(y)
        else:
            rows = lax.broadcasted_iota(jnp.int32, (VROWS, tm), 0)
            for g in range(n // VROWS):
                yg = jnp.where(rows == HEAD_DIM, 1.0, y[VROWS * g:VROWS * (g + 1)]).astype(o_ref.dtype)
                for t in range(tm // TK):
                    o_ref[t, VROWS * g:VROWS * (g + 1), :] = yg[:, t * TK:(t + 1) * TK]


def _proj_call(h, gain, tabs, wn, wt, nat_plan, tr_plan, nat_dtypes, batch, seq):
    t_tokens, d = h.shape
    tm = min(TM, seq)
    ns = seq // tm
    in_specs = [pl.BlockSpec((tm, d), lambda i: (i, 0)), _const_spec((1, d))]
    args = [h, gain.reshape(1, d)]
    if nat_plan:
        in_specs += [pl.BlockSpec((tm, LANES), lambda i: (i % ns, 0)),
                     pl.BlockSpec((tm, LANES), lambda i: (i % ns, 0)),
                     _const_spec(wn.shape)]
        args += [tabs["cn"], tabs["sn"], wn]
    in_specs += [pl.BlockSpec((HALF, tm), lambda i: (0, i % ns)),
                 pl.BlockSpec((HALF, tm), lambda i: (0, i % ns)),
                 _const_spec(wt.shape)]
    args += [tabs["ct"], tabs["st"], wt]
    out_shape, out_specs = [], []
    for (kind, _, n), dt in zip(nat_plan, nat_dtypes):
        if kind == "heads":
            nh = n // HEAD_DIM
            out_shape.append(jax.ShapeDtypeStruct((batch, nh, seq, HEAD_DIM), dt))
            out_specs.append(pl.BlockSpec((None, nh, tm, HEAD_DIM), lambda i: (i // ns, 0, i % ns, 0)))
        else:
            out_shape.append(jax.ShapeDtypeStruct((t_tokens, n), dt))
            out_specs.append(pl.BlockSpec((tm, n), lambda i: (i, 0)))
    for kind, _, n in tr_plan:
        if kind == "vaug":
            out_shape.append(jax.ShapeDtypeStruct((batch, seq // TK, n, TK), BF16))
            out_specs.append(pl.BlockSpec((None, tm // TK, n, TK), lambda i: (i // ns, i % ns, 0, 0)))
        else:
            dt = F32 if kind == "sigmoid" else BF16
            out_shape.append(jax.ShapeDtypeStruct((batch, n, seq), dt))
            out_specs.append(pl.BlockSpec((None, n, tm), lambda i: (i // ns, 0, i % ns)))
    return pl.pallas_call(
        functools.partial(_proj_body, nat_plan=tuple(nat_plan), tr_plan=tuple(tr_plan), tm=tm),
        grid=(t_tokens // tm,),
        in_specs=in_specs, out_specs=out_specs, out_shape=out_shape,
        compiler_params=_cparams(("parallel",)),
        name="norm_proj",
    )(*args)


def _gelu_tanh(x):
    c = math.sqrt(2.0 / math.pi)
    return x * (0.5 * (1.0 + jnp.tanh(c * (x + 0.044715 * (x * x * x)))))


def _compress_body(sk_ref, sv_ref, pos_ref, w1_ref, w2k_ref, w2v_ref, cc_ref, sc_ref, ok_ref, ov_ref, *, ncp):
    def hidden(seg, kv):
        xa = (seg + pos_ref[kv, 0]).astype(BF16)
        xb = (seg + pos_ref[kv, 1]).astype(BF16)
        a = jnp.dot(xa, w1_ref[kv, 0], preferred_element_type=F32)
        b = jnp.dot(xb, w1_ref[kv, 1], preferred_element_type=F32)
        return _gelu_tanh(a + pltpu.roll(b, ncp - 1, axis=0)).astype(BF16)

    gk = hidden(sk_ref[...], 0)
    k = jnp.dot(gk, w2k_ref[0], preferred_element_type=F32)
    kr = jnp.dot(gk, w2k_ref[1], preferred_element_type=F32)
    ok_ref[...] = (k * cc_ref[...] + kr * sc_ref[...]).astype(ok_ref.dtype)
    gv = hidden(sv_ref[...], 1)
    vt = lax.dot_general(w2v_ref[...], gv, _NT, preferred_element_type=F32)
    rows = lax.broadcasted_iota(jnp.int32, (VROWS, ncp), 0)
    ov_ref[...] = jnp.where(rows == HEAD_DIM, 1.0, vt).astype(ov_ref.dtype)


def _compress_call(segk, segv, pos, w1, w2k, w2v, cc, sc):
    b, g, ncp, f = segk.shape
    seg_spec = pl.BlockSpec((None, None, ncp, f), lambda i, j: (i, j, 0, 0))
    return pl.pallas_call(
        functools.partial(_compress_body, ncp=ncp),
        grid=(b, g),
        in_specs=[seg_spec, seg_spec, _const_spec(pos.shape), _const_spec(w1.shape), _const_spec(w2k.shape),
                  _const_spec(w2v.shape), _const_spec(cc.shape), _const_spec(sc.shape)],
        out_specs=[pl.BlockSpec((None, None, ncp, LANES), lambda i, j: (i, j, 0, 0)),
                   pl.BlockSpec((None, None, VROWS, ncp), lambda i, j: (i, j, 0, 0))],
        out_shape=[jax.ShapeDtypeStruct((b, g, ncp, LANES), BF16),
                   jax.ShapeDtypeStruct((b, g, VROWS, ncp), BF16)],
        compiler_params=_cparams(("parallel", "parallel")),
        name="compress",
    )(segk, segv, pos, w1, w2k, w2v, cc, sc)


def _load_queries(q_ref, qa_ref, r, tq):
    for rr in range(r):
        qa_ref[0:HEAD_DIM, rr * tq:(rr + 1) * tq] = q_ref[HEAD_DIM * rr:HEAD_DIM * (rr + 1), :]
    qa_ref[HEAD_DIM:LANES, :] = jnp.zeros((LANES - HEAD_DIM, r * tq), BF16)


def _gated_heads(acc, gate_ref, r, tq, colscale=None):
    inv = 1.0 / acc[HEAD_DIM:HEAD_DIM + 1, :]
    if colscale is not None:
        inv = inv * colscale
    o = acc[0:HEAD_DIM, :] * inv
    parts = []
    for rr in range(r):
        z = o[:, rr * tq:(rr + 1) * tq]
        if gate_ref is not None:
            z = z * gate_ref[rr:rr + 1, :]
        parts.append(z)
    return jnp.concatenate(parts, axis=0)


def _store_heads(o_ref, acc, gate_ref, r, tq, colscale=None):
    o_ref[...] = _gated_heads(acc, gate_ref, r, tq, colscale).T


def _window_tiles(window, tq):
    nt = max((s0 + tq - 1) // TK - (s0 - window + 1) // TK + 1 for s0 in range(8 * window, 8 * window + TK, tq))
    return nt, tuple((i == nt - 1, (nt - i) * TK - 1 >= window) for i in range(nt))


def _window_attend(qa_of, k_ref, v_ref, j0, u, checks, *, window, start, tq, r, m0_of=None, acc0=None,
                   between=None, skip=0):
    row0 = pl.multiple_of(j0 * TK, TK)
    kt = k_ref[pl.ds(pl.multiple_of(row0 + skip, LANES), u * TK - skip), :]
    vt = jnp.concatenate([v_ref[j0][:, skip:]] + [v_ref[j0 + i] for i in range(1, u)], axis=1)
    bounds = [(max(i * TK - skip, 0), (i + 1) * TK - skip) for i in range(u)]
    d0 = (start - row0) + (lax.broadcasted_iota(jnp.int32, (TK, tq), 1)
                           - lax.broadcasted_iota(jnp.int32, (TK, tq), 0))
    oks = []
    for i in range(u):
        lower, upper = checks[i]
        if lower and upper:
            oks.append((d0 >= i * TK) & (d0 < window + i * TK))
        elif lower:
            oks.append(d0 >= i * TK)
        elif upper:
            oks.append(d0 < window + i * TK)
        else:
            oks.append(None)
    if skip and oks[0] is not None:
        oks[0] = oks[0][skip:]
    cw = max(tq, 2 * LANES)
    chains = [slice(c0, c0 + cw) for c0 in range(0, r * tq, cw)]
    scores = [jnp.dot(kt, qa_of(ch), preferred_element_type=F32) for ch in chains]
    if between is not None:
        between()

    def masked(si, ok):
        if ok is None:
            return si
        return jnp.concatenate([jnp.where(ok, si[:, c0:c0 + tq], NEG) for c0 in range(0, cw, tq)], axis=1)

    outs = []
    for ch, s in zip(chains, scores):
        s = jnp.concatenate([masked(s[lo:hi], oks[i]) for i, (lo, hi) in enumerate(bounds)], axis=0)
        m = jnp.max(s, axis=0, keepdims=True)
        if m0_of is not None:
            m = jnp.maximum(m, m0_of(ch))
        acc = jnp.dot(vt, jnp.exp2(s - m).astype(BF16), preferred_element_type=F32)
        if acc0 is not None:
            acc = acc + acc0 * jnp.exp2(m0_of(ch) - m)
        outs.append(acc)
    return jnp.concatenate(outs, axis=1)


def _cmp_body(q_ref, k_ref, v_ref, gate_ref, o_ref, mb_ref, qa_ref, *, tq, nb, r):
    qi = pl.program_id(2)
    _load_queries(q_ref, qa_ref, r, tq)
    ch = min(CMP_CHUNK, nb)
    buckets = [b for b in range(ch, nb, ch) if b >= N_SELECT] + [nb]
    args = (q_ref, k_ref, v_ref, gate_ref, o_ref, mb_ref, qa_ref, qi)

    @pl.when(qi == 0)
    def _():
        _cmp_rows(*args, tq=tq, nb=nb, r=r, rows=buckets[0], forced_distinct=False)

    lo = 1
    for rows in buckets:
        hi = rows * SEL_BLOCK // tq if rows < nb else pl.num_programs(2)

        @pl.when((qi >= lo) & (qi < hi))
        def _(rows=rows):
            _cmp_rows(*args, tq=tq, nb=nb, r=r, rows=rows, forced_distinct=True)
        lo = hi


def _cmp_rows(q_ref, k_ref, v_ref, gate_ref, o_ref, mb_ref, qa_ref, qi, *, tq, nb, r, rows, forced_distinct):
    lw = r * tq
    ch = min(CMP_CHUNK, nb)
    s = jnp.dot(k_ref[0:4 * rows, :], qa_ref[...], preferred_element_type=F32)
    if rows < nb:
        mb_ref[rows:nb, :] = jnp.full((nb - rows, tq), NEG, mb_ref.dtype)
    nb = rows
    i_io = lax.broadcasted_iota(jnp.int32, (ch, tq), 0)
    lim = qi * tq + lax.broadcasted_iota(jnp.int32, (ch, tq), 1) - (CMP_BLOCK - 1)
    pieces = []
    for c in range(nb // ch):
        for m in range(4):
            valid = SEL_BLOCK * i_io <= lim - (SEL_BLOCK * c * ch + CMP_STRIDE * m)
            r0 = (4 * c + m) * ch
            pieces.append(jnp.concatenate(
                [jnp.where(valid, s[r0:r0 + ch, rr * tq:(rr + 1) * tq], NEG) for rr in range(r)], axis=1))
    mx = pieces[0]
    for x in pieces[1:]:
        mx = jnp.maximum(mx, x)
    mx = jnp.max(mx, axis=0, keepdims=True)
    pm = [jnp.exp2(x - mx) for x in pieces]
    tot = pm[0]
    for x in pm[1:]:
        tot = tot + x
    den = jnp.sum(tot, axis=0, keepdims=True)
    t_lane = qi * tq + (lax.broadcasted_iota(jnp.int32, (1, lw), 1) & (tq - 1))
    has_key = jnp.where(t_lane >= CMP_BLOCK - 1, 1.0, 0.0)
    acc = jnp.dot(v_ref[:, 0:4 * rows], jnp.concatenate(pm, axis=0).astype(BF16), preferred_element_type=F32)
    _store_heads(o_ref, acc, gate_ref, r, tq, has_key)

    inv = has_key / den
    ps = []
    for m in range(4):
        slabs = []
        for c in range(nb // ch):
            pn = pm[4 * c + m] * inv
            acc_h = pn[:, 0:tq]
            for rr in range(1, r):
                acc_h = acc_h + pn[:, rr * tq:(rr + 1) * tq]
            slabs.append(acc_h)
        ps.append(jnp.concatenate(slabs, axis=0))
    j_io = lax.broadcasted_iota(jnp.int32, (nb, tq), 0)
    tt = qi * tq + lax.broadcasted_iota(jnp.int32, (nb, tq), 1)
    prev3 = jnp.where(j_io == 0, 0.0, pltpu.roll(ps[3], 1, axis=0))
    imp = prev3 + 2.0 * (ps[0] + ps[1] + ps[2]) + ps[3]
    cur = tt >> (SEL_BLOCK.bit_length() - 1)
    forced = (j_io == 0) | (j_io == cur) | (j_io == cur - 1)
    causal = j_io * SEL_BLOCK <= tt
    v = jnp.where(forced, REMOVED if forced_distinct else FORCE, jnp.where(causal, imp, -FORCE))
    jf = j_io.astype(F32)
    for _ in range(min(N_SELECT, nb) - (3 if forced_distinct else 0)):
        top = jnp.max(v, axis=0, keepdims=True)
        idx = jnp.min(jnp.where(v == top, jf, float(nb)), axis=0, keepdims=True)
        v = jnp.where(jf == idx, REMOVED, v)
    mb_ref[0:nb, :] = jnp.where(v == REMOVED, 0.0, NEG).astype(mb_ref.dtype)


def _cmp_call(qt, kc, vct, gates, batch, seq):
    g, r = A_KV_HEADS, N_HEADS // A_KV_HEADS
    nb = seq // SEL_BLOCK
    ncp = 4 * nb
    tq = TQ_WINDOW
    return pl.pallas_call(
        functools.partial(_cmp_body, tq=tq, nb=nb, r=r),
        grid=(batch, g, seq // tq),
        in_specs=[pl.BlockSpec((None, r * HEAD_DIM, tq), lambda b, gg, q: (b, gg, q)),
                  pl.BlockSpec((None, None, ncp, LANES), lambda b, gg, q: (b, gg, 0, 0)),
                  pl.BlockSpec((None, None, VROWS, ncp), lambda b, gg, q: (b, gg, 0, 0)),
                  pl.BlockSpec((None, None, None, r, tq), lambda b, gg, q: (b, 0, gg, 0, q))],
        out_specs=[pl.BlockSpec((None, tq, r * HEAD_DIM), lambda b, gg, q: (b, q, gg)),
                   pl.BlockSpec((None, None, nb, tq), lambda b, gg, q: (b, gg, 0, q))],
        out_shape=[jax.ShapeDtypeStruct((batch, seq, N_HEADS * HEAD_DIM), F32),
                   jax.ShapeDtypeStruct((batch, g, nb, seq), BF16)],
        scratch_shapes=[pltpu.VMEM((LANES, r * tq), BF16)],
        compiler_params=_cparams(("parallel", "parallel", "parallel")),
        name="cmp_attn_topk",
    )(qt, kc, vct, gates)


def _sel_body(q_ref, k_ref, v_ref, gate_ref, mb_ref, e_ref, kw_ref, vw_ref, gatew_ref, o_ref,
              qa_ref, acc_ref, accw_ref, m_ref, macc_ref,
              s0_ref, s1_ref, p0_ref, p1_ref, mt0_ref, mt1_ref, cm0_ref, cm1_ref, *, tq, r, mr, nb, window):
    s_refs, p_refs, mt_refs, cm_refs = (s0_ref, s1_ref), (p0_ref, p1_ref), (mt0_ref, mt1_ref), (cm0_ref, cm1_ref)
    qi = pl.program_id(2)
    lw = r * tq
    start = qi * tq
    u = p0_ref.shape[0]
    gph = mr * SEL_BLOCK // (u * TK)
    assert tq == TK
    jd = start // TK
    gd = jd // u
    nh = nb // mr

    _load_queries(q_ref, qa_ref.at[0], r, tq)
    for hf in range(1, nh):
        qa_ref[hf, 0:LANES, :] = qa_ref[0, 0:LANES, :]
    for hf in range(nh):
        for rr in range(r):
            qa_ref[hf, LANES:LANES + mr, rr * tq:(rr + 1) * tq] = mb_ref[hf * mr:(hf + 1) * mr, :]
    acc_ref[...] = jnp.zeros((VROWS, lw), F32)

    def key_tile(gi, i):
        row0 = pl.multiple_of(gi * (u * TK), u * TK)
        eoff = pl.multiple_of(row0 % (mr * SEL_BLOCK), u * TK)
        return jnp.concatenate([k_ref[pl.ds(row0 + i * TK, TK), :], e_ref[pl.ds(eoff + i * TK, TK), :]], axis=1)

    def causal(gi, i):
        t_io = start + (lax.broadcasted_iota(jnp.int32, (TK, lw), 1) & (tq - 1))
        return (gi * u + i) * TK + lax.broadcasted_iota(jnp.int32, (TK, lw), 0) <= t_io

    def pipeline(stage_a, stage_b, stage_c):
        def trip(gi, slot):
            stage_a(gi + 1, 1 - slot)
            stage_b(gi, slot, None)
            stage_c(gi - 1, 1 - slot, u)

        def last(slot, has_prev):
            for kd in range(u):
                @pl.when(jd % u == kd)
                def _(kd=kd):
                    stage_b(gd, slot, kd)
                    if has_prev:
                        stage_c(gd - 1, 1 - slot, u)
                    stage_c(gd, slot, kd + 1)

        nt, _ = _window_tiles(window, tq)
        accw_ref[...] = _window_attend(lambda ch: qa_ref[0, 0:LANES, ch], kw_ref, vw_ref,
                                       jnp.maximum(start // TK - (nt - 1), 0), nt, ((True, True),) * nt,
                                       window=window, start=start, tq=tq, r=r, between=lambda: stage_a(0, 0))

        @pl.when(gd == 0)
        def _():
            last(0, False)

        @pl.when(gd > 0)
        def _():
            stage_a(1, 1)
            stage_b(0, 0, None)
            n = gd - 1

            def body(pi, c):
                trip(1 + 2 * pi, 1)
                trip(2 + 2 * pi, 0)
                return c
            lax.fori_loop(0, n // 2, body, 0)

            @pl.when(n % 2 == 1)
            def _():
                trip(gd - 1, 1)
                last(0, True)

            @pl.when(n % 2 == 0)
            def _():
                last(1, True)

    m_ref[...] = jnp.full((1, lw), M_INIT, F32)
    macc_ref[...] = jnp.full((1, lw), M_INIT, F32)

    def scores(gi, slot):
        qa = qa_ref[gi // gph]
        for i in range(u):
            s = jnp.dot(key_tile(gi, i), qa, preferred_element_type=F32)
            s_refs[slot][i] = s
            cm_refs[slot][i] = jnp.max(s, axis=0, keepdims=True)

    def softmax(gi, slot, diag):
        m = m_ref[...]
        for i in range(u if diag is None else diag + 1):
            if i == diag:
                s = jnp.where(causal(gi, i), s_refs[slot][i], NEG)
                m = jnp.maximum(m, jnp.max(s, axis=0, keepdims=True))
            else:
                m = jnp.maximum(m, cm_refs[slot][i])
                s = s_refs[slot][i]
            p_refs[slot][i] = jnp.exp2(s - m).astype(BF16)
            mt_refs[slot][i] = m
        m_ref[...] = m

    def values(gi, slot, ntiles):
        acc, ma = acc_ref[...], macc_ref[...]
        for i in range(ntiles):
            mt = mt_refs[slot][i]
            acc = acc * jnp.exp2(ma - mt) + jnp.dot(v_ref[gi * u + i], p_refs[slot][i],
                                                    preferred_element_type=F32)
            ma = mt
        acc_ref[...], macc_ref[...] = acc, ma

    pipeline(scores, softmax, values)
    o_ref[...] = (_gated_heads(acc_ref[...], gate_ref, r, tq) + _gated_heads(accw_ref[...], gatew_ref, r, tq)).T


def _swa_body(q_ref, k_ref, v_ref, sink_ref, o_ref, qa_ref, acc_ref, *, tq, r, window):
    m0_of = lambda ch: sink_ref[:, ch]
    acc0 = jnp.where(lax.broadcasted_iota(jnp.int32, (VROWS, 1), 0) == HEAD_DIM, 1.0, 0.0)
    qi = pl.program_id(2)
    start = qi * tq
    jd = start // TK
    _load_queries(q_ref, qa_ref, r, tq)

    def step(j0, checks, skip):
        acc_ref[...] = _window_attend(lambda ch: qa_ref[:, ch], k_ref, v_ref, j0, nt, checks, window=window,
                                      start=start, tq=tq, r=r, m0_of=m0_of, acc0=acc0, skip=skip)

    nt, roles = _window_tiles(window, tq)
    skip = (nt * TK - (tq + window - 1)) // LANES * LANES if tq % TK == 0 else 0

    @pl.when(jd >= nt - 1)
    def _():
        step(jd - (nt - 1), roles, skip)

    @pl.when(jd < nt - 1)
    def _():
        step(0, ((True, True),) * nt, 0)

    _store_heads(o_ref, acc_ref[...], None, r, tq)


def _attn_call(mode, qt, k, vt, batch, seq, g, gates=None, branch=None, mb=None, emat=None, sinks=None,
               window=None, kw=None, vwt=None):
    r = N_HEADS // g
    nb = seq // SEL_BLOCK
    mr = emat.shape[1] if mode == "sel" else 0
    nkt = seq // TK
    kc = LANES + mr
    tq = TQ if mode == "sel" else TQ_WINDOW
    k_spec = pl.BlockSpec((None, seq, LANES), lambda b, gg, q: (b, 0, gg))
    v_spec = pl.BlockSpec((None, nkt, VROWS, TK), lambda b, gg, q: (b, 0, gg, 0))
    in_specs = [pl.BlockSpec((None, r * HEAD_DIM, tq), lambda b, gg, q: (b, gg, q)), k_spec, v_spec]
    args = [qt, k, vt]
    if mode == "sel":
        in_specs += [pl.BlockSpec((None, None, None, r, tq), lambda b, gg, q: (b, branch, gg, 0, q)),
                     pl.BlockSpec((None, None, nb, tq), lambda b, gg, q: (b, gg, 0, q)), _const_spec(emat.shape),
                     k_spec, v_spec,
                     pl.BlockSpec((None, None, None, r, tq), lambda b, gg, q: (b, branch + 1, gg, 0, q))]
        args += [gates, mb, emat, kw, vwt, gates]
    else:
        in_specs.append(pl.BlockSpec((None, 1, r * tq), lambda b, gg, q: (gg, 0, 0)))
        args.append(sinks)
    lw = r * tq
    scratch = [pltpu.VMEM((kc, lw), BF16), pltpu.VMEM((VROWS, lw), F32)]
    if mode == "sel":
        u = min(SEL_GROUP, mr * SEL_BLOCK // TK)
        scratch[0] = pltpu.VMEM((nb // mr, kc, lw), BF16)
        scratch += ([pltpu.VMEM((VROWS, lw), F32)] + [pltpu.VMEM((1, lw), F32)] * 2
                    + [pltpu.VMEM((u, TK, lw), F32)] * 2
                    + [pltpu.VMEM((u, TK, lw), BF16)] * 2 + [pltpu.VMEM((u, 1, lw), F32)] * 4)
        body = functools.partial(_sel_body, tq=tq, r=r, mr=mr, nb=nb, window=window)
    else:
        body = functools.partial(_swa_body, tq=tq, r=r, window=window)
    return pl.pallas_call(
        body,
        grid=(batch, g, seq // tq),
        in_specs=in_specs,
        out_specs=pl.BlockSpec((None, tq, r * HEAD_DIM), lambda b, gg, q: (b, q, gg)),
        out_shape=jax.ShapeDtypeStruct((batch, seq, N_HEADS * HEAD_DIM), F32),
        scratch_shapes=scratch,
        compiler_params=_cparams(("parallel", "parallel", "arbitrary")),
        name=mode + "_attn",
    )(*args)


def _ffn_body(*refs, n_o, final, tm, ns, nchunk):
    it = iter(refs)
    h_ref = next(it)
    o_refs = [next(it) for _ in range(n_o)]
    wo_ref, g_ref, wa_ref, wg_ref, cw_ref, wout_ref = (next(it) for _ in range(6))
    gf_ref = next(it) if final else None
    out_ref, hn_ref, y_ref, prev_ref = (next(it) for _ in range(4))
    u_refs = [next(it) for _ in range(3)]
    act_refs = [next(it) for _ in range(3)]

    osum = o_refs[0][...]
    for o_ref in o_refs[1:]:
        osum = osum + o_ref[...]
    h = h_ref[...] + jnp.dot(osum.astype(BF16), wo_ref[...], preferred_element_type=F32)
    hn_ref[...] = _rms(h, g_ref[...]).astype(BF16)
    y_ref[...] = h

    @pl.when(pl.program_id(0) % ns == 0)
    def _():
        prev_ref[...] = jnp.zeros_like(prev_ref)

    rid = lax.broadcasted_iota(jnp.int32, (8, FF_CHUNK), 0)

    def conv(u, p8, w):
        u1 = pltpu.roll(u, 1, axis=0)
        u2 = pltpu.roll(u, 2, axis=0)
        f1 = jnp.where(rid < 1, pltpu.roll(p8, 1, axis=0), u1[0:8])
        f2 = jnp.where(rid < 2, pltpu.roll(p8, 2, axis=0), u2[0:8])
        u1 = jnp.concatenate([f1, u1[8:]], axis=0)
        u2 = jnp.concatenate([f2, u2[8:]], axis=0)
        return w[3:4] + w[0:1] * u2 + w[1:2] * u1 + w[2:3] * u

    def proj_in(c, slot):
        hn = hn_ref[...]
        u_refs[slot][0] = jnp.dot(hn, wa_ref[c], preferred_element_type=F32)
        u_refs[slot][1] = jnp.dot(hn, wg_ref[c], preferred_element_type=F32)

    def gate(c, slot):
        ua, ug = u_refs[slot][0], u_refs[slot][1]
        cw = cw_ref[c]
        pa, pg = prev_ref[c, 0:8], prev_ref[c, 8:16]
        prev_ref[c, 0:8] = ua[tm - 8:tm]
        prev_ref[c, 8:16] = ug[tm - 8:tm]
        ca = conv(ua, pa, cw[0:4])
        cg = conv(ug, pg, cw[4:8])
        act_refs[slot][...] = (ca * jax.nn.sigmoid(ca) * cg).astype(BF16)

    def proj_out(c, slot):
        y_ref[...] += jnp.dot(act_refs[slot][...], wout_ref[c], preferred_element_type=F32)

    proj_in(0, 0)
    for c in range(nchunk):
        if c + 1 < nchunk:
            proj_in(c + 1, (c + 1) % 3)
        gate(c, c % 3)
        if c >= 1:
            proj_out(c - 1, (c - 1) % 3)
    proj_out(nchunk - 1, (nchunk - 1) % 3)
    out = y_ref[...]
    if final:
        out = _rms(out, gf_ref[...])
    out_ref[...] = out


def _ffn_call(h, o_list, wo, gain, wa, wg, cw, wout, seq, final_gain=None):
    t_tokens, d = h.shape
    tm = min(TM, seq)
    ns = seq // tm
    nchunk = wa.shape[0]
    tile = pl.BlockSpec((tm, d), lambda i: (i, 0))
    in_specs = [tile] + [tile] * len(o_list) + [
        _const_spec(wo.shape), _const_spec((1, d)), _const_spec(wa.shape), _const_spec(wg.shape),
        _const_spec(cw.shape), _const_spec(wout.shape)]
    args = [h] + list(o_list) + [wo, gain.reshape(1, d), wa, wg, cw, wout]
    if final_gain is not None:
        in_specs.append(_const_spec((1, d)))
        args.append(final_gain.reshape(1, d))
    return pl.pallas_call(
        functools.partial(_ffn_body, n_o=len(o_list), final=final_gain is not None, tm=tm, ns=ns, nchunk=nchunk),
        grid=(t_tokens // tm,),
        in_specs=in_specs, out_specs=tile,
        out_shape=jax.ShapeDtypeStruct((t_tokens, d), F32),
        scratch_shapes=[pltpu.VMEM((tm, d), BF16), pltpu.VMEM((tm, d), F32),
                        pltpu.VMEM((nchunk, 16, FF_CHUNK), F32)]
        + [pltpu.VMEM((2, tm, FF_CHUNK), F32)] * 3 + [pltpu.VMEM((tm, FF_CHUNK), BF16)] * 3,
        compiler_params=_cparams(("arbitrary",)),
        name="attn_out_ffn",
    )(*args)


def _pad_heads(w, g, width):
    d = w.shape[0]
    w3 = w.reshape(d, g, HEAD_DIM)
    return jnp.pad(w3, ((0, 0), (0, 0), (0, width - HEAD_DIM))).reshape(d, g * width)


def _rot_heads(w, g):
    d = w.shape[0]
    w3 = w.reshape(d, g, HEAD_DIM)
    return jnp.concatenate([-w3[..., HALF:], w3[..., :HALF]], axis=-1).reshape(d, g * HEAD_DIM)


def _rope_k_weights(w, g):
    return jnp.concatenate([_pad_heads(w, g, LANES), _pad_heads(_rot_heads(w, g), g, LANES)], axis=1)


def _rope_tables(seq):
    inv = jnp.float32(ROPE_THETA) ** (-jnp.arange(HALF, dtype=F32) / HALF)

    def cs(pos):
        ang = pos.astype(F32)[:, None] * inv[None, :]
        return jnp.cos(ang), jnp.sin(ang)

    def nat(c):
        return jnp.concatenate([c, c, jnp.zeros((c.shape[0], LANES - HEAD_DIM), F32)], axis=1)

    cos, sin = cs(jnp.arange(seq))
    qscale = HEAD_DIM ** -0.5 * LOG2E
    cc, sc = cs(jnp.arange(seq // CMP_STRIDE) * CMP_STRIDE + CMP_BLOCK - 1)
    return {"cn": nat(cos), "sn": nat(sin), "ct": (cos * qscale).T, "st": (sin * qscale).T,
            "cc": nat(cc), "sc": nat(sc)}


def _ffn_weights(w_in, conv_w, conv_b, w_out):
    d, two_ff = w_in.shape
    dff = two_ff // 2
    assert conv_w.shape[0] == CONV_WIDTH and dff % FF_CHUNK == 0
    nchunk = dff // FF_CHUNK

    def chunks(w):
        return w.reshape(d, nchunk, FF_CHUNK).transpose(1, 0, 2).astype(BF16)

    wa, wg = chunks(w_in[:, :dff]), chunks(w_in[:, dff:])
    taps = jnp.concatenate([conv_w, conv_b[None, :]], axis=0)
    cw = jnp.concatenate([taps[:, :dff].reshape(4, nchunk, FF_CHUNK), taps[:, dff:].reshape(4, nchunk, FF_CHUNK)],
                         axis=0).transpose(1, 0, 2)
    return wa, wg, cw, w_out.reshape(nchunk, FF_CHUNK, d).astype(BF16)


def _nsa_attention(h, gain, w_in, cmp_pos, cmp_w1, cmp_w2, tabs, emat, batch, seq):
    g, r = A_KV_HEADS, N_HEADS // A_KV_HEADS
    d = h.shape[1]
    kvw = g * HEAD_DIM
    nq = N_HEADS * HEAD_DIM
    wq, wkc, wvc, wks, wvs, wkw, wvw, wgl = jnp.split(
        w_in, [nq, nq + kvw, nq + 2 * kvw, nq + 3 * kvw, nq + 4 * kvw, nq + 5 * kvw, nq + 6 * kvw], axis=1)
    wn = jnp.concatenate([_rope_k_weights(wks, g), _rope_k_weights(wkw, g), wkc, wvc], axis=1).astype(BF16)
    wgl = wgl.reshape(d, N_HEADS, 3).transpose(0, 2, 1).reshape(d, 3 * N_HEADS)
    wt = jnp.concatenate([wq, wgl, _pad_heads(wvs, g, VROWS), _pad_heads(wvw, g, VROWS)], axis=1).T.astype(BF16)
    kpad = g * LANES
    nat_plan = [("rope", 0, kpad), ("rope", 2 * kpad, kpad), ("heads", 4 * kpad, kvw), ("heads", 4 * kpad + kvw, kvw)]
    ng = 3 * N_HEADS
    tr_plan = [("ropeq", 0, nq), ("sigmoid", nq, ng), ("vaug", nq + ng, g * VROWS),
               ("vaug", nq + ng + g * VROWS, g * VROWS)]
    ks, kw, kc, vc, qt, gates, vst, vwt = _proj_call(
        h, gain, tabs, wn, wt, nat_plan, tr_plan, [BF16, BF16, F32, F32], batch, seq)
    ks = ks.reshape(batch, seq, kpad)
    kw = kw.reshape(batch, seq, kpad)
    gates = gates.reshape(batch, 3, g, r, seq)

    nseg = seq // CMP_STRIDE
    nb = seq // SEL_BLOCK
    seg_f = CMP_STRIDE * HEAD_DIM

    def segs(x):
        return x.reshape(batch, g, nseg, seg_f)

    pos = cmp_pos.reshape(2, 2, 1, seg_f)
    w1 = cmp_w1.reshape(2, 2, seg_f, cmp_w1.shape[-1]).astype(BF16)
    w2k = jnp.stack([_pad_heads(cmp_w2[0], 1, LANES), _pad_heads(_rot_heads(cmp_w2[0], 1), 1, LANES)]).astype(BF16)
    w2v = _pad_heads(cmp_w2[1], 1, VROWS).T.astype(BF16)
    kcc, vcc = _compress_call(segs(kc), segs(vc), pos, w1, w2k, w2v, tabs["cc"], tabs["sc"])
    ch = min(CMP_CHUNK, nb)
    kcc = kcc.reshape(batch, g, nb // ch, ch, 4, LANES).transpose(0, 1, 2, 4, 3, 5).reshape(batch, g, nseg, LANES)
    vcc = vcc.reshape(batch, g, VROWS, nb // ch, ch, 4).transpose(0, 1, 2, 3, 5, 4).reshape(batch, g, VROWS, nseg)

    o_c, mb = _cmp_call(qt, kcc, vcc, gates, batch, seq)
    o_sw = _attn_call("sel", qt, ks, vst, batch, seq, g, gates=gates, branch=1, mb=mb, emat=emat,
                      kw=kw, vwt=vwt, window=WINDOW_A)
    t_tokens = batch * seq
    return [o.reshape(t_tokens, nq) for o in (o_c, o_sw)]


def kernel(x, norm_attn, norm_ffn, a_w_in, a_cmp_pos, a_cmp_w1, a_cmp_w2, a_w_out, kv_norm, b_w_kv, b_w_q, b_sinks,
           b_w_out, ffn_w_in, ffn_conv_w, ffn_conv_b, ffn_w_out, final_norm):
    batch, seq, d = x.shape
    depth = norm_attn.shape[0]
    n_a = a_w_in.shape[0]
    tabs = _rope_tables(seq)
    mr = min(MASK_ROWS, seq // SEL_BLOCK)
    emat = (jnp.arange(mr * SEL_BLOCK)[:, None] // SEL_BLOCK == jnp.arange(mr)[None, :]).astype(BF16)
    h = x.reshape(batch * seq, d)
    k_sh = v_sh = None
    gb, rb = B_KV_HEADS, N_HEADS // B_KV_HEADS
    for layer in range(depth):
        if layer < n_a:
            o_list = _nsa_attention(h, norm_attn[layer], a_w_in[layer], a_cmp_pos[layer], a_cmp_w1[layer],
                                    a_cmp_w2[layer], tabs, emat, batch, seq)
            wo = a_w_out[layer]
        else:
            j = layer - n_a
            if k_sh is None:
                wk, wv = jnp.split(b_w_kv, 2, axis=1)
                kpad = gb * LANES
                k_sh, v_sh = _proj_call(
                    h, kv_norm, tabs, _rope_k_weights(wk, gb).astype(BF16), _pad_heads(wv, gb, VROWS).T.astype(BF16),
                    [("rope", 0, kpad)], [("vaug", 0, gb * VROWS)], [BF16], batch, seq)
                k_sh = k_sh.reshape(batch, seq, kpad)
            (qt,) = _proj_call(h, norm_attn[layer], tabs, None, b_w_q[j].T.astype(BF16), [],
                               [("ropeq", 0, N_HEADS * HEAD_DIM)], [], batch, seq)
            sinks = jnp.broadcast_to((b_sinks[j] * LOG2E).reshape(gb, 1, rb, 1),
                                     (gb, 1, rb, TQ_WINDOW)).reshape(gb, 1, rb * TQ_WINDOW)
            o = _attn_call("swa", qt, k_sh, v_sh, batch, seq, gb, sinks=sinks, window=WINDOW_B)
            o_list = [o.reshape(batch * seq, N_HEADS * HEAD_DIM)]
            wo = b_w_out[j]
        wa, wg, cw, wout = _ffn_weights(ffn_w_in[layer], ffn_conv_w[layer], ffn_conv_b[layer], ffn_w_out[layer])
        h = _ffn_call(h, o_list, wo.astype(BF16), norm_ffn[layer], wa, wg, cw, wout, seq,
                      final_gain=final_norm if layer == depth - 1 else None)
    return h.reshape(batch, seq, d)
```

```python
import functools
import math

import jax
import jax.numpy as jnp
from jax import lax
from jax.experimental import pallas as pl
from jax.experimental.pallas import tpu as pltpu

F32 = jnp.float32
BF16 = jnp.bfloat16

HEAD_DIM = 64
HALF = HEAD_DIM // 2
N_HEADS = 16
A_KV_HEADS = 4
B_KV_HEADS = 2
CMP_BLOCK = 32
CMP_STRIDE = 16
SEL_BLOCK = 64
N_SELECT = 16
WINDOW_A = 512
WINDOW_B = 128
CONV_WIDTH = 3
ROPE_THETA = 10000.0
EPS = 1e-6
FORCE = 1e6

LANES = 128
BF16_SUBLANES = 16
VMEM_LIMIT = 56 * 1024 * 1024
TQ = 256
TQ_WINDOW = 256
TK = 256
SEL_GROUP = 4
CMP_CHUNK = 64
TM = 512
FF_CHUNK = 256
VROWS = HEAD_DIM + BF16_SUBLANES
MASK_ROWS = 128

LOG2E = 1.4426950408889634
NEG = -1e30
M_INIT = -1e29
REMOVED = -3e38

_NT = (((1,), (1,)), ((), ()))


def _cparams(sem):
    return pltpu.CompilerParams(dimension_semantics=sem, vmem_limit_bytes=VMEM_LIMIT)


def _const_spec(shape):
    n = len(shape)
    return pl.BlockSpec(shape, lambda *_: (0,) * n, pipeline_mode=pl.Buffered(1))


def _rms(x, g):
    ms = jnp.mean(x * x, axis=-1, keepdims=True)
    return x * lax.rsqrt(ms + EPS) * g


def _proj_body(*refs, nat_plan, tr_plan, tm):
    it = iter(refs)
    h_ref, g_ref = next(it), next(it)
    if nat_plan:
        cn_ref, sn_ref, wn_ref = next(it), next(it), next(it)
    ct_ref, st_ref, wt_ref = next(it), next(it), next(it)
    outs = list(it)

    hn = _rms(h_ref[...], g_ref[...]).astype(BF16)
    oi = 0
    for kind, c0, n in nat_plan:
        o_ref = outs[oi]
        oi += 1
        y = jnp.dot(hn, wn_ref[:, c0:c0 + n], preferred_element_type=F32)
        if kind == "rope":
            lane = lax.broadcasted_iota(jnp.int32, (1, n), 1) & (LANES - 1)
            yr = jnp.where(lane < HALF, -pltpu.roll(y, n - HALF, axis=1), pltpu.roll(y, HALF, axis=1))
            c, s = cn_ref[...], sn_ref[...]
            for g in range(n // LANES):
                sl = slice(LANES * g, LANES * (g + 1))
                o_ref[:, sl] = (y[:, sl] * c + yr[:, sl] * s).astype(o_ref.dtype)
        elif kind == "heads":
            for g in range(n // HEAD_DIM):
                o_ref[g] = y[:, HEAD_DIM * g:HEAD_DIM * (g + 1)].astype(o_ref.dtype)
        else:
            o_ref[...] = y.astype(o_ref.dtype)
    for kind, r0, n in tr_plan:
        o_ref = outs[oi]
        oi += 1
        y = lax.dot_general(wt_ref[r0:r0 + n, :], hn, _NT, preferred_element_type=F32)
        if kind == "ropeq":
            c, s = ct_ref[...], st_ref[...]
            for hd in range(n // HEAD_DIM):
                a = HEAD_DIM * hd
                y1, y2 = y[a:a + HALF], y[a + HALF:a + HEAD_DIM]
                o_ref[a:a + HALF, :] = (y1 * c - y2 * s).astype(o_ref.dtype)
                o_ref[a + HALF:a + HEAD_DIM, :] = (y2 * c + y1 * s).astype(o_ref.dtype)
        elif kind == "sigmoid":
            o_ref[...] = jax.nn.sigmoid(y)
        else:
            rows = lax.broadcasted_iota(jnp.int32, (VROWS, tm), 0)
            for g in range(n // VROWS):
                yg = jnp.where(rows == HEAD_DIM, 1.0, y[VROWS * g:VROWS * (g + 1)]).astype(o_ref.dtype)
                for t in range(tm // TK):
                    o_ref[t, VROWS * g:VROWS * (g + 1), :] = yg[:, t * TK:(t + 1) * TK]


def _proj_call(h, gain, tabs, wn, wt, nat_plan, tr_plan, nat_dtypes, batch, seq):
    t_tokens, d = h.shape
    tm = min(TM, seq)
    ns = seq // tm
    in_specs = [pl.BlockSpec((tm, d), lambda i: (i, 0)), _const_spec((1, d))]
    args = [h, gain.reshape(1, d)]
    if nat_plan:
        in_specs += [pl.BlockSpec((tm, LANES), lambda i: (i % ns, 0)),
                     pl.BlockSpec((tm, LANES), lambda i: (i % ns, 0)),
                     _const_spec(wn.shape)]
        args += [tabs["cn"], tabs["sn"], wn]
    in_specs += [pl.BlockSpec((HALF, tm), lambda i: (0, i % ns)),
                 pl.BlockSpec((HALF, tm), lambda i: (0, i % ns)),
                 _const_spec(wt.shape)]
    args += [tabs["ct"], tabs["st"], wt]
    out_shape, out_specs = [], []
    for (kind, _, n), dt in zip(nat_plan, nat_dtypes):
        if kind == "heads":
            nh = n // HEAD_DIM
            out_shape.append(jax.ShapeDtypeStruct((batch, nh, seq, HEAD_DIM), dt))
            out_specs.append(pl.BlockSpec((None, nh, tm, HEAD_DIM), lambda i: (i // ns, 0, i % ns, 0)))
        else:
            out_shape.append(jax.ShapeDtypeStruct((t_tokens, n), dt))
            out_specs.append(pl.BlockSpec((tm, n), lambda i: (i, 0)))
    for kind, _, n in tr_plan:
        if kind == "vaug":
            out_shape.append(jax.ShapeDtypeStruct((batch, seq // TK, n, TK), BF16))
            out_specs.append(pl.BlockSpec((None, tm // TK, n, TK), lambda i: (i // ns, i % ns, 0, 0)))
        else:
            dt = F32 if kind == "sigmoid" else BF16
            out_shape.append(jax.ShapeDtypeStruct((batch, n, seq), dt))
            out_specs.append(pl.BlockSpec((None, n, tm), lambda i: (i // ns, 0, i % ns)))
    return pl.pallas_call(
        functools.partial(_proj_body, nat_plan=tuple(nat_plan), tr_plan=tuple(tr_plan), tm=tm),
        grid=(t_tokens // tm,),
        in_specs=in_specs, out_specs=out_specs, out_shape=out_shape,
        compiler_params=_cparams(("parallel",)),
        name="norm_proj",
    )(*args)


def _gelu_tanh(x):
    c = math.sqrt(2.0 / math.pi)
    return x * (0.5 * (1.0 + jnp.tanh(c * (x + 0.044715 * (x * x * x)))))


def _compress_body(sk_ref, sv_ref, pos_ref, w1_ref, w2k_ref, w2v_ref, cc_ref, sc_ref, ok_ref, ov_ref, *, ncp):
    def hidden(seg, kv):
        xa = (seg + pos_ref[kv, 0]).astype(BF16)
        xb = (seg + pos_ref[kv, 1]).astype(BF16)
        a = jnp.dot(xa, w1_ref[kv, 0], preferred_element_type=F32)
        b = jnp.dot(xb, w1_ref[kv, 1], preferred_element_type=F32)
        return _gelu_tanh(a + pltpu.roll(b, ncp - 1, axis=0)).astype(BF16)

    gk = hidden(sk_ref[...], 0)
    k = jnp.dot(gk, w2k_ref[0], preferred_element_type=F32)
    kr = jnp.dot(gk, w2k_ref[1], preferred_element_type=F32)
    ok_ref[...] = (k * cc_ref[...] + kr * sc_ref[...]).astype(ok_ref.dtype)
    gv = hidden(sv_ref[...], 1)
    vt = lax.dot_general(w2v_ref[...], gv, _NT, preferred_element_type=F32)
    rows = lax.broadcasted_iota(jnp.int32, (VROWS, ncp), 0)
    ov_ref[...] = jnp.where(rows == HEAD_DIM, 1.0, vt).astype(ov_ref.dtype)


def _compress_call(segk, segv, pos, w1, w2k, w2v, cc, sc):
    b, g, ncp, f = segk.shape
    seg_spec = pl.BlockSpec((None, None, ncp, f), lambda i, j: (i, j, 0, 0))
    return pl.pallas_call(
        functools.partial(_compress_body, ncp=ncp),
        grid=(b, g),
        in_specs=[seg_spec, seg_spec, _const_spec(pos.shape), _const_spec(w1.shape), _const_spec(w2k.shape),
                  _const_spec(w2v.shape), _const_spec(cc.shape), _const_spec(sc.shape)],
        out_specs=[pl.BlockSpec((None, None, ncp, LANES), lambda i, j: (i, j, 0, 0)),
                   pl.BlockSpec((None, None, VROWS, ncp), lambda i, j: (i, j, 0, 0))],
        out_shape=[jax.ShapeDtypeStruct((b, g, ncp, LANES), BF16),
                   jax.ShapeDtypeStruct((b, g, VROWS, ncp), BF16)],
        compiler_params=_cparams(("parallel", "parallel")),
        name="compress",
    )(segk, segv, pos, w1, w2k, w2v, cc, sc)


def _load_queries(q_ref, qa_ref, r, tq):
    for rr in range(r):
        qa_ref[0:HEAD_DIM, rr * tq:(rr + 1) * tq] = q_ref[HEAD_DIM * rr:HEAD_DIM * (rr + 1), :]
    qa_ref[HEAD_DIM:LANES, :] = jnp.zeros((LANES - HEAD_DIM, r * tq), BF16)


def _gated_heads(acc, gate_ref, r, tq, colscale=None):
    inv = 1.0 / acc[HEAD_DIM:HEAD_DIM + 1, :]
    if colscale is not None:
        inv = inv * colscale
    o = acc[0:HEAD_DIM, :] * inv
    parts = []
    for rr in range(r):
        z = o[:, rr * tq:(rr + 1) * tq]
        if gate_ref is not None:
            z = z * gate_ref[rr:rr + 1, :]
        parts.append(z)
    return jnp.concatenate(parts, axis=0)


def _store_heads(o_ref, acc, gate_ref, r, tq, colscale=None):
    o_ref[...] = _gated_heads(acc, gate_ref, r, tq, colscale).T


def _window_tiles(window, tq):
    nt = max((s0 + tq - 1) // TK - (s0 - window + 1) // TK + 1 for s0 in range(8 * window, 8 * window + TK, tq))
    return nt, tuple((i == nt - 1, (nt - i) * TK - 1 >= window) for i in range(nt))


def _window_attend(qa_of, k_ref, v_ref, j0, u, checks, *, window, start, tq, r, m0_of=None, acc0=None,
                   between=None, skip=0):
    row0 = pl.multiple_of(j0 * TK, TK)
    kt = k_ref[pl.ds(pl.multiple_of(row0 + skip, LANES), u * TK - skip), :]
    vt = jnp.concatenate([v_ref[j0][:, skip:]] + [v_ref[j0 + i] for i in range(1, u)], axis=1)
    bounds = [(max(i * TK - skip, 0), (i + 1) * TK - skip) for i in range(u)]
    d0 = (start - row0) + (lax.broadcasted_iota(jnp.int32, (TK, tq), 1)
                           - lax.broadcasted_iota(jnp.int32, (TK, tq), 0))
    oks = []
    for i in range(u):
        lower, upper = checks[i]
        if lower and upper:
            oks.append((d0 >= i * TK) & (d0 < window + i * TK))
        elif lower:
            oks.append(d0 >= i * TK)
        elif upper:
            oks.append(d0 < window + i * TK)
        else:
            oks.append(None)
    if skip and oks[0] is not None:
        oks[0] = oks[0][skip:]
    cw = max(tq, 2 * LANES)
    chains = [slice(c0, c0 + cw) for c0 in range(0, r * tq, cw)]
    scores = [jnp.dot(kt, qa_of(ch), preferred_element_type=F32) for ch in chains]
    if between is not None:
        between()

    def masked(si, ok):
        if ok is None:
            return si
        return jnp.concatenate([jnp.where(ok, si[:, c0:c0 + tq], NEG) for c0 in range(0, cw, tq)], axis=1)

    outs = []
    for ch, s in zip(chains, scores):
        s = jnp.concatenate([masked(s[lo:hi], oks[i]) for i, (lo, hi) in enumerate(bounds)], axis=0)
        m = jnp.max(s, axis=0, keepdims=True)
        if m0_of is not None:
            m = jnp.maximum(m, m0_of(ch))
        acc = jnp.dot(vt, jnp.exp2(s - m).astype(BF16), preferred_element_type=F32)
        if acc0 is not None:
            acc = acc + acc0 * jnp.exp2(m0_of(ch) - m)
        outs.append(acc)
    return jnp.concatenate(outs, axis=1)


def _cmp_body(q_ref, k_ref, v_ref, gate_ref, o_ref, mb_ref, qa_ref, *, tq, nb, r):
    qi = pl.program_id(2)
    _load_queries(q_ref, qa_ref, r, tq)
    ch = min(CMP_CHUNK, nb)
    buckets = [b for b in range(ch, nb, ch) if b >= N_SELECT] + [nb]
    args = (q_ref, k_ref, v_ref, gate_ref, o_ref, mb_ref, qa_ref, qi)

    @pl.when(qi == 0)
    def _():
        _cmp_rows(*args, tq=tq, nb=nb, r=r, rows=buckets[0], forced_distinct=False)

    lo = 1
    for rows in buckets:
        hi = rows * SEL_BLOCK // tq if rows < nb else pl.num_programs(2)

        @pl.when((qi >= lo) & (qi < hi))
        def _(rows=rows):
            _cmp_rows(*args, tq=tq, nb=nb, r=r, rows=rows, forced_distinct=True)
        lo = hi


def _cmp_rows(q_ref, k_ref, v_ref, gate_ref, o_ref, mb_ref, qa_ref, qi, *, tq, nb, r, rows, forced_distinct):
    lw = r * tq
    ch = min(CMP_CHUNK, nb)
    s = jnp.dot(k_ref[0:4 * rows, :], qa_ref[...], preferred_element_type=F32)
    if rows < nb:
        mb_ref[rows:nb, :] = jnp.full((nb - rows, tq), NEG, mb_ref.dtype)
    nb = rows
    i_io = lax.broadcasted_iota(jnp.int32, (ch, tq), 0)
    lim = qi * tq + lax.broadcasted_iota(jnp.int32, (ch, tq), 1) - (CMP_BLOCK - 1)
    pieces = []
    for c in range(nb // ch):
        for m in range(4):
            valid = SEL_BLOCK * i_io <= lim - (SEL_BLOCK * c * ch + CMP_STRIDE * m)
            r0 = (4 * c + m) * ch
            pieces.append(jnp.concatenate(
                [jnp.where(valid, s[r0:r0 + ch, rr * tq:(rr + 1) * tq], NEG) for rr in range(r)], axis=1))
    mx = pieces[0]
    for x in pieces[1:]:
        mx = jnp.maximum(mx, x)
    mx = jnp.max(mx, axis=0, keepdims=True)
    pm = [jnp.exp2(x - mx) for x in pieces]
    tot = pm[0]
    for x in pm[1:]:
        tot = tot + x
    den = jnp.sum(tot, axis=0, keepdims=True)
    t_lane = qi * tq + (lax.broadcasted_iota(jnp.int32, (1, lw), 1) & (tq - 1))
    has_key = jnp.where(t_lane >= CMP_BLOCK - 1, 1.0, 0.0)
    acc = jnp.dot(v_ref[:, 0:4 * rows], jnp.concatenate(pm, axis=0).astype(BF16), preferred_element_type=F32)
    _store_heads(o_ref, acc, gate_ref, r, tq, has_key)

    inv = has_key / den
    ps = []
    for m in range(4):
        slabs = []
        for c in range(nb // ch):
            pn = pm[4 * c + m] * inv
            acc_h = pn[:, 0:tq]
            for rr in range(1, r):
                acc_h = acc_h + pn[:, rr * tq:(rr + 1) * tq]
            slabs.append(acc_h)
        ps.append(jnp.concatenate(slabs, axis=0))
    j_io = lax.broadcasted_iota(jnp.int32, (nb, tq), 0)
    tt = qi * tq + lax.broadcasted_iota(jnp.int32, (nb, tq), 1)
    prev3 = jnp.where(j_io == 0, 0.0, pltpu.roll(ps[3], 1, axis=0))
    imp = prev3 + 2.0 * (ps[0] + ps[1] + ps[2]) + ps[3]
    cur = tt >> (SEL_BLOCK.bit_length() - 1)
    forced = (j_io == 0) | (j_io == cur) | (j_io == cur - 1)
    causal = j_io * SEL_BLOCK <= tt
    v = jnp.where(forced, REMOVED if forced_distinct else FORCE, jnp.where(causal, imp, -FORCE))
    jf = j_io.astype(F32)
    for _ in range(min(N_SELECT, nb) - (3 if forced_distinct else 0)):
        top = jnp.max(v, axis=0, keepdims=True)
        idx = jnp.min(jnp.where(v == top, jf, float(nb)), axis=0, keepdims=True)
        v = jnp.where(jf == idx, REMOVED, v)
    mb_ref[0:nb, :] = jnp.where(v == REMOVED, 0.0, NEG).astype(mb_ref.dtype)


def _cmp_call(qt, kc, vct, gates, batch, seq):
    g, r = A_KV_HEADS, N_HEADS // A_KV_HEADS
    nb = seq // SEL_BLOCK
    ncp = 4 * nb
    tq = TQ_WINDOW
    return pl.pallas_call(
        functools.partial(_cmp_body, tq=tq, nb=nb, r=r),
        grid=(batch, g, seq // tq),
        in_specs=[pl.BlockSpec((None, r * HEAD_DIM, tq), lambda b, gg, q: (b, gg, q)),
                  pl.BlockSpec((None, None, ncp, LANES), lambda b, gg, q: (b, gg, 0, 0)),
                  pl.BlockSpec((None, None, VROWS, ncp), lambda b, gg, q: (b, gg, 0, 0)),
                  pl.BlockSpec((None, None, None, r, tq), lambda b, gg, q: (b, 0, gg, 0, q))],
        out_specs=[pl.BlockSpec((None, tq, r * HEAD_DIM), lambda b, gg, q: (b, q, gg)),
                   pl.BlockSpec((None, None, nb, tq), lambda b, gg, q: (b, gg, 0, q))],
        out_shape=[jax.ShapeDtypeStruct((batch, seq, N_HEADS * HEAD_DIM), F32),
                   jax.ShapeDtypeStruct((batch, g, nb, seq), BF16)],
        scratch_shapes=[pltpu.VMEM((LANES, r * tq), BF16)],
        compiler_params=_cparams(("parallel", "parallel", "parallel")),
        name="cmp_attn_topk",
    )(qt, kc, vct, gates)


def _sel_body(q_ref, k_ref, v_ref, gate_ref, mb_ref, e_ref, kw_ref, vw_ref, gatew_ref, o_ref,
              qa_ref, acc_ref, accw_ref, m_ref, macc_ref,
              s0_ref, s1_ref, p0_ref, p1_ref, mt0_ref, mt1_ref, cm0_ref, cm1_ref, *, tq, r, mr, nb, window):
    s_refs, p_refs, mt_refs, cm_refs = (s0_ref, s1_ref), (p0_ref, p1_ref), (mt0_ref, mt1_ref), (cm0_ref, cm1_ref)
    qi = pl.program_id(2)
    lw = r * tq
    start = qi * tq
    u = p0_ref.shape[0]
    gph = mr * SEL_BLOCK // (u * TK)
    assert tq == TK
    jd = start // TK
    gd = jd // u
    nh = nb // mr

    _load_queries(q_ref, qa_ref.at[0], r, tq)
    for hf in range(1, nh):
        qa_ref[hf, 0:LANES, :] = qa_ref[0, 0:LANES, :]
    for hf in range(nh):
        for rr in range(r):
            qa_ref[hf, LANES:LANES + mr, rr * tq:(rr + 1) * tq] = mb_ref[hf * mr:(hf + 1) * mr, :]
    acc_ref[...] = jnp.zeros((VROWS, lw), F32)

    def key_tile(gi, i):
        row0 = pl.multiple_of(gi * (u * TK), u * TK)
        eoff = pl.multiple_of(row0 % (mr * SEL_BLOCK), u * TK)
        return jnp.concatenate([k_ref[pl.ds(row0 + i * TK, TK), :], e_ref[pl.ds(eoff + i * TK, TK), :]], axis=1)

    def causal(gi, i):
        t_io = start + (lax.broadcasted_iota(jnp.int32, (TK, lw), 1) & (tq - 1))
        return (gi * u + i) * TK + lax.broadcasted_iota(jnp.int32, (TK, lw), 0) <= t_io

    def pipeline(stage_a, stage_b, stage_c):
        def trip(gi, slot):
            stage_a(gi + 1, 1 - slot)
            stage_b(gi, slot, None)
            stage_c(gi - 1, 1 - slot, u)

        def last(slot, has_prev):
            for kd in range(u):
                @pl.when(jd % u == kd)
                def _(kd=kd):
                    stage_b(gd, slot, kd)
                    if has_prev:
                        stage_c(gd - 1, 1 - slot, u)
                    stage_c(gd, slot, kd + 1)

        nt, _ = _window_tiles(window, tq)
        accw_ref[...] = _window_attend(lambda ch: qa_ref[0, 0:LANES, ch], kw_ref, vw_ref,
                                       jnp.maximum(start // TK - (nt - 1), 0), nt, ((True, True),) * nt,
                                       window=window, start=start, tq=tq, r=r, between=lambda: stage_a(0, 0))

        @pl.when(gd == 0)
        def _():
            last(0, False)

        @pl.when(gd > 0)
        def _():
            stage_a(1, 1)
            stage_b(0, 0, None)
            n = gd - 1

            def body(pi, c):
                trip(1 + 2 * pi, 1)
                trip(2 + 2 * pi, 0)
                return c
            lax.fori_loop(0, n // 2, body, 0)

            @pl.when(n % 2 == 1)
            def _():
                trip(gd - 1, 1)
                last(0, True)

            @pl.when(n % 2 == 0)
            def _():
                last(1, True)

    m_ref[...] = jnp.full((1, lw), M_INIT, F32)
    macc_ref[...] = jnp.full((1, lw), M_INIT, F32)

    def scores(gi, slot):
        qa = qa_ref[gi // gph]
        for i in range(u):
            s = jnp.dot(key_tile(gi, i), qa, preferred_element_type=F32)
            s_refs[slot][i] = s
            cm_refs[slot][i] = jnp.max(s, axis=0, keepdims=True)

    def softmax(gi, slot, diag):
        m = m_ref[...]
        for i in range(u if diag is None else diag + 1):
            if i == diag:
                s = jnp.where(causal(gi, i), s_refs[slot][i], NEG)
                m = jnp.maximum(m, jnp.max(s, axis=0, keepdims=True))
            else:
                m = jnp.maximum(m, cm_refs[slot][i])
                s = s_refs[slot][i]
            p_refs[slot][i] = jnp.exp2(s - m).astype(BF16)
            mt_refs[slot][i] = m
        m_ref[...] = m

    def values(gi, slot, ntiles):
        acc, ma = acc_ref[...], macc_ref[...]
        for i in range(ntiles):
            mt = mt_refs[slot][i]
            acc = acc * jnp.exp2(ma - mt) + jnp.dot(v_ref[gi * u + i], p_refs[slot][i],
                                                    preferred_element_type=F32)
            ma = mt
        acc_ref[...], macc_ref[...] = acc, ma

    pipeline(scores, softmax, values)
    o_ref[...] = (_gated_heads(acc_ref[...], gate_ref, r, tq) + _gated_heads(accw_ref[...], gatew_ref, r, tq)).T


def _swa_body(q_ref, k_ref, v_ref, sink_ref, o_ref, qa_ref, acc_ref, *, tq, r, window):
    m0_of = lambda ch: sink_ref[:, ch]
    acc0 = jnp.where(lax.broadcasted_iota(jnp.int32, (VROWS, 1), 0) == HEAD_DIM, 1.0, 0.0)
    qi = pl.program_id(2)
    start = qi * tq
    jd = start // TK
    _load_queries(q_ref, qa_ref, r, tq)

    def step(j0, checks, skip):
        acc_ref[...] = _window_attend(lambda ch: qa_ref[:, ch], k_ref, v_ref, j0, nt, checks, window=window,
                                      start=start, tq=tq, r=r, m0_of=m0_of, acc0=acc0, skip=skip)

    nt, roles = _window_tiles(window, tq)
    skip = (nt * TK - (tq + window - 1)) // LANES * LANES if tq % TK == 0 else 0

    @pl.when(jd >= nt - 1)
    def _():
        step(jd - (nt - 1), roles, skip)

    @pl.when(jd < nt - 1)
    def _():
        step(0, ((True, True),) * nt, 0)

    _store_heads(o_ref, acc_ref[...], None, r, tq)


def _attn_call(mode, qt, k, vt, batch, seq, g, gates=None, branch=None, mb=None, emat=None, sinks=None,
               window=None, kw=None, vwt=None):
    r = N_HEADS // g
    nb = seq // SEL_BLOCK
    mr = emat.shape[1] if mode == "sel" else 0
    nkt = seq // TK
    kc = LANES + mr
    tq = TQ if mode == "sel" else TQ_WINDOW
    k_spec = pl.BlockSpec((None, seq, LANES), lambda b, gg, q: (b, 0, gg))
    v_spec = pl.BlockSpec((None, nkt, VROWS, TK), lambda b, gg, q: (b, 0, gg, 0))
    in_specs = [pl.BlockSpec((None, r * HEAD_DIM, tq), lambda b, gg, q: (b, gg, q)), k_spec, v_spec]
    args = [qt, k, vt]
    if mode == "sel":
        in_specs += [pl.BlockSpec((None, None, None, r, tq), lambda b, gg, q: (b, branch, gg, 0, q)),
                     pl.BlockSpec((None, None, nb, tq), lambda b, gg, q: (b, gg, 0, q)), _const_spec(emat.shape),
                     k_spec, v_spec,
                     pl.BlockSpec((None, None, None, r, tq), lambda b, gg, q: (b, branch + 1, gg, 0, q))]
        args += [gates, mb, emat, kw, vwt, gates]
    else:
        in_specs.append(pl.BlockSpec((None, 1, r * tq), lambda b, gg, q: (gg, 0, 0)))
        args.append(sinks)
    lw = r * tq
    scratch = [pltpu.VMEM((kc, lw), BF16), pltpu.VMEM((VROWS, lw), F32)]
    if mode == "sel":
        u = min(SEL_GROUP, mr * SEL_BLOCK // TK)
        scratch[0] = pltpu.VMEM((nb // mr, kc, lw), BF16)
        scratch += ([pltpu.VMEM((VROWS, lw), F32)] + [pltpu.VMEM((1, lw), F32)] * 2
                    + [pltpu.VMEM((u, TK, lw), F32)] * 2
                    + [pltpu.VMEM((u, TK, lw), BF16)] * 2 + [pltpu.VMEM((u, 1, lw), F32)] * 4)
        body = functools.partial(_sel_body, tq=tq, r=r, mr=mr, nb=nb, window=window)
    else:
        body = functools.partial(_swa_body, tq=tq, r=r, window=window)
    return pl.pallas_call(
        body,
        grid=(batch, g, seq // tq),
        in_specs=in_specs,
        out_specs=pl.BlockSpec((None, tq, r * HEAD_DIM), lambda b, gg, q: (b, q, gg)),
        out_shape=jax.ShapeDtypeStruct((batch, seq, N_HEADS * HEAD_DIM), F32),
        scratch_shapes=scratch,
        compiler_params=_cparams(("parallel", "parallel", "arbitrary")),
        name=mode + "_attn",
    )(*args)


def _ffn_body(*refs, n_o, final, tm, ns, nchunk):
    it = iter(refs)
    h_ref = next(it)
    o_refs = [next(it) for _ in range(n_o)]
    wo_ref, g_ref, wa_ref, wg_ref, cw_ref, wout_ref = (next(it) for _ in range(6))
    gf_ref = next(it) if final else None
    out_ref, hn_ref, y_ref, prev_ref = (next(it) for _ in range(4))
    u_refs = [next(it) for _ in range(3)]
    act_refs = [next(it) for _ in range(3)]

    osum = o_refs[0][...]
    for o_ref in o_refs[1:]:
        osum = osum + o_ref[...]
    h = h_ref[...] + jnp.dot(osum.astype(BF16), wo_ref[...], preferred_element_type=F32)
    hn_ref[...] = _rms(h, g_ref[...]).astype(BF16)
    y_ref[...] = h

    @pl.when(pl.program_id(0) % ns == 0)
    def _():
        prev_ref[...] = jnp.zeros_like(prev_ref)

    rid = lax.broadcasted_iota(jnp.int32, (8, FF_CHUNK), 0)

    def conv(u, p8, w):
        u1 = pltpu.roll(u, 1, axis=0)
        u2 = pltpu.roll(u, 2, axis=0)
        f1 = jnp.where(rid < 1, pltpu.roll(p8, 1, axis=0), u1[0:8])
        f2 = jnp.where(rid < 2, pltpu.roll(p8, 2, axis=0), u2[0:8])
        u1 = jnp.concatenate([f1, u1[8:]], axis=0)
        u2 = jnp.concatenate([f2, u2[8:]], axis=0)
        return w[3:4] + w[0:1] * u2 + w[1:2] * u1 + w[2:3] * u

    def proj_in(c, slot):
        hn = hn_ref[...]
        u_refs[slot][0] = jnp.dot(hn, wa_ref[c], preferred_element_type=F32)
        u_refs[slot][1] = jnp.dot(hn, wg_ref[c], preferred_element_type=F32)

    def gate(c, slot):
        ua, ug = u_refs[slot][0], u_refs[slot][1]
        cw = cw_ref[c]
        pa, pg = prev_ref[c, 0:8], prev_ref[c, 8:16]
        prev_ref[c, 0:8] = ua[tm - 8:tm]
        prev_ref[c, 8:16] = ug[tm - 8:tm]
        ca = conv(ua, pa, cw[0:4])
        cg = conv(ug, pg, cw[4:8])
        act_refs[slot][...] = (ca * jax.nn.sigmoid(ca) * cg).astype(BF16)

    def proj_out(c, slot):
        y_ref[...] += jnp.dot(act_refs[slot][...], wout_ref[c], preferred_element_type=F32)

    proj_in(0, 0)
    for c in range(nchunk):
        if c + 1 < nchunk:
            proj_in(c + 1, (c + 1) % 3)
        gate(c, c % 3)
        if c >= 1:
            proj_out(c - 1, (c - 1) % 3)
    proj_out(nchunk - 1, (nchunk - 1) % 3)
    out = y_ref[...]
    if final:
        out = _rms(out, gf_ref[...])
    out_ref[...] = out


def _ffn_call(h, o_list, wo, gain, wa, wg, cw, wout, seq, final_gain=None):
    t_tokens, d = h.shape
    tm = min(TM, seq)
    ns = seq // tm
    nchunk = wa.shape[0]
    tile = pl.BlockSpec((tm, d), lambda i: (i, 0))
    in_specs = [tile] + [tile] * len(o_list) + [
        _const_spec(wo.shape), _const_spec((1, d)), _const_spec(wa.shape), _const_spec(wg.shape),
        _const_spec(cw.shape), _const_spec(wout.shape)]
    args = [h] + list(o_list) + [wo, gain.reshape(1, d), wa, wg, cw, wout]
    if final_gain is not None:
        in_specs.append(_const_spec((1, d)))
        args.append(final_gain.reshape(1, d))
    return pl.pallas_call(
        functools.partial(_ffn_body, n_o=len(o_list), final=final_gain is not None, tm=tm, ns=ns, nchunk=nchunk),
        grid=(t_tokens // tm,),
        in_specs=in_specs, out_specs=tile,
        out_shape=jax.ShapeDtypeStruct((t_tokens, d), F32),
        scratch_shapes=[pltpu.VMEM((tm, d), BF16), pltpu.VMEM((tm, d), F32),
                        pltpu.VMEM((nchunk, 16, FF_CHUNK), F32)]
        + [pltpu.VMEM((2, tm, FF_CHUNK), F32)] * 3 + [pltpu.VMEM((tm, FF_CHUNK), BF16)] * 3,
        compiler_params=_cparams(("arbitrary",)),
        name="attn_out_ffn",
    )(*args)


def _pad_heads(w, g, width):
    d = w.shape[0]
    w3 = w.reshape(d, g, HEAD_DIM)
    return jnp.pad(w3, ((0, 0), (0, 0), (0, width - HEAD_DIM))).reshape(d, g * width)


def _rot_heads(w, g):
    d = w.shape[0]
    w3 = w.reshape(d, g, HEAD_DIM)
    return jnp.concatenate([-w3[..., HALF:], w3[..., :HALF]], axis=-1).reshape(d, g * HEAD_DIM)


def _rope_k_weights(w, g):
    return _pad_heads(w, g, LANES)


def _rope_tables(seq):
    inv = jnp.float32(ROPE_THETA) ** (-jnp.arange(HALF, dtype=F32) / HALF)

    def cs(pos):
        ang = pos.astype(F32)[:, None] * inv[None, :]
        return jnp.cos(ang), jnp.sin(ang)

    def nat(c):
        return jnp.concatenate([c, c, jnp.zeros((c.shape[0], LANES - HEAD_DIM), F32)], axis=1)

    cos, sin = cs(jnp.arange(seq))
    qscale = HEAD_DIM ** -0.5 * LOG2E
    cc, sc = cs(jnp.arange(seq // CMP_STRIDE) * CMP_STRIDE + CMP_BLOCK - 1)
    return {"cn": nat(cos), "sn": nat(sin), "ct": (cos * qscale).T, "st": (sin * qscale).T,
            "cc": nat(cc), "sc": nat(sc)}


def _ffn_weights(w_in, conv_w, conv_b, w_out):
    d, two_ff = w_in.shape
    dff = two_ff // 2
    assert conv_w.shape[0] == CONV_WIDTH and dff % FF_CHUNK == 0
    nchunk = dff // FF_CHUNK

    def chunks(w):
        return w.reshape(d, nchunk, FF_CHUNK).transpose(1, 0, 2).astype(BF16)

    wa, wg = chunks(w_in[:, :dff]), chunks(w_in[:, dff:])
    taps = jnp.concatenate([conv_w, conv_b[None, :]], axis=0)
    cw = jnp.concatenate([taps[:, :dff].reshape(4, nchunk, FF_CHUNK), taps[:, dff:].reshape(4, nchunk, FF_CHUNK)],
                         axis=0).transpose(1, 0, 2)
    return wa, wg, cw, w_out.reshape(nchunk, FF_CHUNK, d).astype(BF16)


def _nsa_attention(h, gain, w_in, cmp_pos, cmp_w1, cmp_w2, tabs, emat, batch, seq):
    g, r = A_KV_HEADS, N_HEADS // A_KV_HEADS
    d = h.shape[1]
    kvw = g * HEAD_DIM
    nq = N_HEADS * HEAD_DIM
    wq, wkc, wvc, wks, wvs, wkw, wvw, wgl = jnp.split(
        w_in, [nq, nq + kvw, nq + 2 * kvw, nq + 3 * kvw, nq + 4 * kvw, nq + 5 * kvw, nq + 6 * kvw], axis=1)
    wn = jnp.concatenate([_rope_k_weights(wks, g), _rope_k_weights(wkw, g), wkc, wvc], axis=1).astype(BF16)
    wgl = wgl.reshape(d, N_HEADS, 3).transpose(0, 2, 1).reshape(d, 3 * N_HEADS)
    wt = jnp.concatenate([wq, wgl, _pad_heads(wvs, g, VROWS), _pad_heads(wvw, g, VROWS)], axis=1).T.astype(BF16)
    kpad = g * LANES
    nat_plan = [("rope", 0, kpad), ("rope", kpad, kpad), ("heads", 2 * kpad, kvw), ("heads", 2 * kpad + kvw, kvw)]
    ng = 3 * N_HEADS
    tr_plan = [("ropeq", 0, nq), ("sigmoid", nq, ng), ("vaug", nq + ng, g * VROWS),
               ("vaug", nq + ng + g * VROWS, g * VROWS)]
    ks, kw, kc, vc, qt, gates, vst, vwt = _proj_call(
        h, gain, tabs, wn, wt, nat_plan, tr_plan, [BF16, BF16, F32, F32], batch, seq)
    ks = ks.reshape(batch, seq, kpad)
    kw = kw.reshape(batch, seq, kpad)
    gates = gates.reshape(batch, 3, g, r, seq)

    nseg = seq // CMP_STRIDE
    nb = seq // SEL_BLOCK
    seg_f = CMP_STRIDE * HEAD_DIM

    def segs(x):
        return x.reshape(batch, g, nseg, seg_f)

    pos = cmp_pos.reshape(2, 2, 1, seg_f)
    w1 = cmp_w1.reshape(2, 2, seg_f, cmp_w1.shape[-1]).astype(BF16)
    w2k = jnp.stack([_pad_heads(cmp_w2[0], 1, LANES), _pad_heads(_rot_heads(cmp_w2[0], 1), 1, LANES)]).astype(BF16)
    w2v = _pad_heads(cmp_w2[1], 1, VROWS).T.astype(BF16)
    kcc, vcc = _compress_call(segs(kc), segs(vc), pos, w1, w2k, w2v, tabs["cc"], tabs["sc"])
    ch = min(CMP_CHUNK, nb)
    kcc = kcc.reshape(batch, g, nb // ch, ch, 4, LANES).transpose(0, 1, 2, 4, 3, 5).reshape(batch, g, nseg, LANES)
    vcc = vcc.reshape(batch, g, VROWS, nb // ch, ch, 4).transpose(0, 1, 2, 3, 5, 4).reshape(batch, g, VROWS, nseg)

    o_c, mb = _cmp_call(qt, kcc, vcc, gates, batch, seq)
    o_sw = _attn_call("sel", qt, ks, vst, batch, seq, g, gates=gates, branch=1, mb=mb, emat=emat,
                      kw=kw, vwt=vwt, window=WINDOW_A)
    t_tokens = batch * seq
    return [o.reshape(t_tokens, nq) for o in (o_c, o_sw)]


def kernel(x, norm_attn, norm_ffn, a_w_in, a_cmp_pos, a_cmp_w1, a_cmp_w2, a_w_out, kv_norm, b_w_kv, b_w_q, b_sinks,
           b_w_out, ffn_w_in, ffn_conv_w, ffn_conv_b, ffn_w_out, final_norm):
    batch, seq, d = x.shape
    depth = norm_attn.shape[0]
    n_a = a_w_in.shape[0]
    tabs = _rope_tables(seq)
    mr = min(MASK_ROWS, seq // SEL_BLOCK)
    emat = (jnp.arange(mr * SEL_BLOCK)[:, None] // SEL_BLOCK == jnp.arange(mr)[None, :]).astype(BF16)
    h = x.reshape(batch * seq, d)
    k_sh = v_sh = None
    gb, rb = B_KV_HEADS, N_HEADS // B_KV_HEADS
    for layer in range(depth):
        if layer < n_a:
            o_list = _nsa_attention(h, norm_attn[layer], a_w_in[layer], a_cmp_pos[layer], a_cmp_w1[layer],
                                    a_cmp_w2[layer], tabs, emat, batch, seq)
            wo = a_w_out[layer]
        else:
            j = layer - n_a
            if k_sh is None:
                wk, wv = jnp.split(b_w_kv, 2, axis=1)
                kpad = gb * LANES
                k_sh, v_sh = _proj_call(
                    h, kv_norm, tabs, _rope_k_weights(wk, gb).astype(BF16), _pad_heads(wv, gb, VROWS).T.astype(BF16),
                    [("rope", 0, kpad)], [("vaug", 0, gb * VROWS)], [BF16], batch, seq)
                k_sh = k_sh.reshape(batch, seq, kpad)
            (qt,) = _proj_call(h, norm_attn[layer], tabs, None, b_w_q[j].T.astype(BF16), [],
                               [("ropeq", 0, N_HEADS * HEAD_DIM)], [], batch, seq)
            sinks = jnp.broadcast_to((b_sinks[j] * LOG2E).reshape(gb, 1, rb, 1),
                                     (gb, 1, rb, TQ_WINDOW)).reshape(gb, 1, rb * TQ_WINDOW)
            o = _attn_call("swa", qt, k_sh, v_sh, batch, seq, gb, sinks=sinks, window=WINDOW_B)
            o_list = [o.reshape(batch * seq, N_HEADS * HEAD_DIM)]
            wo = b_w_out[j]
        wa, wg, cw, wout = _ffn_weights(ffn_w_in[layer], ffn_conv_w[layer], ffn_conv_b[layer], ffn_w_out[layer])
        h = _ffn_call(h, o_list, wo.astype(BF16), norm_ffn[layer], wa, wg, cw, wout, seq,
                      final_gain=final_norm if layer == depth - 1 else None)
    return h.reshape(batch, seq, d)
```
